```python
import jax, jax.numpy as jnp
from jax import lax
import numpy as np

D_MODEL = 1024
BATCH = 32
SEQ = 2048
DEPTH = 1
DEC_BATCH = 32
DEC_SEQ = 16
PAST_LEN = 4096

CHUNK = 64
WINDOW = 128
WIN_CHUNKS = WINDOW // CHUNK
HEAD_DIM = 64
ATT_HEADS = 8
ATT_KV_HEADS = 2
ATT_GROUP = ATT_HEADS // ATT_KV_HEADS
ATT_SCALE = HEAD_DIM ** -0.5
ROPE_THETA = 10000.0
RWKV_HEADS = 8
RWKV_N = 64
RWKV_W = RWKV_HEADS * RWKV_N
DECAY_LORA = 64
AAA_LORA = 64
GATE_LORA = 160
D_FF = 4 * D_MODEL
ATT_Q = ATT_HEADS * HEAD_DIM
ATT_KV = ATT_KV_HEADS * HEAD_DIM
ATT_COLS = ATT_Q + 2 * ATT_KV
RWKV_COLS = 3 * RWKV_W + DECAY_LORA + AAA_LORA + GATE_LORA
IN_COLS = ATT_COLS + RWKV_COLS
MIX_W = ATT_Q + RWKV_W
RWKV_SPLITS = (RWKV_W, 2 * RWKV_W, 3 * RWKV_W, 3 * RWKV_W + DECAY_LORA, 3 * RWKV_W + DECAY_LORA + AAA_LORA)
NORM_EPS = 1e-6
GN_EPS = 64e-5

kernel_name = 'hymba_swa_sink_rwkv7_stream_step'


def _rms(x, g, eps=NORM_EPS):
    xf = x.astype(jnp.float32)
    y = xf * lax.rsqrt(jnp.mean(xf * xf, axis=-1, keepdims=True) + eps)
    return (y * g.astype(jnp.float32)).astype(x.dtype)


def _rope(x, pos):
    half = HEAD_DIM // 2
    inv = ROPE_THETA ** (-jnp.arange(half, dtype=jnp.float32) / half)
    ang = pos.astype(jnp.float32)[:, None] * inv[None, :]
    cos = jnp.cos(ang)[:, None, :]
    sin = jnp.sin(ang)[:, None, :]
    xf = x.astype(jnp.float32)
    x1, x2 = xf[..., :half], xf[..., half:]
    return jnp.concatenate([x1 * cos - x2 * sin, x2 * cos + x1 * sin], axis=-1).astype(x.dtype)


def _sink_probs(s, sink):
    m = jnp.maximum(jnp.max(s, axis=-1), sink)
    p = jnp.exp(s - m[..., None])
    denom = jnp.sum(p, axis=-1) + jnp.exp(sink - m)
    return p / denom[..., None]


def _swa_prompt(q, k, v, sinks):
    B, S = q.shape[:2]
    nc = S // CHUNK
    span = (WIN_CHUNKS + 1) * CHUNK
    qb = q.reshape(B, nc, CHUNK, ATT_KV_HEADS, ATT_GROUP, HEAD_DIM)
    pad = ((0, 0), (WIN_CHUNKS * CHUNK, 0), (0, 0), (0, 0))
    kp = jnp.pad(k, pad).reshape(B, nc + WIN_CHUNKS, CHUNK, ATT_KV_HEADS, HEAD_DIM)
    vp = jnp.pad(v, pad).reshape(B, nc + WIN_CHUNKS, CHUNK, ATT_KV_HEADS, HEAD_DIM)
    kb = jnp.concatenate([kp[:, j:j + nc] for j in range(WIN_CHUNKS + 1)], axis=2)
    vb = jnp.concatenate([vp[:, j:j + nc] for j in range(WIN_CHUNKS + 1)], axis=2)
    s = jnp.einsum('bnqkgd,bnskd->bnkgqs', qb, kb, preferred_element_type=jnp.float32) * ATT_SCALE
    key_chunk = jnp.arange(nc)[:, None] + jnp.arange(span)[None, :] // CHUNK - WIN_CHUNKS
    s = jnp.where((key_chunk >= 0)[None, :, None, None, None, :], s, -jnp.inf)
    sink = sinks.reshape(ATT_KV_HEADS, ATT_GROUP)[:, :, None].astype(jnp.float32)
    pr = _sink_probs(s, sink)
    o = jnp.einsum('bnkgqs,bnskd->bnqkgd', pr.astype(vb.dtype), vb, preferred_element_type=jnp.float32)
    return o.reshape(B, S, ATT_Q).astype(q.dtype)


def _swa_sample(q, k_all, v_all, sinks):
    B, T = q.shape[:2]
    qg = q.reshape(B, T, ATT_KV_HEADS, ATT_GROUP, HEAD_DIM)
    s = jnp.einsum('btkgd,bskd->bkgts', qg, k_all, preferred_element_type=jnp.float32) * ATT_SCALE
    sink = sinks.reshape(ATT_KV_HEADS, ATT_GROUP)[:, :, None].astype(jnp.float32)
    pr = _sink_probs(s, sink)
    o = jnp.einsum('bkgts,bskd->btkgd', pr.astype(v_all.dtype), v_all, preferred_element_type=jnp.float32)
    return o.reshape(B, T, ATT_Q).astype(q.dtype)


def _wkv_scan(r, w, k, v, a, b, s0):
    def step(S, inp):
        r_t, w_t, k_t, v_t, a_t, b_t = inp
        sa = jnp.einsum('bhij,bhj->bhi', S, a_t)
        S = S * w_t[:, :, None, :] + sa[..., None] * b_t[:, :, None, :] + v_t[..., None] * k_t[:, :, None, :]
        y = jnp.einsum('bhij,bhj->bhi', S, r_t)
        return S, y
    xs = tuple(jnp.moveaxis(t, 1, 0) for t in (r, w, k, v, a, b))
    s_last, ys = lax.scan(step, s0.astype(jnp.float32), xs)
    return jnp.moveaxis(ys, 0, 1), s_last


def _rwkv(p_rw, shift_prev, wkv0, shift_mu, decay_w0, decay_w2, iclr_a0, iclr_a2, gate_g2,
          k_k, k_a, r_k, lnx_g, lnx_b):
    B, T = p_rw.shape[:2]
    f32 = jnp.float32
    prev = jnp.concatenate([shift_prev[:, None, :].astype(p_rw.dtype), p_rw[:, :-1]], axis=1)
    xs = p_rw + (prev - p_rw) * shift_mu
    r, k, v, wd, ad, gd = jnp.split(xs, RWKV_SPLITS, axis=-1)
    w = -jax.nn.softplus(-(decay_w0 + jnp.tanh(wd) @ decay_w2)) - 0.5
    a = jax.nn.sigmoid(iclr_a0 + ad @ iclr_a2)
    g = jax.nn.sigmoid(gd) @ gate_g2
    heads = lambda t: t.reshape(B, T, RWKV_HEADS, RWKV_N).astype(f32)
    kk = heads(k * k_k)
    kk = kk / jnp.maximum(jnp.sqrt(jnp.sum(kk * kk, axis=-1, keepdims=True)), 1e-12)
    k = k * (1 + (a - 1) * k_a)
    rh, kh, vh, ah = heads(r), heads(k), heads(v), heads(a)
    decay = jnp.exp(-jnp.exp(heads(w)))
    y, wkv = _wkv_scan(rh, decay, kh, vh, -kk, kk * ah, wkv0)
    mu = jnp.mean(y, axis=-1, keepdims=True)
    var = jnp.mean(jnp.square(y - mu), axis=-1, keepdims=True)
    y = ((y - mu) * lax.rsqrt(var + GN_EPS)).reshape(B, T, RWKV_W) * lnx_g.astype(f32) + lnx_b.astype(f32)
    bonus = jnp.sum(rh * kh * r_k.astype(f32), axis=-1, keepdims=True) * vh
    y = (y + bonus.reshape(B, T, RWKV_W)) * g.astype(f32)
    return y.astype(p_rw.dtype), wkv, p_rw[:, -1]


def _layer(x, pos, k_past, v_past, shift_prev, wkv0, ln1_g, w_in, q_norm_g, k_norm_g, attn_sinks,
           shift_mu, decay_w0, decay_w2, iclr_a0, iclr_a2, gate_g2, k_k, k_a, r_k, lnx_g, lnx_b,
           w_out, ln2_g, w_up, w_down):
    B, T, _ = x.shape
    h = _rms(x, ln1_g)
    p = h @ w_in
    q, k, v, p_rw = jnp.split(p, (ATT_Q, ATT_Q + ATT_KV, ATT_COLS), axis=-1)
    q = _rope(_rms(q.reshape(B, T, ATT_HEADS, HEAD_DIM), q_norm_g), pos)
    k = _rope(_rms(k.reshape(B, T, ATT_KV_HEADS, HEAD_DIM), k_norm_g), pos)
    v = v.reshape(B, T, ATT_KV_HEADS, HEAD_DIM)
    if k_past is None:
        att = _swa_prompt(q, k, v, attn_sinks)
        rows = min(WINDOW, T)
        new_k, new_v = k[:, T - rows:], v[:, T - rows:]
    else:
        k_all = jnp.concatenate([k_past.astype(k.dtype), k], axis=1)
        v_all = jnp.concatenate([v_past.astype(v.dtype), v], axis=1)
        att = _swa_sample(q, k_all, v_all, attn_sinks)
        new_k, new_v = k, v
    rw, wkv, shift_last = _rwkv(p_rw, shift_prev, wkv0, shift_mu, decay_w0, decay_w2, iclr_a0, iclr_a2,
                                gate_g2, k_k, k_a, r_k, lnx_g, lnx_b)
    x = x + jnp.concatenate([att, rw], axis=-1) @ w_out
    u = jax.nn.relu(_rms(x, ln2_g) @ w_up)
    x = x + (u * u) @ w_down
    return x, new_k, new_v, wkv, shift_last


def setup_inputs(seed: int = 0) -> dict:
    key = jax.random.key(seed)
    ks = iter(jax.random.split(key, 32))
    nrm = lambda shape, scale: jax.random.normal(next(ks), shape, jnp.float32) * scale
    L = DEPTH
    kv_rows = min(WINDOW, PAST_LEN)
    return {
        'x_prompt': nrm((BATCH, SEQ, D_MODEL), 1.0),
        'x_sample': nrm((DEC_BATCH, DEC_SEQ, D_MODEL), 1.0),
        'cache_attn_k': nrm((L, DEC_BATCH, kv_rows, ATT_KV_HEADS, HEAD_DIM), 1.0),
        'cache_attn_v': nrm((L, DEC_BATCH, kv_rows, ATT_KV_HEADS, HEAD_DIM), 1.0),
        'state_rwkv_wkv': nrm((L, DEC_BATCH, RWKV_HEADS, RWKV_N, RWKV_N), 0.3),
        'state_rwkv_shift': nrm((L, DEC_BATCH, RWKV_COLS), 1.0),
        'ln1_g': 1.0 + nrm((L, D_MODEL), 0.02),
        'w_in': nrm((L, D_MODEL, IN_COLS), D_MODEL ** -0.5),
        'q_norm_g': 1.0 + nrm((L, HEAD_DIM), 0.02),
        'k_norm_g': 1.0 + nrm((L, HEAD_DIM), 0.02),
        'attn_sinks': nrm((L, ATT_HEADS), 0.5),
        'shift_mu': jax.random.uniform(next(ks), (L, RWKV_COLS), jnp.float32),
        'decay_w0': -1.0 + nrm((L, RWKV_W), 0.5),
        'decay_w2': nrm((L, DECAY_LORA, RWKV_W), 0.5 * DECAY_LORA ** -0.5),
        'iclr_a0': nrm((L, RWKV_W), 0.3),
        'iclr_a2': nrm((L, AAA_LORA, RWKV_W), 0.5 * AAA_LORA ** -0.5),
        'gate_g2': nrm((L, GATE_LORA, RWKV_W), GATE_LORA ** -0.5),
        'k_k': 0.85 + nrm((L, RWKV_W), 0.02),
        'k_a': 1.0 + nrm((L, RWKV_W), 0.02),
        'r_k': nrm((L, RWKV_HEADS, RWKV_N), 0.1),
        'lnx_g': 1.0 + nrm((L, RWKV_W), 0.02),
        'lnx_b': nrm((L, RWKV_W), 0.02),
        'w_out': nrm((L, MIX_W, D_MODEL), MIX_W ** -0.5),
        'ln2_g': 1.0 + nrm((L, D_MODEL), 0.02),
        'w_up': nrm((L, D_MODEL, D_FF), D_MODEL ** -0.5),
        'w_down': nrm((L, D_FF, D_MODEL), D_FF ** -0.5),
    }


def reference(x_prompt, x_sample, cache_attn_k, cache_attn_v, state_rwkv_wkv, state_rwkv_shift,
              ln1_g, w_in, q_norm_g, k_norm_g, attn_sinks, shift_mu, decay_w0, decay_w2, iclr_a0, iclr_a2,
              gate_g2, k_k, k_a, r_k, lnx_g, lnx_b, w_out, ln2_g, w_up, w_down):
    Bp, Tp = x_prompt.shape[:2]
    Bs, Ts = x_sample.shape[:2]
    pos_p = jnp.arange(Tp)
    pos_s = PAST_LEN + jnp.arange(Ts)
    hp, hs = x_prompt, x_sample
    p_k, p_v, p_wkv, p_shift = [], [], [], []
    s_k, s_v, s_wkv, s_shift = [], [], [], []
    for l in range(DEPTH):
        lw = (ln1_g[l], w_in[l], q_norm_g[l], k_norm_g[l], attn_sinks[l], shift_mu[l], decay_w0[l],
              decay_w2[l], iclr_a0[l], iclr_a2[l], gate_g2[l], k_k[l], k_a[l], r_k[l], lnx_g[l], lnx_b[l],
              w_out[l], ln2_g[l], w_up[l], w_down[l])
        zero_shift = jnp.zeros((Bp, RWKV_COLS), hp.dtype)
        zero_wkv = jnp.zeros((Bp, RWKV_HEADS, RWKV_N, RWKV_N), jnp.float32)
        hp, a1, a2, a3, a4 = _layer(hp, pos_p, None, None, zero_shift, zero_wkv, *lw)
        hs, b1, b2, b3, b4 = _layer(hs, pos_s, cache_attn_k[l], cache_attn_v[l], state_rwkv_shift[l],
                                    state_rwkv_wkv[l], *lw)
        p_k.append(a1); p_v.append(a2); p_wkv.append(a3); p_shift.append(a4)
        s_k.append(b1); s_v.append(b2); s_wkv.append(b3); s_shift.append(b4)
    return (hp, hs, jnp.stack(p_k), jnp.stack(p_v), jnp.stack(p_wkv), jnp.stack(p_shift),
            jnp.stack(s_k), jnp.stack(s_v), jnp.stack(s_wkv), jnp.stack(s_shift))
```

```python
import functools

import jax
import jax.numpy as jnp
import numpy as np
from jax import lax
from jax.experimental import pallas as pl
from jax.experimental.pallas import tpu as pltpu

F32 = jnp.float32
BF16 = jnp.bfloat16

CHUNK = 64
WINDOW = 128
HEAD_DIM = 64
ATT_HEADS = 8
ATT_KV_HEADS = 2
RWKV_HEADS = 8
RWKV_W = 512
ATT_Q = 512
ATT_KV = 128
LORA_WA = 128
GATE_PAD = 256
RW_COLS = 3 * RWKV_W + LORA_WA + GATE_PAD
RW_REAL = 3 * RWKV_W + 64 + 64 + 160
IN_COLS_PAD = ATT_Q + 2 * ATT_KV + RW_COLS
PAST_LEN = 4096
ROPE_THETA = 10000.0
ATT_SCALE = HEAD_DIM ** -0.5
NORM_EPS = 1e-6
GN_EPS = 64e-5
LANES = 128
VMEM_LIMIT = 52 * 1024 * 1024

NN = (((1,), (0,)), ((), ()))
NT = (((1,), (1,)), ((), ()))
BNN = (((2,), (1,)), ((0,), (0,)))
BNT = (((2,), (2,)), ((0,), (0,)))
BTN = (((1,), (1,)), ((0,), (0,)))


def _dot(a, b, dims=NN):
    return lax.dot_general(a.astype(BF16), b.astype(BF16), dims, preferred_element_type=F32)


def _split2(x):
    hi = x.astype(BF16)
    lo = (x - hi.astype(F32)).astype(BF16)
    return hi, lo


def _seg_sum(x, ones_bd):
    outs = []
    for j in range(x.shape[1] // LANES):
        hi, lo = _split2(x[:, LANES * j:LANES * (j + 1)])
        outs.append(lax.dot_general(hi, ones_bd, NN, preferred_element_type=F32)
                    + lax.dot_general(lo, ones_bd, NN, preferred_element_type=F32))
    return outs[0] if len(outs) == 1 else jnp.concatenate(outs, axis=1)


def _ones_bd():
    r = lax.broadcasted_iota(jnp.int32, (LANES, LANES), 0)
    c = lax.broadcasted_iota(jnp.int32, (LANES, LANES), 1)
    return jnp.where((r < HEAD_DIM) == (c < HEAD_DIM), 1.0, 0.0).astype(BF16)


def _lane_lo(shape):
    return lax.broadcasted_iota(jnp.int32, shape, len(shape) - 1) < HEAD_DIM


def _inproj_kernel(x_ref, g1_ref, w_ref, qg_ref, kg_ref, cos_ref, sa_ref, sb_ref,
                   q_ref, k_ref, v_ref, rw_ref):
    x = x_ref[...]
    ms = jnp.mean(x * x, axis=-1, keepdims=True)
    h = (x * lax.rsqrt(ms + NORM_EPS) * g1_ref[...]).astype(BF16)
    ones_bd = _ones_bd()
    cos = cos_ref[...]
    sin_a = sa_ref[...]
    sin_b = sb_ref[...]

    def norm_rope(p, g):
        ss = _seg_sum(p * p, ones_bd)
        y = p * lax.rsqrt(ss * (1.0 / HEAD_DIM) + NORM_EPS) * g
        outs = []
        for j in range(p.shape[1] // LANES):
            yb = y[:, LANES * j:LANES * (j + 1)]
            outs.append(yb * cos + pltpu.roll(yb, LANES - 32, 1) * sin_a + pltpu.roll(yb, 32, 1) * sin_b)
        return outs[0] if len(outs) == 1 else jnp.concatenate(outs, axis=1)

    pq = lax.dot_general(h, w_ref[:, 0:ATT_Q], NN, preferred_element_type=F32)
    q_ref[...] = norm_rope(pq, qg_ref[...])
    pk = lax.dot_general(h, w_ref[:, ATT_Q:ATT_Q + ATT_KV], NN, preferred_element_type=F32)
    k_ref[...] = norm_rope(pk, kg_ref[...])
    v_ref[...] = lax.dot_general(h, w_ref[:, ATT_Q + ATT_KV:ATT_Q + 2 * ATT_KV], NN,
                                 preferred_element_type=F32)
    rw_ref[...] = lax.dot_general(h, w_ref[:, ATT_Q + 2 * ATT_KV:], NN, preferred_element_type=F32)


def _inproj(x2d, g1, w_in_p, qg, kg, cos_t, sin_a, sin_b, tm):
    n, d = x2d.shape
    tab_blocks = cos_t.shape[0] // tm
    row = lambda i: (i, 0)
    const = lambda i: (0, 0)
    tab = lambda i: (i % tab_blocks, 0)
    return pl.pallas_call(
        _inproj_kernel,
        grid=(n // tm,),
        in_specs=[
            pl.BlockSpec((tm, d), row),
            pl.BlockSpec((1, d), const),
            pl.BlockSpec((d, IN_COLS_PAD), const),
            pl.BlockSpec((1, ATT_Q), const),
            pl.BlockSpec((1, ATT_KV), const),
            pl.BlockSpec((tm, LANES), tab),
            pl.BlockSpec((tm, LANES), tab),
            pl.BlockSpec((tm, LANES), tab),
        ],
        out_specs=[
            pl.BlockSpec((tm, ATT_Q), row),
            pl.BlockSpec((tm, ATT_KV), row),
            pl.BlockSpec((tm, ATT_KV), row),
            pl.BlockSpec((tm, RW_COLS), row),
        ],
        out_shape=[
            jax.ShapeDtypeStruct((n, ATT_Q), F32),
            jax.ShapeDtypeStruct((n, ATT_KV), F32),
            jax.ShapeDtypeStruct((n, ATT_KV), F32),
            jax.ShapeDtypeStruct((n, RW_COLS), F32),
        ],
        compiler_params=pltpu.CompilerParams(
            dimension_semantics=("parallel",), vmem_limit_bytes=VMEM_LIMIT),
        name="inproj",
    )(x2d, g1, w_in_p, qg, kg, cos_t, sin_a, sin_b)


def _attend(q, kc, vc, sink_col, first_key_pos):
    tq = q.shape[0]
    nk = kc.shape[0]
    lo = _lane_lo((tq, LANES))
    blocks = [q[:, LANES * j:LANES * (j + 1)] for j in range(ATT_HEADS // 2)]
    stacked = jnp.concatenate([jnp.where(lo, b, 0.0) for b in blocks]
                              + [jnp.where(lo, 0.0, b) for b in blocks], axis=0)
    s = _dot(stacked, kc, NT) * ATT_SCALE
    if first_key_pos is not None:
        kpos = lax.broadcasted_iota(jnp.int32, s.shape, 1) + first_key_pos
        s = jnp.where(kpos >= 0, s, -jnp.inf)
    m = jnp.maximum(jnp.max(s, axis=-1, keepdims=True), sink_col)
    p = jnp.exp(s - m)
    denom = jnp.sum(p, axis=-1, keepdims=True) + jnp.exp(sink_col - m)
    o = _dot(p, vc, NN) * (1.0 / denom)
    half = (ATT_HEADS // 2) * tq
    outs = [jnp.where(lo, o[tq * j:tq * (j + 1)], o[half + tq * j:half + tq * (j + 1)])
            for j in range(ATT_HEADS // 2)]
    return jnp.concatenate(outs, axis=1)


def _attn_prompt_kernel(q_ref, k_ref, v_ref, kh_ref, vh_ref, sink_ref, o_ref, *, tq):
    i = pl.program_id(1)
    kcat = jnp.concatenate([kh_ref[...], k_ref[...]], axis=0)
    vcat = jnp.concatenate([vh_ref[...], v_ref[...]], axis=0)
    sink_col = sink_ref[...]
    span = WINDOW + CHUNK
    for c in range(tq // CHUNK):
        first_key_pos = i * tq + c * CHUNK - WINDOW
        o_ref[CHUNK * c:CHUNK * (c + 1), :] = _attend(
            q_ref[CHUNK * c:CHUNK * (c + 1), :], kcat[CHUNK * c:CHUNK * c + span],
            vcat[CHUNK * c:CHUNK * c + span], sink_col, first_key_pos)


def _attn_prompt(q, k, v, sink_col, batch, seq, tq):
    nt = seq // tq
    row = lambda b, i: (b * nt + i, 0)
    halo = lambda b, i: (jnp.maximum((b * nt + i) * (tq // WINDOW) - 1, 0), 0)
    return pl.pallas_call(
        functools.partial(_attn_prompt_kernel, tq=tq),
        grid=(batch, nt),
        in_specs=[
            pl.BlockSpec((tq, ATT_Q), row),
            pl.BlockSpec((tq, ATT_KV), row),
            pl.BlockSpec((tq, ATT_KV), row),
            pl.BlockSpec((WINDOW, ATT_KV), halo),
            pl.BlockSpec((WINDOW, ATT_KV), halo),
            pl.BlockSpec((ATT_HEADS * CHUNK, 1), lambda b, i: (0, 0)),
        ],
        out_specs=pl.BlockSpec((tq, ATT_Q), row),
        out_shape=jax.ShapeDtypeStruct((batch * seq, ATT_Q), F32),
        compiler_params=pltpu.CompilerParams(
            dimension_semantics=("parallel", "parallel"), vmem_limit_bytes=VMEM_LIMIT),
        name="attn_prompt",
    )(q, k, v, k, v, sink_col)


def _attn_sample_kernel(q_ref, k_ref, v_ref, kc_ref, vc_ref, sink_ref, o_ref):
    kall = jnp.concatenate([kc_ref[...], k_ref[...]], axis=0)
    vall = jnp.concatenate([vc_ref[...], v_ref[...]], axis=0)
    o_ref[...] = _attend(q_ref[...], kall, vall, sink_ref[...], None)


def _attn_sample(q, k, v, k_cache, v_cache, sink_col, batch, t):
    rows = k_cache.shape[0] // batch
    row = lambda b: (b, 0)
    return pl.pallas_call(
        _attn_sample_kernel,
        grid=(batch,),
        in_specs=[
            pl.BlockSpec((t, ATT_Q), row),
            pl.BlockSpec((t, ATT_KV), row),
            pl.BlockSpec((t, ATT_KV), row),
            pl.BlockSpec((rows, ATT_KV), row),
            pl.BlockSpec((rows, ATT_KV), row),
            pl.BlockSpec((ATT_HEADS * t, 1), lambda b: (0, 0)),
        ],
        out_specs=pl.BlockSpec((t, ATT_Q), row),
        out_shape=jax.ShapeDtypeStruct((batch * t, ATT_Q), F32),
        compiler_params=pltpu.CompilerParams(dimension_semantics=("parallel",)),
        name="attn_sample",
    )(q, k, v, k_cache, v_cache, sink_col)


def _pairs(x):
    return jnp.stack([x[:, LANES * j:LANES * (j + 1)] for j in range(RWKV_HEADS // 2)])


def _bdot(a, b, dims):
    return lax.dot_general(a.astype(BF16), b.astype(BF16), dims, preferred_element_type=F32)


def _wkv_chunk(s_bd, r, k, v, lw, a, b):
    c = CHUNK
    ri = lax.broadcasted_iota(jnp.int32, (c, c), 0)
    ci = lax.broadcasted_iota(jnp.int32, (c, c), 1)
    tri = jnp.where(ri >= ci, 1.0, 0.0).astype(BF16)
    hi = lw.astype(BF16)
    rem = lw - hi.astype(F32)
    mid = rem.astype(BF16)
    low = (rem - mid.astype(F32)).astype(BF16)
    cum = (lax.dot_general(tri, hi, NN, preferred_element_type=F32)
           + lax.dot_general(tri, mid, NN, preferred_element_type=F32)
           + lax.dot_general(tri, low, NN, preferred_element_type=F32))
    e = jnp.exp(cum)
    e_inv = jnp.exp(-cum)
    e_x = jnp.exp(cum - lw)
    e_c = e[c - 1:c]
    rt = r * e
    at = a * e_x
    kt = k * e_inv
    bt = b * e_inv
    kh = kt * e_c
    bh = bt * e_c

    lo = _lane_lo((1, 1, LANES))
    at_p, rt_p, kt_p, bt_p, v_p, kh_p, bh_p = map(_pairs, (at, rt, kt, bt, v, kh, bh))
    ec_p = _pairs(e_c)
    at0 = jnp.where(lo, at_p, 0.0)
    at1 = jnp.where(lo, 0.0, at_p)
    rt0 = jnp.where(lo, rt_p, 0.0)
    rt1 = jnp.where(lo, 0.0, rt_p)
    lhs = jnp.concatenate([at0, at1, rt0, rt1], axis=1)
    rhs = jnp.concatenate([bt_p, kt_p], axis=1)
    g = _bdot(lhs, rhs, BNT)

    r128 = lax.broadcasted_iota(jnp.int32, (1, LANES, LANES), 1)
    l128 = lax.broadcasted_iota(jnp.int32, (1, LANES, LANES), 2)
    t_idx = r128 & (c - 1)
    s_idx = l128 & (c - 1)
    ga = jnp.where(s_idx < t_idx, g[:, :LANES], 0.0)
    gr = jnp.where(s_idx <= t_idx, g[:, LANES:], 0.0)
    same = (r128 < c) == (l128 < c)
    ga_sw = jnp.concatenate(
        [ga[:, :c], jnp.stack([pltpu.roll(ga[j, c:], c, 1) for j in range(ga.shape[0])])], axis=1)
    a_bd = jnp.where(same, ga_sw, 0.0)

    eye = jnp.where(r128 == l128, 1.0, 0.0)
    t_inv = eye + a_bd
    pw = a_bd
    for _ in range(5):
        pw = _bdot(pw, pw, BNN)
        t_inv = t_inv + _bdot(t_inv, pw, BNN)

    zv = jnp.concatenate([jnp.zeros_like(v_p), v_p], axis=1)
    xak = _bdot(ga, zv, BNN)
    row_lo = r128 < c
    xak = jnp.where(row_lo == (l128 < c), xak, 0.0)
    z = jnp.concatenate([jnp.concatenate([at0, at1], axis=1), xak], axis=2)
    tz = _bdot(t_inv, z, BNN)
    tzs = tz[:, :c] + tz[:, c:]
    w_mat = tzs[:, :, :LANES]
    uv = tzs[:, :, LANES:]

    wr = jnp.concatenate([w_mat, rt_p], axis=1)
    ws = _bdot(wr, s_bd, BNT)
    u = ws[:, :c] + uv
    uvs = jnp.concatenate([u, v_p], axis=1)
    y2 = _bdot(gr, uvs, BNN)
    y = ws[:, c:] + jnp.where(lo, y2[:, :c], y2[:, c:])
    bk = jnp.concatenate([bh_p, kh_p], axis=1)
    upd = _bdot(uvs, bk, BTN)
    s_new = s_bd * ec_p + jnp.where(same, upd, 0.0)
    y_flat = jnp.concatenate([y[j] for j in range(y.shape[0])], axis=1)
    return y_flat, s_new


def _rwkv_kernel(p_ref, sprev_ref, s0_ref, mu_ref, w0_ref, w2_ref, a0_ref, a2_ref, g2_ref,
                 kk_ref, ka_ref, rk_ref, lng_ref, lnb_ref,
                 o_ref, sout_ref, shout_ref, s_scr, carry_scr, *, tt, valid):
    i = pl.program_id(1)

    @pl.when(i == 0)
    def _():
        s_scr[...] = s0_ref[...]
        carry_scr[0:1, :] = sprev_ref[...]

    p = p_ref[...]
    rows = lax.broadcasted_iota(jnp.int32, (tt, 1), 0)
    prev = jnp.where(rows == 0, carry_scr[0:1, :], pltpu.roll(p, 1, 0))
    xs = p + (prev - p) * mu_ref[...]
    last_row = valid - 1 if valid < tt else tt - 1
    carry_scr[0:1, :] = p[last_row:last_row + 1, :]

    r = xs[:, 0:RWKV_W]
    k = xs[:, RWKV_W:2 * RWKV_W]
    v = xs[:, 2 * RWKV_W:3 * RWKV_W]
    wa = xs[:, 3 * RWKV_W:3 * RWKV_W + LORA_WA]
    gd = xs[:, 3 * RWKV_W + LORA_WA:]

    zw = w0_ref[...] + _dot(jnp.tanh(wa), w2_ref[...])
    nz = -zw
    softplus = jnp.maximum(nz, 0.0) + jnp.log(1.0 + jnp.exp(-jnp.abs(nz)))
    w = -softplus - 0.5
    lw = -jnp.exp(w)
    a_ic = 1.0 / (1.0 + jnp.exp(-(a0_ref[...] + _dot(wa, a2_ref[...]))))
    gate = _dot(1.0 / (1.0 + jnp.exp(-gd)), g2_ref[...])

    ones_bd = _ones_bd()
    kk = k * kk_ref[...]
    kk = kk / jnp.maximum(jnp.sqrt(_seg_sum(kk * kk, ones_bd)), 1e-12)
    kmod = k * (1.0 + (a_ic - 1.0) * ka_ref[...])
    a_vec = -kk
    b_vec = kk * a_ic
    if valid < tt:
        ok = rows < valid
        zero = lambda t: jnp.where(ok, t, 0.0)
        r, kmod, v, lw, a_vec, b_vec = map(zero, (r, kmod, v, lw, a_vec, b_vec))

    s_bd = s_scr[...]
    ys = []
    for c in range(tt // CHUNK):
        sl = slice(CHUNK * c, CHUNK * (c + 1))
        y_c, s_bd = _wkv_chunk(s_bd, r[sl], kmod[sl], v[sl], lw[sl], a_vec[sl], b_vec[sl])
        ys.append(y_c)
    s_scr[...] = s_bd
    y = ys[0] if len(ys) == 1 else jnp.concatenate(ys, axis=0)

    inv_n = 1.0 / HEAD_DIM
    mean = _seg_sum(y, ones_bd) * inv_n
    d = y - mean
    var = _seg_sum(d * d, ones_bd) * inv_n
    yn = d * lax.rsqrt(var + GN_EPS) * lng_ref[...] + lnb_ref[...]
    bonus = _seg_sum(r * kmod * rk_ref[...], ones_bd) * v
    o_ref[...] = (yn + bonus) * gate
    sout_ref[...] = s_bd
    shout_ref[...] = carry_scr[0:1, :]


def _rwkv(prw, shift_prev, s0_bd, weights, batch, seq, tt, valid):
    nt = seq // tt
    row = lambda b, i: (b * nt + i, 0)
    const = lambda b, i: (0, 0)
    per_b3 = lambda b, i: (b, 0, 0)
    per_b4 = lambda b, i: (b, 0, 0, 0)
    w_specs = [pl.BlockSpec(w.shape, const) for w in weights]
    return pl.pallas_call(
        functools.partial(_rwkv_kernel, tt=tt, valid=valid),
        grid=(batch, nt),
        in_specs=[
            pl.BlockSpec((tt, RW_COLS), row),
            pl.BlockSpec((None, 1, RW_COLS), per_b3),
            pl.BlockSpec((None, RWKV_HEADS // 2, LANES, LANES), per_b4),
        ] + w_specs,
        out_specs=[
            pl.BlockSpec((tt, RWKV_W), row),
            pl.BlockSpec((None, RWKV_HEADS // 2, LANES, LANES), per_b4),
            pl.BlockSpec((None, 1, RW_COLS), per_b3),
        ],
        out_shape=[
            jax.ShapeDtypeStruct((batch * seq, RWKV_W), F32),
            jax.ShapeDtypeStruct((batch, RWKV_HEADS // 2, LANES, LANES), F32),
            jax.ShapeDtypeStruct((batch, 1, RW_COLS), F32),
        ],
        scratch_shapes=[
            pltpu.VMEM((RWKV_HEADS // 2, LANES, LANES), F32),
            pltpu.VMEM((8, RW_COLS), F32),
        ],
        compiler_params=pltpu.CompilerParams(
            dimension_semantics=("parallel", "arbitrary"), vmem_limit_bytes=VMEM_LIMIT),
        name="rwkv",
    )(prw, shift_prev, s0_bd, *weights)


def _ffn_kernel(x_ref, att_ref, rw_ref, woa_ref, wor_ref, g2_ref, wup_ref, wdn_ref,
                o_ref, h_scr):
    j = pl.program_id(1)

    @pl.when(j == 0)
    def _():
        x2 = x_ref[...] + _dot(att_ref[...], woa_ref[...]) + _dot(rw_ref[...], wor_ref[...])
        ms = jnp.mean(x2 * x2, axis=-1, keepdims=True)
        h_scr[...] = (x2 * lax.rsqrt(ms + NORM_EPS) * g2_ref[...]).astype(BF16)
        o_ref[...] = x2

    u = jnp.maximum(lax.dot_general(h_scr[...], wup_ref[...], NN, preferred_element_type=F32), 0.0)
    o_ref[...] += _dot(u * u, wdn_ref[...])


def _out_ffn(x2d, att, rw, wo_att, wo_rw, g2, w_up, w_down, tm, tf):
    n, d = x2d.shape
    dff = w_up.shape[1]
    row = lambda i, j: (i, 0)
    const = lambda i, j: (0, 0)
    return pl.pallas_call(
        _ffn_kernel,
        grid=(n // tm, dff // tf),
        in_specs=[
            pl.BlockSpec((tm, d), row),
            pl.BlockSpec((tm, ATT_Q), row),
            pl.BlockSpec((tm, RWKV_W), row),
            pl.BlockSpec((ATT_Q, d), const),
            pl.BlockSpec((RWKV_W, d), const),
            pl.BlockSpec((1, d), const),
            pl.BlockSpec((d, tf), lambda i, j: (0, j)),
            pl.BlockSpec((tf, d), lambda i, j: (j, 0)),
        ],
        out_specs=pl.BlockSpec((tm, d), row),
        out_shape=jax.ShapeDtypeStruct((n, d), F32),
        scratch_shapes=[pltpu.VMEM((tm, d), BF16)],
        compiler_params=pltpu.CompilerParams(
            dimension_semantics=("parallel", "arbitrary"), vmem_limit_bytes=VMEM_LIMIT),
        name="out_ffn",
    )(x2d, att, rw, wo_att, wo_rw, g2, w_up, w_down)


def _rope_tables(pos):
    half = HEAD_DIM // 2
    inv = ROPE_THETA ** (-jnp.arange(half, dtype=F32) / half)
    ang = pos.astype(F32)[:, None] * inv[None, :]
    cos = jnp.cos(ang)
    sin = jnp.sin(ang)
    zero = jnp.zeros_like(sin)
    tile = lambda t: jnp.concatenate([t, t], axis=1)
    return (tile(jnp.concatenate([cos, cos], axis=1)),
            tile(jnp.concatenate([-sin, zero], axis=1)),
            tile(jnp.concatenate([zero, sin], axis=1)))


def _pair_perm():
    idx = []
    for j in range(ATT_HEADS // 2):
        idx += list(range(HEAD_DIM * j, HEAD_DIM * (j + 1)))
        idx += list(range(HEAD_DIM * (j + 4), HEAD_DIM * (j + 5)))
    return np.asarray(idx, dtype=np.int32)


def _state_to_bd(wkv):
    b = wkv.shape[0]
    s = wkv.reshape(b, RWKV_HEADS // 2, 2, HEAD_DIM, HEAD_DIM)
    z = jnp.zeros_like(s[:, :, 0])
    top = jnp.concatenate([s[:, :, 0], z], axis=-1)
    bot = jnp.concatenate([z, s[:, :, 1]], axis=-1)
    return jnp.concatenate([top, bot], axis=-2)


def _bd_to_state(s_bd):
    b = s_bd.shape[0]
    h0 = s_bd[:, :, :HEAD_DIM, :HEAD_DIM]
    h1 = s_bd[:, :, HEAD_DIM:, HEAD_DIM:]
    return jnp.stack([h0, h1], axis=2).reshape(b, RWKV_HEADS, HEAD_DIM, HEAD_DIM)


def _pad_rw(t):
    return jnp.pad(t, [(0, 0)] * (t.ndim - 1) + [(0, RW_COLS - RW_REAL)])


def _layer(x, pos_tab_rows, k_past, v_past, shift_prev, wkv0, lw, tm):
    (g1, w_in_p, qg, kg, sink8, rw_weights, wo_att, wo_rw, g2, w_up, w_down) = lw
    b, t, d = x.shape
    n = b * t
    x2d = x.reshape(n, d)
    cos_t, sin_a, sin_b = pos_tab_rows
    q, k, v, prw = _inproj(x2d, g1, w_in_p, qg, kg, cos_t, sin_a, sin_b, tm)

    if k_past is None:
        sink_col = jnp.repeat(sink8, CHUNK)[:, None]
        att = _attn_prompt(q, k, v, sink_col, b, t, min(t, 512))
        rows = min(WINDOW, t)
        new_k = k.reshape(b, t, ATT_KV_HEADS, HEAD_DIM)[:, t - rows:]
        new_v = v.reshape(b, t, ATT_KV_HEADS, HEAD_DIM)[:, t - rows:]
        t_pad, prw_c = t, prw
    else:
        sink_col = jnp.repeat(sink8, t)[:, None]
        att = _attn_sample(q, k, v, k_past.reshape(-1, ATT_KV), v_past.reshape(-1, ATT_KV),
                           sink_col, b, t)
        new_k = k.reshape(b, t, ATT_KV_HEADS, HEAD_DIM)
        new_v = v.reshape(b, t, ATT_KV_HEADS, HEAD_DIM)
        t_pad = -(-t // CHUNK) * CHUNK
        prw_c = jnp.pad(prw.reshape(b, t, RW_COLS), ((0, 0), (0, t_pad - t), (0, 0))).reshape(-1, RW_COLS)

    tt = min(t_pad, 256)
    rw, s_bd, shift_out = _rwkv(prw_c, _pad_rw(shift_prev)[:, None, :], _state_to_bd(wkv0),
                                rw_weights, b, t_pad, tt, t if t < t_pad else t_pad)
    if t_pad != t:
        rw = rw.reshape(b, t_pad, RWKV_W)[:, :t].reshape(n, RWKV_W)
    y = _out_ffn(x2d, att, rw, wo_att, wo_rw, g2, w_up, w_down, tm, 1024)
    return (y.reshape(b, t, d), new_k, new_v, _bd_to_state(s_bd), shift_out[:, 0, :RW_REAL])


def kernel(x_prompt, x_sample, cache_attn_k, cache_attn_v, state_rwkv_wkv, state_rwkv_shift, ln1_g, w_in, q_norm_g, k_norm_g, attn_sinks, shift_mu, decay_w0, decay_w2, iclr_a0, iclr_a2, gate_g2, k_k, k_a, r_k, lnx_g, lnx_b, w_out, ln2_g, w_up, w_down):
    bp, tp, d = x_prompt.shape
    bs, ts, _ = x_sample.shape
    depth = w_in.shape[0]
    perm = _pair_perm()
    tabs_p = _rope_tables(jnp.arange(tp))
    tabs_s = tuple(jnp.tile(t, (bs, 1)) for t in _rope_tables(PAST_LEN + jnp.arange(ts)))
    tm_p = min(512, bp * tp)
    tm_s = min(512, bs * ts)

    hp, hs = x_prompt, x_sample
    outs_p, outs_s = [], []
    for l in range(depth):
        wl = w_in[l]
        att_cols = ATT_Q + 2 * ATT_KV
        w_in_p = jnp.concatenate(
            [wl[:, :ATT_Q][:, perm], wl[:, ATT_Q:att_cols], _pad_rw(wl[:, att_cols:])], axis=1).astype(BF16)
        row2 = lambda t: t.reshape(1, -1)
        zeros64 = jnp.zeros((64, RWKV_W), F32)
        rw_weights = (
            row2(_pad_rw(shift_mu[l])),
            row2(decay_w0[l]),
            jnp.concatenate([decay_w2[l], zeros64], axis=0).astype(BF16),
            row2(iclr_a0[l]),
            jnp.concatenate([zeros64, iclr_a2[l]], axis=0).astype(BF16),
            jnp.pad(gate_g2[l], ((0, GATE_PAD - gate_g2.shape[1]), (0, 0))).astype(BF16),
            row2(k_k[l]), row2(k_a[l]), row2(r_k[l]), row2(lnx_g[l]), row2(lnx_b[l]),
        )
        lw = (row2(ln1_g[l]), w_in_p, row2(jnp.tile(q_norm_g[l], ATT_HEADS)),
              row2(jnp.tile(k_norm_g[l], ATT_KV_HEADS)), attn_sinks[l], rw_weights,
              w_out[l][:ATT_Q][perm].astype(BF16), w_out[l][ATT_Q:].astype(BF16),
              row2(ln2_g[l]), w_up[l].astype(BF16), w_down[l].astype(BF16))
        zero_shift = jnp.zeros((bp, RW_REAL), F32)
        zero_wkv = jnp.zeros((bp, RWKV_HEADS, HEAD_DIM, HEAD_DIM), F32)
        hp, *op = _layer(hp, tabs_p, None, None, zero_shift, zero_wkv, lw, tm_p)
        hs, *os_ = _layer(hs, tabs_s, cache_attn_k[l], cache_attn_v[l], state_rwkv_shift[l],
                          state_rwkv_wkv[l], lw, tm_s)
        outs_p.append(op)
        outs_s.append(os_)
    stack = lambda outs, i: jnp.stack([o[i] for o in outs])
    return (hp, hs,
            stack(outs_p, 0), stack(outs_p, 1), stack(outs_p, 2), stack(outs_p, 3),
            stack(outs_s, 0), stack(outs_s, 1), stack(outs_s, 2), stack(outs_s, 3))
```

```python
import functools

import jax
import jax.numpy as jnp
import numpy as np
from jax import lax
from jax.experimental import pallas as pl
from jax.experimental.pallas import tpu as pltpu

F32 = jnp.float32
BF16 = jnp.bfloat16

CHUNK = 64
WINDOW = 128
HEAD_DIM = 64
ATT_HEADS = 8
ATT_KV_HEADS = 2
RWKV_HEADS = 8
RWKV_W = 512
ATT_Q = 512
ATT_KV = 128
LORA_WA = 128
GATE_PAD = 256
RW_COLS = 3 * RWKV_W + LORA_WA + GATE_PAD
RW_REAL = 3 * RWKV_W + 64 + 64 + 160
IN_COLS_PAD = ATT_Q + 2 * ATT_KV + RW_COLS
PAST_LEN = 4096
ROPE_THETA = 10000.0
ATT_SCALE = HEAD_DIM ** -0.5
NORM_EPS = 1e-6
GN_EPS = 64e-5
LANES = 128
VMEM_LIMIT = 52 * 1024 * 1024

NN = (((1,), (0,)), ((), ()))
NT = (((1,), (1,)), ((), ()))
BNN = (((2,), (1,)), ((0,), (0,)))
BNT = (((2,), (2,)), ((0,), (0,)))
BTN = (((1,), (1,)), ((0,), (0,)))


def _dot(a, b, dims=NN):
    return lax.dot_general(a.astype(BF16), b.astype(BF16), dims, preferred_element_type=F32)


def _split2(x):
    hi = x.astype(BF16)
    lo = (x - hi.astype(F32)).astype(BF16)
    return hi, lo


def _seg_sum(x, ones_bd):
    outs = []
    for j in range(x.shape[1] // LANES):
        hi, lo = _split2(x[:, LANES * j:LANES * (j + 1)])
        outs.append(lax.dot_general(hi, ones_bd, NN, preferred_element_type=F32)
                    + lax.dot_general(lo, ones_bd, NN, preferred_element_type=F32))
    return outs[0] if len(outs) == 1 else jnp.concatenate(outs, axis=1)


def _ones_bd():
    r = lax.broadcasted_iota(jnp.int32, (LANES, LANES), 0)
    c = lax.broadcasted_iota(jnp.int32, (LANES, LANES), 1)
    return jnp.where((r < HEAD_DIM) == (c < HEAD_DIM), 1.0, 0.0).astype(BF16)


def _lane_lo(shape):
    return lax.broadcasted_iota(jnp.int32, shape, len(shape) - 1) < HEAD_DIM


def _inproj_kernel(x_ref, g1_ref, w_ref, qg_ref, kg_ref, cos_ref, sa_ref, sb_ref,
                   q_ref, k_ref, v_ref, rw_ref):
    x = x_ref[...]
    ms = jnp.mean(x * x, axis=-1, keepdims=True)
    h = (x * lax.rsqrt(ms + NORM_EPS) * g1_ref[...]).astype(BF16)
    ones_bd = _ones_bd()
    cos = cos_ref[...]
    sin_a = sa_ref[...]
    sin_b = sb_ref[...]

    def norm_rope(p, g):
        ss = _seg_sum(p * p, ones_bd)
        y = p * lax.rsqrt(ss * (1.0 / HEAD_DIM) + NORM_EPS) * g
        outs = []
        for j in range(p.shape[1] // LANES):
            yb = y[:, LANES * j:LANES * (j + 1)]
            outs.append(yb * cos + pltpu.roll(yb, LANES - 32, 1) * sin_a + pltpu.roll(yb, 32, 1) * sin_b)
        return outs[0] if len(outs) == 1 else jnp.concatenate(outs, axis=1)

    pq = lax.dot_general(h, w_ref[:, 0:ATT_Q], NN, preferred_element_type=F32)
    q_ref[...] = norm_rope(pq, qg_ref[...])
    pk = lax.dot_general(h, w_ref[:, ATT_Q:ATT_Q + ATT_KV], NN, preferred_element_type=F32)
    k_ref[...] = norm_rope(pk, kg_ref[...])
    v_ref[...] = lax.dot_general(h, w_ref[:, ATT_Q + ATT_KV:ATT_Q + 2 * ATT_KV], NN,
                                 preferred_element_type=F32)
    rw_ref[...] = lax.dot_general(h, w_ref[:, ATT_Q + 2 * ATT_KV:], NN, preferred_element_type=F32)


def _inproj(x2d, g1, w_in_p, qg, kg, cos_t, sin_a, sin_b, tm):
    n, d = x2d.shape
    tab_blocks = cos_t.shape[0] // tm
    row = lambda i: (i, 0)
    const = lambda i: (0, 0)
    tab = lambda i: (i % tab_blocks, 0)
    return pl.pallas_call(
        _inproj_kernel,
        grid=(n // tm,),
        in_specs=[
            pl.BlockSpec((tm, d), row),
            pl.BlockSpec((1, d), const),
            pl.BlockSpec((d, IN_COLS_PAD), const),
            pl.BlockSpec((1, ATT_Q), const),
            pl.BlockSpec((1, ATT_KV), const),
            pl.BlockSpec((tm, LANES), tab),
            pl.BlockSpec((tm, LANES), tab),
            pl.BlockSpec((tm, LANES), tab),
        ],
        out_specs=[
            pl.BlockSpec((tm, ATT_Q), row),
            pl.BlockSpec((tm, ATT_KV), row),
            pl.BlockSpec((tm, ATT_KV), row),
            pl.BlockSpec((tm, RW_COLS), row),
        ],
        out_shape=[
            jax.ShapeDtypeStruct((n, ATT_Q), F32),
            jax.ShapeDtypeStruct((n, ATT_KV), F32),
            jax.ShapeDtypeStruct((n, ATT_KV), F32),
            jax.ShapeDtypeStruct((n, RW_COLS), F32),
        ],
        compiler_params=pltpu.CompilerParams(
            dimension_semantics=("parallel",), vmem_limit_bytes=VMEM_LIMIT),
        name="inproj",
    )(x2d, g1, w_in_p, qg, kg, cos_t, sin_a, sin_b)


def _attend(q, kc, vc, sink_tile, first_key_pos):
    tq = q.shape[0]
    nk = kc.shape[0]
    pad = 2 * LANES - nk
    lo = _lane_lo((tq, LANES))
    blocks = [q[:, LANES * j:LANES * (j + 1)] * ATT_SCALE for j in range(ATT_HEADS // 2)]
    stacked = jnp.concatenate([jnp.where(lo, b, 0.0) for b in blocks]
                              + [jnp.where(lo, 0.0, b) for b in blocks], axis=0).astype(BF16)
    zpad = jnp.zeros((pad, LANES), BF16)
    k_ext = jnp.concatenate([kc, zpad], axis=0)
    v_ones = jnp.concatenate([jnp.concatenate([vc, zpad], axis=0),
                              jnp.ones((2 * LANES, LANES), BF16)], axis=1)
    s = lax.dot_general(stacked, k_ext, NT, preferred_element_type=F32)
    lane = lax.broadcasted_iota(jnp.int32, (ATT_HEADS * tq, LANES), 1)
    s0 = s[:, :LANES]
    s1 = s[:, LANES:]
    if first_key_pos is not None:
        s0 = jnp.where(lane + first_key_pos >= 0, s0, -jnp.inf)
        s1 = jnp.where(lane + (first_key_pos + LANES) >= 0, s1, -jnp.inf)
    s1 = jnp.where(lane < nk - LANES, s1, sink_tile)
    m = jnp.max(jnp.maximum(s0, s1), axis=-1, keepdims=True)
    p = jnp.concatenate([jnp.exp(s0 - m), jnp.exp(s1 - m)], axis=1).astype(BF16)
    od = lax.dot_general(p, v_ones, NN, preferred_element_type=F32)
    o = od[:, :LANES] / od[:, LANES:]
    half = (ATT_HEADS // 2) * tq
    outs = [jnp.where(lo, o[tq * j:tq * (j + 1)], o[half + tq * j:half + tq * (j + 1)])
            for j in range(ATT_HEADS // 2)]
    return jnp.concatenate(outs, axis=1)


def _attn_prompt_kernel(q_ref, k_ref, v_ref, kh_ref, vh_ref, sink_ref, o_ref, *, tq):
    i = pl.program_id(1)
    kcat = jnp.concatenate([kh_ref[...], k_ref[...]], axis=0).astype(BF16)
    vcat = jnp.concatenate([vh_ref[...], v_ref[...]], axis=0).astype(BF16)
    sink_tile = sink_ref[...]
    span = WINDOW + CHUNK
    for c in range(tq // CHUNK):
        first_key_pos = i * tq + c * CHUNK - WINDOW if c < WINDOW // CHUNK else None
        o_ref[CHUNK * c:CHUNK * (c + 1), :] = _attend(
            q_ref[CHUNK * c:CHUNK * (c + 1), :], kcat[CHUNK * c:CHUNK * c + span],
            vcat[CHUNK * c:CHUNK * c + span], sink_tile, first_key_pos)


def _sink_tile(sinks, rows_per_head, nk):
    lane = jnp.arange(LANES)[None, :]
    col = jnp.repeat(sinks.astype(F32), rows_per_head)[:, None]
    return jnp.where(lane == nk - LANES, col, jnp.where(lane > nk - LANES, -jnp.inf, 0.0))


def _attn_prompt(q, k, v, sinks, batch, seq, tq):
    sink_tile = _sink_tile(sinks, CHUNK, WINDOW + CHUNK)
    nt = seq // tq
    row = lambda b, i: (b * nt + i, 0)
    halo = lambda b, i: (jnp.maximum((b * nt + i) * (tq // WINDOW) - 1, 0), 0)
    return pl.pallas_call(
        functools.partial(_attn_prompt_kernel, tq=tq),
        grid=(batch, nt),
        in_specs=[
            pl.BlockSpec((tq, ATT_Q), row),
            pl.BlockSpec((tq, ATT_KV), row),
            pl.BlockSpec((tq, ATT_KV), row),
            pl.BlockSpec((WINDOW, ATT_KV), halo),
            pl.BlockSpec((WINDOW, ATT_KV), halo),
            pl.BlockSpec((ATT_HEADS * CHUNK, LANES), lambda b, i: (0, 0)),
        ],
        out_specs=pl.BlockSpec((tq, ATT_Q), row),
        out_shape=jax.ShapeDtypeStruct((batch * seq, ATT_Q), F32),
        compiler_params=pltpu.CompilerParams(
            dimension_semantics=("parallel", "parallel"), vmem_limit_bytes=VMEM_LIMIT),
        name="attn_prompt",
    )(q, k, v, k, v, sink_tile)


def _attn_sample_kernel(q_ref, k_ref, v_ref, kc_ref, vc_ref, sink_ref, o_ref):
    kall = jnp.concatenate([kc_ref[...], k_ref[...]], axis=0).astype(BF16)
    vall = jnp.concatenate([vc_ref[...], v_ref[...]], axis=0).astype(BF16)
    o_ref[...] = _attend(q_ref[...], kall, vall, sink_ref[...], None)


def _attn_sample(q, k, v, k_cache, v_cache, sinks, batch, t):
    rows = k_cache.shape[0] // batch
    sink_tile = _sink_tile(sinks, t, rows + t)
    row = lambda b: (b, 0)
    return pl.pallas_call(
        _attn_sample_kernel,
        grid=(batch,),
        in_specs=[
            pl.BlockSpec((t, ATT_Q), row),
            pl.BlockSpec((t, ATT_KV), row),
            pl.BlockSpec((t, ATT_KV), row),
            pl.BlockSpec((rows, ATT_KV), row),
            pl.BlockSpec((rows, ATT_KV), row),
            pl.BlockSpec((ATT_HEADS * t, LANES), lambda b: (0, 0)),
        ],
        out_specs=pl.BlockSpec((t, ATT_Q), row),
        out_shape=jax.ShapeDtypeStruct((batch * t, ATT_Q), F32),
        compiler_params=pltpu.CompilerParams(dimension_semantics=("parallel",)),
        name="attn_sample",
    )(q, k, v, k_cache, v_cache, sink_tile)


def _systems(x, rows):
    nc = x.shape[0] // rows
    return jnp.stack([x[rows * c:rows * (c + 1), LANES * j:LANES * (j + 1)]
                      for c in range(nc) for j in range(RWKV_HEADS // 2)])


def _bdot(a, b, dims):
    return lax.dot_general(a.astype(BF16), b.astype(BF16), dims, preferred_element_type=F32)


def _wkv_tile(s_bd, r, k, v, lw, a, b):
    c = CHUNK
    tt = r.shape[0]
    nc = tt // c
    np_ = RWKV_HEADS // 2
    ri = lax.broadcasted_iota(jnp.int32, (tt, tt), 0)
    ci = lax.broadcasted_iota(jnp.int32, (tt, tt), 1)
    tri = jnp.where((ri >= ci) & ((ri & -c) == (ci & -c)), 1.0, 0.0).astype(BF16)
    hi = lw.astype(BF16)
    rem = lw - hi.astype(F32)
    mid = rem.astype(BF16)
    low = (rem - mid.astype(F32)).astype(BF16)
    cum = (lax.dot_general(tri, hi, NN, preferred_element_type=F32)
           + lax.dot_general(tri, mid, NN, preferred_element_type=F32)
           + lax.dot_general(tri, low, NN, preferred_element_type=F32))
    e = jnp.exp(cum)
    e_inv = jnp.exp(-cum)
    e_x = jnp.exp(cum - lw)
    rt = r * e
    at = a * e_x
    kt = k * e_inv
    bt = b * e_inv

    lo = _lane_lo((1, 1, LANES))
    at_p, rt_p, kt_p, bt_p, v_p = (_systems(t, c) for t in (at, rt, kt, bt, v))
    ec_p = jnp.stack([e[c * i + c - 1:c * i + c, LANES * j:LANES * (j + 1)]
                      for i in range(nc) for j in range(np_)])
    kh_p = kt_p * ec_p
    bh_p = bt_p * ec_p
    at0 = jnp.where(lo, at_p, 0.0)
    at1 = jnp.where(lo, 0.0, at_p)
    rt0 = jnp.where(lo, rt_p, 0.0)
    rt1 = jnp.where(lo, 0.0, rt_p)
    lhs = jnp.concatenate([at0, at1, rt0, rt1], axis=1)
    rhs = jnp.concatenate([bt_p, kt_p], axis=1)
    g = _bdot(lhs, rhs, BNT)

    r128 = lax.broadcasted_iota(jnp.int32, (1, LANES, LANES), 1)
    l128 = lax.broadcasted_iota(jnp.int32, (1, LANES, LANES), 2)
    t_idx = r128 & (c - 1)
    s_idx = l128 & (c - 1)
    ga = jnp.where(s_idx < t_idx, g[:, :LANES], 0.0)
    gr = jnp.where(s_idx <= t_idx, g[:, LANES:], 0.0)
    same = (r128 < c) == (l128 < c)
    ga_sw = jnp.concatenate(
        [ga[:, :c], jnp.stack([pltpu.roll(ga[j, c:], c, 1) for j in range(ga.shape[0])])], axis=1)
    a_bd = jnp.where(same, ga_sw, 0.0)

    eye = jnp.where(r128 == l128, 1.0, 0.0)
    t_inv = eye + a_bd
    pw = a_bd
    for _ in range(5):
        pw = _bdot(pw, pw, BNN)
        t_inv = t_inv + _bdot(t_inv, pw, BNN)

    zeros = jnp.zeros_like(v_p)
    xak = _bdot(ga, jnp.concatenate([zeros, v_p], axis=1), BNN)
    xak = jnp.where(same, xak, 0.0)
    z = jnp.concatenate([jnp.concatenate([at0, at1], axis=1), xak], axis=2)
    tz = _bdot(t_inv, z, BNN)
    tzs = tz[:, :c] + tz[:, c:]
    q_mat = jnp.concatenate(
        [tzs, jnp.concatenate([zeros, v_p], axis=2)], axis=1).astype(BF16)
    gq = _bdot(gr, q_mat, BNN)
    r_eff = rt_p + jnp.where(lo, gq[:, :c, :LANES], gq[:, c:, :LANES])
    y0 = jnp.where(lo, gq[:, :c, LANES:], gq[:, c:, LANES:])
    bk = jnp.concatenate([bh_p, kh_p], axis=1)
    mn = _bdot(q_mat, bk, BTN)
    wb = jnp.where(same, mn[:, :LANES], 0.0)
    n_mat = jnp.where(same, mn[:, LANES:], 0.0)

    ys = []
    for i in range(nc):
        sl = slice(np_ * i, np_ * (i + 1))
        s_bf = s_bd.astype(BF16)
        y_i = _bdot(r_eff[sl], s_bf, BNT) + y0[sl]
        ys.append(jnp.concatenate([y_i[j] for j in range(np_)], axis=1))
        s_bd = s_bd * ec_p[sl] + _bdot(s_bf, wb[sl], BNN) + n_mat[sl]
    y = ys[0] if nc == 1 else jnp.concatenate(ys, axis=0)
    return y, s_bd


def _rwkv_kernel(p_ref, sprev_ref, s0_ref, mu_ref, w0_ref, w2_ref, a0_ref, a2_ref, g2_ref,
                 kk_ref, ka_ref, rk_ref, lng_ref, lnb_ref,
                 o_ref, sout_ref, shout_ref, s_scr, carry_scr, *, tt, valid):
    i = pl.program_id(1)

    @pl.when(i == 0)
    def _():
        s_scr[...] = s0_ref[...]
        carry_scr[0:1, :] = sprev_ref[...]

    p = p_ref[...]
    rows = lax.broadcasted_iota(jnp.int32, (tt, 1), 0)
    prev = jnp.where(rows == 0, carry_scr[0:1, :], pltpu.roll(p, 1, 0))
    xs = p + (prev - p) * mu_ref[...]
    last_row = valid - 1 if valid < tt else tt - 1
    carry_scr[0:1, :] = p[last_row:last_row + 1, :]

    r = xs[:, 0:RWKV_W]
    k = xs[:, RWKV_W:2 * RWKV_W]
    v = xs[:, 2 * RWKV_W:3 * RWKV_W]
    wa = xs[:, 3 * RWKV_W:3 * RWKV_W + LORA_WA]
    gd = xs[:, 3 * RWKV_W + LORA_WA:]

    zw = w0_ref[...] + _dot(jnp.tanh(wa), w2_ref[...])
    nz = -zw
    softplus = jnp.maximum(nz, 0.0) + jnp.log(1.0 + jnp.exp(-jnp.abs(nz)))
    w = -softplus - 0.5
    lw = -jnp.exp(w)
    a_ic = 1.0 / (1.0 + jnp.exp(-(a0_ref[...] + _dot(wa, a2_ref[...]))))
    gate = _dot(1.0 / (1.0 + jnp.exp(-gd)), g2_ref[...])

    ones_bd = _ones_bd()
    kk = k * kk_ref[...]
    kk = kk / jnp.maximum(jnp.sqrt(_seg_sum(kk * kk, ones_bd)), 1e-12)
    kmod = k * (1.0 + (a_ic - 1.0) * ka_ref[...])
    a_vec = -kk
    b_vec = kk * a_ic
    if valid < tt:
        ok = rows < valid
        zero = lambda t: jnp.where(ok, t, 0.0)
        r, kmod, v, lw, a_vec, b_vec = map(zero, (r, kmod, v, lw, a_vec, b_vec))

    y, s_bd = _wkv_tile(s_scr[...], r, kmod, v, lw, a_vec, b_vec)
    s_scr[...] = s_bd

    inv_n = 1.0 / HEAD_DIM
    mean = _seg_sum(y, ones_bd) * inv_n
    d = y - mean
    var = _seg_sum(d * d, ones_bd) * inv_n
    yn = d * lax.rsqrt(var + GN_EPS) * lng_ref[...] + lnb_ref[...]
    bonus = _seg_sum(r * kmod * rk_ref[...], ones_bd) * v
    o_ref[...] = (yn + bonus) * gate
    sout_ref[...] = s_bd
    shout_ref[...] = carry_scr[0:1, :]


def _rwkv(prw, shift_prev, s0_bd, weights, batch, seq, tt, valid):
    nt = seq // tt
    row = lambda b, i: (b * nt + i, 0)
    const = lambda b, i: (0, 0)
    per_b3 = lambda b, i: (b, 0, 0)
    per_b4 = lambda b, i: (b, 0, 0, 0)
    w_specs = [pl.BlockSpec(w.shape, const) for w in weights]
    return pl.pallas_call(
        functools.partial(_rwkv_kernel, tt=tt, valid=valid),
        grid=(batch, nt),
        in_specs=[
            pl.BlockSpec((tt, RW_COLS), row),
            pl.BlockSpec((None, 1, RW_COLS), per_b3),
            pl.BlockSpec((None, RWKV_HEADS // 2, LANES, LANES), per_b4),
        ] + w_specs,
        out_specs=[
            pl.BlockSpec((tt, RWKV_W), row),
            pl.BlockSpec((None, RWKV_HEADS // 2, LANES, LANES), per_b4),
            pl.BlockSpec((None, 1, RW_COLS), per_b3),
        ],
        out_shape=[
            jax.ShapeDtypeStruct((batch * seq, RWKV_W), F32),
            jax.ShapeDtypeStruct((batch, RWKV_HEADS // 2, LANES, LANES), F32),
            jax.ShapeDtypeStruct((batch, 1, RW_COLS), F32),
        ],
        scratch_shapes=[
            pltpu.VMEM((RWKV_HEADS // 2, LANES, LANES), F32),
            pltpu.VMEM((8, RW_COLS), F32),
        ],
        compiler_params=pltpu.CompilerParams(
            dimension_semantics=("parallel", "arbitrary"), vmem_limit_bytes=VMEM_LIMIT),
        name="rwkv",
    )(prw, shift_prev, s0_bd, *weights)


def _ffn_kernel(x_ref, att_ref, rw_ref, woa_ref, wor_ref, g2_ref, wup_ref, wdn_ref,
                o_ref, h_scr):
    j = pl.program_id(1)

    @pl.when(j == 0)
    def _():
        x2 = x_ref[...] + _dot(att_ref[...], woa_ref[...]) + _dot(rw_ref[...], wor_ref[...])
        ms = jnp.mean(x2 * x2, axis=-1, keepdims=True)
        h_scr[...] = (x2 * lax.rsqrt(ms + NORM_EPS) * g2_ref[...]).astype(BF16)
        o_ref[...] = x2

    u = jnp.maximum(lax.dot_general(h_scr[...], wup_ref[...], NN, preferred_element_type=F32), 0.0)
    o_ref[...] += _dot(u * u, wdn_ref[...])


def _out_ffn(x2d, att, rw, wo_att, wo_rw, g2, w_up, w_down, tm, tf):
    n, d = x2d.shape
    dff = w_up.shape[1]
    row = lambda i, j: (i, 0)
    const = lambda i, j: (0, 0)
    return pl.pallas_call(
        _ffn_kernel,
        grid=(n // tm, dff // tf),
        in_specs=[
            pl.BlockSpec((tm, d), row),
            pl.BlockSpec((tm, ATT_Q), row),
            pl.BlockSpec((tm, RWKV_W), row),
            pl.BlockSpec((ATT_Q, d), const),
            pl.BlockSpec((RWKV_W, d), const),
            pl.BlockSpec((1, d), const),
            pl.BlockSpec((d, tf), lambda i, j: (0, j)),
            pl.BlockSpec((tf, d), lambda i, j: (j, 0)),
        ],
        out_specs=pl.BlockSpec((tm, d), row),
        out_shape=jax.ShapeDtypeStruct((n, d), F32),
        scratch_shapes=[pltpu.VMEM((tm, d), BF16)],
        compiler_params=pltpu.CompilerParams(
            dimension_semantics=("parallel", "arbitrary"), vmem_limit_bytes=VMEM_LIMIT),
        name="out_ffn",
    )(x2d, att, rw, wo_att, wo_rw, g2, w_up, w_down)


def _rope_tables(pos):
    half = HEAD_DIM // 2
    inv = ROPE_THETA ** (-jnp.arange(half, dtype=F32) / half)
    ang = pos.astype(F32)[:, None] * inv[None, :]
    cos = jnp.cos(ang)
    sin = jnp.sin(ang)
    zero = jnp.zeros_like(sin)
    tile = lambda t: jnp.concatenate([t, t], axis=1)
    return (tile(jnp.concatenate([cos, cos], axis=1)),
            tile(jnp.concatenate([-sin, zero], axis=1)),
            tile(jnp.concatenate([zero, sin], axis=1)))


def _pair_perm():
    idx = []
    for j in range(ATT_HEADS // 2):
        idx += list(range(HEAD_DIM * j, HEAD_DIM * (j + 1)))
        idx += list(range(HEAD_DIM * (j + 4), HEAD_DIM * (j + 5)))
    return np.asarray(idx, dtype=np.int32)


def _state_to_bd(wkv):
    b = wkv.shape[0]
    s = wkv.reshape(b, RWKV_HEADS // 2, 2, HEAD_DIM, HEAD_DIM)
    z = jnp.zeros_like(s[:, :, 0])
    top = jnp.concatenate([s[:, :, 0], z], axis=-1)
    bot = jnp.concatenate([z, s[:, :, 1]], axis=-1)
    return jnp.concatenate([top, bot], axis=-2)


def _bd_to_state(s_bd):
    b = s_bd.shape[0]
    h0 = s_bd[:, :, :HEAD_DIM, :HEAD_DIM]
    h1 = s_bd[:, :, HEAD_DIM:, HEAD_DIM:]
    return jnp.stack([h0, h1], axis=2).reshape(b, RWKV_HEADS, HEAD_DIM, HEAD_DIM)


def _pad_rw(t):
    return jnp.pad(t, [(0, 0)] * (t.ndim - 1) + [(0, RW_COLS - RW_REAL)])


def _layer(x, pos_tab_rows, k_past, v_past, shift_prev, wkv0, lw, tm):
    (g1, w_in_p, qg, kg, sink8, rw_weights, wo_att, wo_rw, g2, w_up, w_down) = lw
    b, t, d = x.shape
    n = b * t
    x2d = x.reshape(n, d)
    cos_t, sin_a, sin_b = pos_tab_rows
    q, k, v, prw = _inproj(x2d, g1, w_in_p, qg, kg, cos_t, sin_a, sin_b, tm)

    if k_past is None:
        att = _attn_prompt(q, k, v, sink8, b, t, min(t, 512))
        rows = min(WINDOW, t)
        new_k = k.reshape(b, t, ATT_KV_HEADS, HEAD_DIM)[:, t - rows:]
        new_v = v.reshape(b, t, ATT_KV_HEADS, HEAD_DIM)[:, t - rows:]
        t_pad, prw_c = t, prw
    else:
        att = _attn_sample(q, k, v, k_past.reshape(-1, ATT_KV), v_past.reshape(-1, ATT_KV),
                           sink8, b, t)
        new_k = k.reshape(b, t, ATT_KV_HEADS, HEAD_DIM)
        new_v = v.reshape(b, t, ATT_KV_HEADS, HEAD_DIM)
        t_pad = -(-t // CHUNK) * CHUNK
        prw_c = jnp.pad(prw.reshape(b, t, RW_COLS), ((0, 0), (0, t_pad - t), (0, 0))).reshape(-1, RW_COLS)

    tt = min(t_pad, 256)
    rw, s_bd, shift_out = _rwkv(prw_c, _pad_rw(shift_prev)[:, None, :], _state_to_bd(wkv0),
                                rw_weights, b, t_pad, tt, t if t < t_pad else t_pad)
    if t_pad != t:
        rw = rw.reshape(b, t_pad, RWKV_W)[:, :t].reshape(n, RWKV_W)
    y = _out_ffn(x2d, att, rw, wo_att, wo_rw, g2, w_up, w_down, tm, 1024)
    return (y.reshape(b, t, d), new_k, new_v, _bd_to_state(s_bd), shift_out[:, 0, :RW_REAL])


def kernel(x_prompt, x_sample, cache_attn_k, cache_attn_v, state_rwkv_wkv, state_rwkv_shift, ln1_g, w_in, q_norm_g, k_norm_g, attn_sinks, shift_mu, decay_w0, decay_w2, iclr_a0, iclr_a2, gate_g2, k_k, k_a, r_k, lnx_g, lnx_b, w_out, ln2_g, w_up, w_down):
    bp, tp, d = x_prompt.shape
    bs, ts, _ = x_sample.shape
    depth = w_in.shape[0]
    perm = _pair_perm()
    tabs_p = _rope_tables(jnp.arange(tp))
    tabs_s = tuple(jnp.tile(t, (bs, 1)) for t in _rope_tables(PAST_LEN + jnp.arange(ts)))
    tm_p = min(512, bp * tp)
    tm_s = min(512, bs * ts)

    hp, hs = x_prompt, x_sample
    outs_p, outs_s = [], []
    for l in range(depth):
        wl = w_in[l]
        att_cols = ATT_Q + 2 * ATT_KV
        w_in_p = jnp.concatenate(
            [wl[:, :ATT_Q][:, perm], wl[:, ATT_Q:att_cols], _pad_rw(wl[:, att_cols:])], axis=1).astype(BF16)
        row2 = lambda t: t.reshape(1, -1)
        zeros64 = jnp.zeros((64, RWKV_W), F32)
        rw_weights = (
            row2(_pad_rw(shift_mu[l])),
            row2(decay_w0[l]),
            jnp.concatenate([decay_w2[l], zeros64], axis=0).astype(BF16),
            row2(iclr_a0[l]),
            jnp.concatenate([zeros64, iclr_a2[l]], axis=0).astype(BF16),
            jnp.pad(gate_g2[l], ((0, GATE_PAD - gate_g2.shape[1]), (0, 0))).astype(BF16),
            row2(k_k[l]), row2(k_a[l]), row2(r_k[l]), row2(lnx_g[l]), row2(lnx_b[l]),
        )
        lw = (row2(ln1_g[l]), w_in_p, row2(jnp.tile(q_norm_g[l], ATT_HEADS)),
              row2(jnp.tile(k_norm_g[l], ATT_KV_HEADS)), attn_sinks[l], rw_weights,
              w_out[l][:ATT_Q][perm].astype(BF16), w_out[l][ATT_Q:].astype(BF16),
              row2(ln2_g[l]), w_up[l].astype(BF16), w_down[l].astype(BF16))
        zero_shift = jnp.zeros((bp, RW_REAL), F32)
        zero_wkv = jnp.zeros((bp, RWKV_HEADS, HEAD_DIM, HEAD_DIM), F32)
        hp, *op = _layer(hp, tabs_p, None, None, zero_shift, zero_wkv, lw, tm_p)
        hs, *os_ = _layer(hs, tabs_s, cache_attn_k[l], cache_attn_v[l], state_rwkv_shift[l],
                          state_rwkv_wkv[l], lw, tm_s)
        outs_p.append(op)
        outs_s.append(os_)
    stack = lambda outs, i: jnp.stack([o[i] for o in outs])
    return (hp, hs,
            stack(outs_p, 0), stack(outs_p, 1), stack(outs_p, 2), stack(outs_p, 3),
            stack(outs_s, 0), stack(outs_s, 1), stack(outs_s, 2), stack(outs_s, 3))
```

```python
import functools

import jax
import jax.numpy as jnp
import numpy as np
from jax import lax
from jax.experimental import pallas as pl
from jax.experimental.pallas import tpu as pltpu

F32 = jnp.float32
BF16 = jnp.bfloat16

CHUNK = 64
WINDOW = 128
HEAD_DIM = 64
ATT_HEADS = 8
ATT_KV_HEADS = 2
RWKV_HEADS = 8
RWKV_W = 512
ATT_Q = 512
ATT_KV = 128
LORA_WA = 128
GATE_PAD = 256
RW_COLS = 3 * RWKV_W + LORA_WA + GATE_PAD
RW_REAL = 3 * RWKV_W + 64 + 64 + 160
IN_COLS_PAD = ATT_Q + 2 * ATT_KV + RW_COLS
PAST_LEN = 4096
ROPE_THETA = 10000.0
ATT_SCALE = HEAD_DIM ** -0.5
NORM_EPS = 1e-6
GN_EPS = 64e-5
LANES = 128
SUBLANES = 8
VMEM_LIMIT = 52 * 1024 * 1024

NN = (((1,), (0,)), ((), ()))
NT = (((1,), (1,)), ((), ()))
BNN = (((2,), (1,)), ((0,), (0,)))
BNT = (((2,), (2,)), ((0,), (0,)))
BTN = (((1,), (1,)), ((0,), (0,)))


def _dot(a, b, dims=NN):
    return lax.dot_general(a.astype(BF16), b.astype(BF16), dims, preferred_element_type=F32)


def _split2(x):
    hi = x.astype(BF16)
    lo = (x - hi.astype(F32)).astype(BF16)
    return hi, lo


def _seg_sum(x, ones_bd):
    outs = []
    for j in range(x.shape[1] // LANES):
        hi, lo = _split2(x[:, LANES * j:LANES * (j + 1)])
        outs.append(lax.dot_general(hi, ones_bd, NN, preferred_element_type=F32)
                    + lax.dot_general(lo, ones_bd, NN, preferred_element_type=F32))
    return outs[0] if len(outs) == 1 else jnp.concatenate(outs, axis=1)


def _ones_bd():
    r = lax.broadcasted_iota(jnp.int32, (LANES, LANES), 0)
    c = lax.broadcasted_iota(jnp.int32, (LANES, LANES), 1)
    return jnp.where((r < HEAD_DIM) == (c < HEAD_DIM), 1.0, 0.0).astype(BF16)


def _lane_lo(shape):
    return lax.broadcasted_iota(jnp.int32, shape, len(shape) - 1) < HEAD_DIM


_resident = functools.partial(pl.BlockSpec, pipeline_mode=pl.Buffered(1))


def _inproj_kernel(x_ref, sprev_ref, g1_ref, w_ref, qg_ref, kg_ref, cos_ref, sa_ref, sb_ref,
                   mu_ref, w0_ref, w2_ref, a0_ref, a2_ref, g2_ref, kk_ref, ka_ref,
                   q_ref, k_ref, v_ref, r_ref, km_ref, vv_ref, lw_ref, a_ref, b_ref, gate_ref,
                   tail_ref, carry_scr, *, seq):
    tm = x_ref.shape[0]
    x = x_ref[...]
    ms = jnp.mean(x * x, axis=-1, keepdims=True)
    h = (x * lax.rsqrt(ms + NORM_EPS) * g1_ref[...]).astype(BF16)
    ones_bd = _ones_bd()
    cos = cos_ref[...]
    sin_a = sa_ref[...]
    sin_b = sb_ref[...]

    def norm_rope(p, g):
        ss = _seg_sum(p * p, ones_bd)
        y = p * lax.rsqrt(ss * (1.0 / HEAD_DIM) + NORM_EPS) * g
        outs = []
        for j in range(p.shape[1] // LANES):
            yb = y[:, LANES * j:LANES * (j + 1)]
            outs.append(yb * cos + pltpu.roll(yb, LANES - 32, 1) * sin_a + pltpu.roll(yb, 32, 1) * sin_b)
        return outs[0] if len(outs) == 1 else jnp.concatenate(outs, axis=1)

    rows = lax.broadcasted_iota(jnp.int32, (tm, 1), 0)
    rw0 = ATT_Q + 2 * ATT_KV
    if seq >= tm:
        @pl.when(pl.program_id(0) % (seq // tm) == 0)
        def _():
            carry_scr[0:1, :] = sprev_ref[...]

    def shifted(c0, c1):
        p = lax.dot_general(h, w_ref[:, rw0 + c0:rw0 + c1], NN, preferred_element_type=F32)
        prev = pltpu.roll(p, 1, 0)
        if seq >= tm:
            prev = jnp.where(rows == 0, carry_scr[0:1, c0:c1], prev)
            carry_scr[0:1, c0:c1] = p[tm - 1:tm, :]
            tail_ref[:, c0:c1] = p[tm - SUBLANES:tm, :]
        else:
            for s in range(tm // seq):
                prev = jnp.where(rows == s * seq, sprev_ref[s, :, c0:c1], prev)
            tail_ref[:, c0:c1] = p
        return p + (prev - p) * mu_ref[:, c0:c1]

    lora = shifted(3 * RWKV_W, RW_COLS)
    wa = lora[:, :LORA_WA]
    gd = lora[:, LORA_WA:]
    k = shifted(RWKV_W, 2 * RWKV_W)
    r_ref[...] = shifted(0, RWKV_W)
    vv_ref[...] = shifted(2 * RWKV_W, 3 * RWKV_W)

    pq = lax.dot_general(h, w_ref[:, 0:ATT_Q], NN, preferred_element_type=F32)
    q_ref[...] = norm_rope(pq, qg_ref[...])
    pkv = lax.dot_general(h, w_ref[:, ATT_Q:ATT_Q + 2 * ATT_KV], NN, preferred_element_type=F32)
    k_ref[...] = norm_rope(pkv[:, :ATT_KV], kg_ref[...])
    v_ref[...] = pkv[:, ATT_KV:]

    zw = w0_ref[...] + _dot(jnp.tanh(wa), w2_ref[...])
    nz = -zw
    softplus = jnp.maximum(nz, 0.0) + jnp.log(1.0 + jnp.exp(-jnp.abs(nz)))
    lw_ref[...] = -jnp.exp(-softplus - 0.5)
    a_ic = 1.0 / (1.0 + jnp.exp(-(a0_ref[...] + _dot(wa, a2_ref[...]))))
    gate_ref[...] = _dot(1.0 / (1.0 + jnp.exp(-gd)), g2_ref[...])
    kk = k * kk_ref[...]
    kk = kk / jnp.maximum(jnp.sqrt(_seg_sum(kk * kk, ones_bd)), 1e-12)
    km_ref[...] = k * (1.0 + (a_ic - 1.0) * ka_ref[...])
    a_ref[...] = -kk
    b_ref[...] = kk * a_ic


def _inproj(x2d, sprev, att_w, rw_w, tabs, tm, seq):
    n, d = x2d.shape
    g1, w_in_p, qg, kg = att_w
    cos_t, sin_a, sin_b = tabs
    tab_blocks = cos_t.shape[0] // tm
    row = lambda i: (i, 0)
    const = lambda i: (0, 0)
    tab = lambda i: (i % tab_blocks, 0)
    if seq >= tm:
        tps = seq // tm
        sprev_spec = pl.BlockSpec((None, 1, RW_COLS), lambda i: (i // tps, 0, 0))
        tail_spec = pl.BlockSpec((None, SUBLANES, RW_COLS), lambda i: (i // tps, 0, 0))
        tail_shape = jax.ShapeDtypeStruct((n // seq, SUBLANES, RW_COLS), F32)
    else:
        sprev_spec = pl.BlockSpec((tm // seq, 1, RW_COLS), lambda i: (i, 0, 0))
        tail_spec = pl.BlockSpec((tm, RW_COLS), row)
        tail_shape = jax.ShapeDtypeStruct((n, RW_COLS), F32)
    wide = pl.BlockSpec((tm, RWKV_W), row)
    wide_shape = jax.ShapeDtypeStruct((n, RWKV_W), F32)
    return pl.pallas_call(
        functools.partial(_inproj_kernel, seq=seq),
        grid=(n // tm,),
        in_specs=[
            pl.BlockSpec((tm, d), row),
            sprev_spec,
            _resident((1, d), const),
            _resident((d, IN_COLS_PAD), const),
            _resident((1, ATT_Q), const),
            _resident((1, ATT_KV), const),
            pl.BlockSpec((tm, LANES), tab),
            pl.BlockSpec((tm, LANES), tab),
            pl.BlockSpec((tm, LANES), tab),
        ] + [_resident(w.shape, const) for w in rw_w],
        out_specs=[
            pl.BlockSpec((tm, ATT_Q), row),
            pl.BlockSpec((tm, ATT_KV), row),
            pl.BlockSpec((tm, ATT_KV), row),
        ] + [wide] * 7 + [tail_spec],
        out_shape=[
            jax.ShapeDtypeStruct((n, ATT_Q), F32),
            jax.ShapeDtypeStruct((n, ATT_KV), F32),
            jax.ShapeDtypeStruct((n, ATT_KV), F32),
        ] + [wide_shape] * 7 + [tail_shape],
        scratch_shapes=[pltpu.VMEM((SUBLANES, RW_COLS), F32)],
        compiler_params=pltpu.CompilerParams(
            dimension_semantics=("arbitrary",), vmem_limit_bytes=VMEM_LIMIT),
        name="inproj",
    )(x2d, sprev, g1, w_in_p, qg, kg, cos_t, sin_a, sin_b, *rw_w)


def _attend(q, kc, vc, sink_tile, first_key_pos):
    tq = q.shape[0]
    nk = kc.shape[0]
    pad = 2 * LANES - nk
    lo = _lane_lo((tq, LANES))
    blocks = [q[:, LANES * j:LANES * (j + 1)] * ATT_SCALE for j in range(ATT_HEADS // 2)]
    stacked = jnp.concatenate([jnp.where(lo, b, 0.0) for b in blocks]
                              + [jnp.where(lo, 0.0, b) for b in blocks], axis=0).astype(BF16)
    zpad = jnp.zeros((pad, LANES), BF16)
    k_ext = jnp.concatenate([kc, zpad], axis=0)
    v_ones = jnp.concatenate([jnp.concatenate([vc, zpad], axis=0),
                              jnp.ones((2 * LANES, LANES), BF16)], axis=1)
    s = lax.dot_general(stacked, k_ext, NT, preferred_element_type=F32)
    lane = lax.broadcasted_iota(jnp.int32, (ATT_HEADS * tq, LANES), 1)
    s0 = s[:, :LANES]
    s1 = s[:, LANES:]
    if first_key_pos is not None:
        s0 = jnp.where(lane + first_key_pos >= 0, s0, -jnp.inf)
        s1 = jnp.where(lane + (first_key_pos + LANES) >= 0, s1, -jnp.inf)
    s1 = jnp.where(lane < nk - LANES, s1, sink_tile)
    m = jnp.max(jnp.maximum(s0, s1), axis=-1, keepdims=True)
    p = jnp.concatenate([jnp.exp(s0 - m), jnp.exp(s1 - m)], axis=1).astype(BF16)
    od = lax.dot_general(p, v_ones, NN, preferred_element_type=F32)
    o = od[:, :LANES] / od[:, LANES:]
    half = (ATT_HEADS // 2) * tq
    outs = [jnp.where(lo, o[tq * j:tq * (j + 1)], o[half + tq * j:half + tq * (j + 1)])
            for j in range(ATT_HEADS // 2)]
    return jnp.concatenate(outs, axis=1)


def _attn_prompt_kernel(q_ref, k_ref, v_ref, kh_ref, vh_ref, sink_ref, o_ref, *, tq):
    i = pl.program_id(1)
    kcat = jnp.concatenate([kh_ref[...], k_ref[...]], axis=0).astype(BF16)
    vcat = jnp.concatenate([vh_ref[...], v_ref[...]], axis=0).astype(BF16)
    sink_tile = sink_ref[...]
    span = WINDOW + CHUNK
    for c in range(tq // CHUNK):
        first_key_pos = i * tq + c * CHUNK - WINDOW if c < WINDOW // CHUNK else None
        o_ref[CHUNK * c:CHUNK * (c + 1), :] = _attend(
            q_ref[CHUNK * c:CHUNK * (c + 1), :], kcat[CHUNK * c:CHUNK * c + span],
            vcat[CHUNK * c:CHUNK * c + span], sink_tile, first_key_pos)


def _sink_tile(sinks, rows_per_head, nk):
    lane = jnp.arange(LANES)[None, :]
    col = jnp.repeat(sinks.astype(F32), rows_per_head)[:, None]
    return jnp.where(lane == nk - LANES, col, jnp.where(lane > nk - LANES, -jnp.inf, 0.0))


def _attn_prompt(q, k, v, sinks, batch, seq, tq):
    sink_tile = _sink_tile(sinks, CHUNK, WINDOW + CHUNK)
    nt = seq // tq
    row = lambda b, i: (b * nt + i, 0)
    halo = lambda b, i: (jnp.maximum((b * nt + i) * (tq // WINDOW) - 1, 0), 0)
    return pl.pallas_call(
        functools.partial(_attn_prompt_kernel, tq=tq),
        grid=(batch, nt),
        in_specs=[
            pl.BlockSpec((tq, ATT_Q), row),
            pl.BlockSpec((tq, ATT_KV), row),
            pl.BlockSpec((tq, ATT_KV), row),
            pl.BlockSpec((WINDOW, ATT_KV), halo),
            pl.BlockSpec((WINDOW, ATT_KV), halo),
            pl.BlockSpec((ATT_HEADS * CHUNK, LANES), lambda b, i: (0, 0)),
        ],
        out_specs=pl.BlockSpec((tq, ATT_Q), row),
        out_shape=jax.ShapeDtypeStruct((batch * seq, ATT_Q), F32),
        compiler_params=pltpu.CompilerParams(
            dimension_semantics=("parallel", "parallel"), vmem_limit_bytes=VMEM_LIMIT),
        name="attn_prompt",
    )(q, k, v, k, v, sink_tile)


def _attn_sample_kernel(q_ref, k_ref, v_ref, kc_ref, vc_ref, sink_ref, o_ref):
    kall = jnp.concatenate([kc_ref[...], k_ref[...]], axis=0).astype(BF16)
    vall = jnp.concatenate([vc_ref[...], v_ref[...]], axis=0).astype(BF16)
    o_ref[...] = _attend(q_ref[...], kall, vall, sink_ref[...], None)


def _attn_sample(q, k, v, k_cache, v_cache, sinks, batch, t):
    rows = k_cache.shape[0] // batch
    sink_tile = _sink_tile(sinks, t, rows + t)
    row = lambda b: (b, 0)
    return pl.pallas_call(
        _attn_sample_kernel,
        grid=(batch,),
        in_specs=[
            pl.BlockSpec((t, ATT_Q), row),
            pl.BlockSpec((t, ATT_KV), row),
            pl.BlockSpec((t, ATT_KV), row),
            pl.BlockSpec((rows, ATT_KV), row),
            pl.BlockSpec((rows, ATT_KV), row),
            pl.BlockSpec((ATT_HEADS * t, LANES), lambda b: (0, 0)),
        ],
        out_specs=pl.BlockSpec((t, ATT_Q), row),
        out_shape=jax.ShapeDtypeStruct((batch * t, ATT_Q), F32),
        compiler_params=pltpu.CompilerParams(dimension_semantics=("parallel",)),
        name="attn_sample",
    )(q, k, v, k_cache, v_cache, sink_tile)


def _systems(x, rows):
    nc = x.shape[0] // rows
    return jnp.stack([x[rows * c:rows * (c + 1), LANES * j:LANES * (j + 1)]
                      for c in range(nc) for j in range(RWKV_HEADS // 2)])


def _bdot(a, b, dims):
    return lax.dot_general(a.astype(BF16), b.astype(BF16), dims, preferred_element_type=F32)


def _wkv_tile(s_bd, r, k, v, lw, a, b, chain):
    c = CHUNK
    tt = r.shape[0]
    nc = tt // c
    np_ = RWKV_HEADS // 2
    ri = lax.broadcasted_iota(jnp.int32, (tt, tt), 0)
    ci = lax.broadcasted_iota(jnp.int32, (tt, tt), 1)
    tri = jnp.where((ri >= ci) & ((ri & -c) == (ci & -c)), 1.0, 0.0).astype(BF16)
    hi = lw.astype(BF16)
    rem = lw - hi.astype(F32)
    mid = rem.astype(BF16)
    low = (rem - mid.astype(F32)).astype(BF16)
    cum = (lax.dot_general(tri, hi, NN, preferred_element_type=F32)
           + lax.dot_general(tri, mid, NN, preferred_element_type=F32)
           + lax.dot_general(tri, low, NN, preferred_element_type=F32))
    e = jnp.exp(cum)
    e_inv = jnp.exp(-cum)
    e_x = jnp.exp(cum - lw)
    rt = r * e
    at = a * e_x
    kt = k * e_inv
    bt = b * e_inv

    lo = _lane_lo((1, 1, LANES))
    at_p, rt_p, kt_p, bt_p, v_p = (_systems(t, c) for t in (at, rt, kt, bt, v))
    ec_p = jnp.stack([e[c * i + c - 1:c * i + c, LANES * j:LANES * (j + 1)]
                      for i in range(nc) for j in range(np_)])
    kh_p = kt_p * ec_p
    bh_p = bt_p * ec_p
    at0 = jnp.where(lo, at_p, 0.0)
    at1 = jnp.where(lo, 0.0, at_p)
    rt0 = jnp.where(lo, rt_p, 0.0)
    rt1 = jnp.where(lo, 0.0, rt_p)
    lhs = jnp.concatenate([at0, at1, rt0, rt1], axis=1)
    rhs = jnp.concatenate([bt_p, kt_p], axis=1)
    g = _bdot(lhs, rhs, BNT)

    r128 = lax.broadcasted_iota(jnp.int32, (1, LANES, LANES), 1)
    l128 = lax.broadcasted_iota(jnp.int32, (1, LANES, LANES), 2)
    t_idx = r128 & (c - 1)
    s_idx = l128 & (c - 1)
    ga = jnp.where(s_idx < t_idx, g[:, :LANES], 0.0)
    gr = jnp.where(s_idx <= t_idx, g[:, LANES:], 0.0)
    same = (r128 < c) == (l128 < c)
    ga_sw = jnp.concatenate(
        [ga[:, :c], jnp.stack([pltpu.roll(ga[j, c:], c, 1) for j in range(ga.shape[0])])], axis=1)
    a_bd = jnp.where(same, ga_sw, 0.0)

    eye = jnp.where(r128 == l128, 1.0, 0.0)
    t_inv = eye + a_bd
    pw = a_bd
    for _ in range(5):
        pw = _bdot(pw, pw, BNN)
        t_inv = t_inv + _bdot(t_inv, pw, BNN)

    zeros = jnp.zeros_like(v_p)
    xak = _bdot(ga, jnp.concatenate([zeros, v_p], axis=1), BNN)
    xak = jnp.where(same, xak, 0.0)
    z = jnp.concatenate([jnp.concatenate([at0, at1], axis=1), xak], axis=2)
    tz = _bdot(t_inv, z, BNN)
    tzs = tz[:, :c] + tz[:, c:]
    q_mat = jnp.concatenate(
        [tzs, jnp.concatenate([zeros, v_p], axis=2)], axis=1).astype(BF16)
    gq = _bdot(gr, q_mat, BNN)
    r_eff = rt_p + jnp.where(lo, gq[:, :c, :LANES], gq[:, c:, :LANES])
    y0 = jnp.where(lo, gq[:, :c, LANES:], gq[:, c:, LANES:])
    bk = jnp.concatenate([bh_p, kh_p], axis=1)
    mn = _bdot(q_mat, bk, BTN)
    wb = jnp.where(same, mn[:, :LANES], 0.0)
    n_mat = jnp.where(same, mn[:, LANES:], 0.0)

    flat = lambda y_i: jnp.concatenate([y_i[j] for j in range(np_)], axis=1)
    if chain:
        ys = []
        for i in range(nc):
            sl = slice(np_ * i, np_ * (i + 1))
            s_bf = s_bd.astype(BF16)
            ys.append(flat(_bdot(r_eff[sl], s_bf, BNT) + y0[sl]))
            s_bd = s_bd * ec_p[sl] + _bdot(s_bf, wb[sl], BNN) + n_mat[sl]
    else:
        s_bf = s_bd.astype(BF16)
        y_all = _bdot(r_eff, s_bf, BNT) + y0
        ys = [flat(y_all[np_ * i:np_ * (i + 1)]) for i in range(nc)]
        s_bd = s_bd * ec_p + _bdot(s_bf, wb, BNN) + n_mat
    y = ys[0] if nc == 1 else jnp.concatenate(ys, axis=0)
    return y, s_bd


def _rwkv_kernel(r_ref, k_ref, v_ref, lw_ref, a_ref, b_ref, gate_ref, s0_ref, rk_ref, lng_ref, lnb_ref,
                 o_ref, sout_ref, s_scr, *, chain):
    r = r_ref[...]
    k = k_ref[...]
    v = v_ref[...]
    np_ = RWKV_HEADS // 2
    if chain:
        @pl.when(pl.program_id(1) == 0)
        def _():
            s_scr[...] = s0_ref[...]
        s_in = s_scr[...]
    else:
        s_in = s0_ref[...].reshape(-1, LANES, LANES)
    y, s_out = _wkv_tile(s_in, r, k, v, lw_ref[...], a_ref[...], b_ref[...], chain)
    if chain:
        s_scr[...] = s_out
        sout_ref[...] = s_out
    else:
        sout_ref[...] = s_out.reshape(-1, np_, LANES, LANES)

    ones_bd = _ones_bd()
    inv_n = 1.0 / HEAD_DIM
    mean = _seg_sum(y, ones_bd) * inv_n
    d = y - mean
    var = _seg_sum(d * d, ones_bd) * inv_n
    yn = d * lax.rsqrt(var + GN_EPS) * lng_ref[...] + lnb_ref[...]
    bonus = _seg_sum(r * k * rk_ref[...], ones_bd) * v
    o_ref[...] = (yn + bonus) * gate_ref[...]


def _rwkv(tok, s0_bd, post_w, n_seq, seq, tt, chain):
    np_ = RWKV_HEADS // 2
    if chain:
        groups, nt, ns = n_seq, seq // tt, None
    else:
        groups, nt, ns = n_seq * seq // tt, 1, tt // seq
    row = lambda g, i: (g * nt + i, 0)
    const = lambda g, i: (0, 0)
    state_spec = pl.BlockSpec((ns, np_, LANES, LANES), lambda g, i: (g, 0, 0, 0))
    wide = pl.BlockSpec((tt, RWKV_W), row)
    return pl.pallas_call(
        functools.partial(_rwkv_kernel, chain=chain),
        grid=(groups, nt),
        in_specs=[wide] * 7 + [state_spec] + [pl.BlockSpec(w.shape, const) for w in post_w],
        out_specs=[wide, state_spec],
        out_shape=[
            jax.ShapeDtypeStruct((n_seq * seq, RWKV_W), F32),
            jax.ShapeDtypeStruct((n_seq, np_, LANES, LANES), F32),
        ],
        scratch_shapes=[pltpu.VMEM((np_, LANES, LANES), F32)],
        compiler_params=pltpu.CompilerParams(
            dimension_semantics=("parallel", "arbitrary"), vmem_limit_bytes=VMEM_LIMIT),
        name="rwkv",
    )(*tok, s0_bd, *post_w)


def _ffn_kernel(x_ref, att_ref, rw_ref, woa_ref, wor_ref, g2_ref, wup_ref, wdn_ref, o_ref, *, tf):
    x2 = x_ref[...] + _dot(att_ref[...], woa_ref[...]) + _dot(rw_ref[...], wor_ref[...])
    ms = jnp.mean(x2 * x2, axis=-1, keepdims=True)
    h = (x2 * lax.rsqrt(ms + NORM_EPS) * g2_ref[...]).astype(BF16)
    us = []
    for j in range(wup_ref.shape[1] // tf):
        u = jnp.maximum(lax.dot_general(h, wup_ref[:, tf * j:tf * (j + 1)], NN,
                                        preferred_element_type=F32), 0.0)
        us.append((u * u).astype(BF16))
    u_all = jnp.concatenate(us, axis=1)
    o_ref[...] = x2 + lax.dot_general(u_all, wdn_ref[...], NN, preferred_element_type=F32)


def _out_ffn(x2d, att, rw, wo_att, wo_rw, g2, w_up, w_down, tm, tf):
    n, d = x2d.shape
    dff = w_up.shape[1]
    row = lambda i: (i, 0)
    const = lambda i: (0, 0)
    return pl.pallas_call(
        functools.partial(_ffn_kernel, tf=tf),
        grid=(n // tm,),
        in_specs=[
            pl.BlockSpec((tm, d), row),
            pl.BlockSpec((tm, ATT_Q), row),
            pl.BlockSpec((tm, RWKV_W), row),
            _resident((ATT_Q, d), const),
            _resident((RWKV_W, d), const),
            _resident((1, d), const),
            _resident((d, dff), const),
            _resident((dff, d), const),
        ],
        out_specs=pl.BlockSpec((tm, d), row),
        out_shape=jax.ShapeDtypeStruct((n, d), F32),
        compiler_params=pltpu.CompilerParams(
            dimension_semantics=("parallel",), vmem_limit_bytes=VMEM_LIMIT),
        name="out_ffn",
    )(x2d, att, rw, wo_att, wo_rw, g2, w_up, w_down)


def _rope_tables(pos):
    half = HEAD_DIM // 2
    inv = ROPE_THETA ** (-jnp.arange(half, dtype=F32) / half)
    ang = pos.astype(F32)[:, None] * inv[None, :]
    cos = jnp.cos(ang)
    sin = jnp.sin(ang)
    zero = jnp.zeros_like(sin)
    tile = lambda t: jnp.concatenate([t, t], axis=1)
    return (tile(jnp.concatenate([cos, cos], axis=1)),
            tile(jnp.concatenate([-sin, zero], axis=1)),
            tile(jnp.concatenate([zero, sin], axis=1)))


def _pair_perm():
    idx = []
    for j in range(ATT_HEADS // 2):
        idx += list(range(HEAD_DIM * j, HEAD_DIM * (j + 1)))
        idx += list(range(HEAD_DIM * (j + 4), HEAD_DIM * (j + 5)))
    return np.asarray(idx, dtype=np.int32)


def _state_to_bd(wkv):
    b = wkv.shape[0]
    s = wkv.reshape(b, RWKV_HEADS // 2, 2, HEAD_DIM, HEAD_DIM)
    z = jnp.zeros_like(s[:, :, 0])
    top = jnp.concatenate([s[:, :, 0], z], axis=-1)
    bot = jnp.concatenate([z, s[:, :, 1]], axis=-1)
    return jnp.concatenate([top, bot], axis=-2)


def _bd_to_state(s_bd):
    b = s_bd.shape[0]
    h0 = s_bd[:, :, :HEAD_DIM, :HEAD_DIM]
    h1 = s_bd[:, :, HEAD_DIM:, HEAD_DIM:]
    return jnp.stack([h0, h1], axis=2).reshape(b, RWKV_HEADS, HEAD_DIM, HEAD_DIM)


def _pad_rw(t):
    return jnp.pad(t, [(0, 0)] * (t.ndim - 1) + [(0, RW_COLS - RW_REAL)])


def _layer(x, tabs, k_past, v_past, shift_prev, wkv0, lw, tm):
    (att_w, rw_pre_w, rw_post_w, sink8, wo_att, wo_rw, g2, w_up, w_down) = lw
    b, t, d = x.shape
    n = b * t
    x2d = x.reshape(n, d)
    q, k, v, *tok, tail = _inproj(x2d, _pad_rw(shift_prev)[:, None, :], att_w, rw_pre_w, tabs, tm, t)

    if k_past is None:
        att = _attn_prompt(q, k, v, sink8, b, t, min(t, 512))
        rows = min(WINDOW, t)
        new_k = k.reshape(b, t, ATT_KV_HEADS, HEAD_DIM)[:, t - rows:]
        new_v = v.reshape(b, t, ATT_KV_HEADS, HEAD_DIM)[:, t - rows:]
    else:
        att = _attn_sample(q, k, v, k_past.reshape(-1, ATT_KV), v_past.reshape(-1, ATT_KV),
                           sink8, b, t)
        new_k = k.reshape(b, t, ATT_KV_HEADS, HEAD_DIM)
        new_v = v.reshape(b, t, ATT_KV_HEADS, HEAD_DIM)

    if t >= tm:
        shift_out = tail[:, SUBLANES - 1, :RW_REAL]
        rw, s_bd = _rwkv(tok, _state_to_bd(wkv0), rw_post_w, b, t, min(t, 256), True)
    else:
        shift_out = tail.reshape(b, t, RW_COLS)[:, t - 1, :RW_REAL]
        pad = lambda a: jnp.pad(a.reshape(b, t, RWKV_W), ((0, 0), (0, CHUNK - t), (0, 0))).reshape(-1, RWKV_W)
        per_tile = max(m for m in (4, 2, 1) if b % m == 0)
        rw, s_bd = _rwkv([pad(a) for a in tok], _state_to_bd(wkv0), rw_post_w, b, CHUNK,
                         per_tile * CHUNK, False)
        rw = rw.reshape(b, CHUNK, RWKV_W)[:, :t].reshape(n, RWKV_W)
    y = _out_ffn(x2d, att, rw, wo_att, wo_rw, g2, w_up, w_down, tm, 1024)
    return (y.reshape(b, t, d), new_k, new_v, _bd_to_state(s_bd), shift_out)


def kernel(x_prompt, x_sample, cache_attn_k, cache_attn_v, state_rwkv_wkv, state_rwkv_shift, ln1_g, w_in, q_norm_g, k_norm_g, attn_sinks, shift_mu, decay_w0, decay_w2, iclr_a0, iclr_a2, gate_g2, k_k, k_a, r_k, lnx_g, lnx_b, w_out, ln2_g, w_up, w_down):
    bp, tp, d = x_prompt.shape
    bs, ts, _ = x_sample.shape
    depth = w_in.shape[0]
    perm = _pair_perm()
    tabs_p = _rope_tables(jnp.arange(tp))
    tabs_s = tuple(jnp.tile(t, (bs, 1)) for t in _rope_tables(PAST_LEN + jnp.arange(ts)))
    tm_p = min(512, bp * tp)
    tm_s = min(512, bs * ts)

    hp, hs = x_prompt, x_sample
    outs_p, outs_s = [], []
    for l in range(depth):
        wl = w_in[l]
        att_cols = ATT_Q + 2 * ATT_KV
        w_in_p = jnp.concatenate(
            [wl[:, :ATT_Q][:, perm], wl[:, ATT_Q:att_cols], _pad_rw(wl[:, att_cols:])], axis=1).astype(BF16)
        row2 = lambda t: t.reshape(1, -1)
        zeros64 = jnp.zeros((64, RWKV_W), F32)
        att_w = (row2(ln1_g[l]), w_in_p, row2(jnp.tile(q_norm_g[l], ATT_HEADS)),
                 row2(jnp.tile(k_norm_g[l], ATT_KV_HEADS)))
        rw_pre_w = (
            row2(_pad_rw(shift_mu[l])),
            row2(decay_w0[l]),
            jnp.concatenate([decay_w2[l], zeros64], axis=0).astype(BF16),
            row2(iclr_a0[l]),
            jnp.concatenate([zeros64, iclr_a2[l]], axis=0).astype(BF16),
            jnp.pad(gate_g2[l], ((0, GATE_PAD - gate_g2.shape[1]), (0, 0))).astype(BF16),
            row2(k_k[l]), row2(k_a[l]),
        )
        rw_post_w = (row2(r_k[l]), row2(lnx_g[l]), row2(lnx_b[l]))
        lw = (att_w, rw_pre_w, rw_post_w, attn_sinks[l],
              w_out[l][:ATT_Q][perm].astype(BF16), w_out[l][ATT_Q:].astype(BF16),
              row2(ln2_g[l]), w_up[l].astype(BF16), w_down[l].astype(BF16))
        zero_shift = jnp.zeros((bp, RW_REAL), F32)
        zero_wkv = jnp.zeros((bp, RWKV_HEADS, HEAD_DIM, HEAD_DIM), F32)
        hp, *op = _layer(hp, tabs_p, None, None, zero_shift, zero_wkv, lw, tm_p)
        hs, *os_ = _layer(hs, tabs_s, cache_attn_k[l], cache_attn_v[l], state_rwkv_shift[l],
                          state_rwkv_wkv[l], lw, tm_s)
        outs_p.append(op)
        outs_s.append(os_)
    stack = lambda outs, i: jnp.stack([o[i] for o in outs])
    return (hp, hs,
            stack(outs_p, 0), stack(outs_p, 1), stack(outs_p, 2), stack(outs_p, 3),
            stack(outs_s, 0), stack(outs_s, 1), stack(outs_s, 2), stack(outs_s, 3))
```

```python
import functools

import jax
import jax.numpy as jnp
import numpy as np
from jax import lax
from jax.experimental import pallas as pl
from jax.experimental.pallas import tpu as pltpu

F32 = jnp.float32
BF16 = jnp.bfloat16

CHUNK = 64
WINDOW = 128
HEAD_DIM = 64
ATT_HEADS = 8
ATT_KV_HEADS = 2
RWKV_HEADS = 8
RWKV_W = 512
ATT_Q = 512
ATT_KV = 128
LORA_WA = 128
GATE_PAD = 256
RW_COLS = 3 * RWKV_W + LORA_WA + GATE_PAD
RW_REAL = 3 * RWKV_W + 64 + 64 + 160
IN_COLS_PAD = ATT_Q + 2 * ATT_KV + RW_COLS
PAST_LEN = 4096
ROPE_THETA = 10000.0
ATT_SCALE = HEAD_DIM ** -0.5
NORM_EPS = 1e-6
GN_EPS = 64e-5
DECAY_SCALE = float(np.exp(-0.5))
LANES = 128
SUBLANES = 8
VMEM_LIMIT = 52 * 1024 * 1024

NN = (((1,), (0,)), ((), ()))
NT = (((1,), (1,)), ((), ()))
BNN = (((2,), (1,)), ((0,), (0,)))
BNT = (((2,), (2,)), ((0,), (0,)))
BTN = (((1,), (1,)), ((0,), (0,)))


def _dot(a, b, dims=NN):
    return lax.dot_general(a.astype(BF16), b.astype(BF16), dims, preferred_element_type=F32)


def _split2(x):
    hi = x.astype(BF16)
    lo = (x - hi.astype(F32)).astype(BF16)
    return hi, lo


def _seg_sum(x, ones_bd):
    outs = []
    for j in range(x.shape[1] // LANES):
        hi, lo = _split2(x[:, LANES * j:LANES * (j + 1)])
        outs.append(lax.dot_general(jnp.concatenate([hi, lo], axis=1), ones_bd, NN,
                                    preferred_element_type=F32))
    return outs[0] if len(outs) == 1 else jnp.concatenate(outs, axis=1)


def _ones_bd():
    r = lax.broadcasted_iota(jnp.int32, (2 * LANES, LANES), 0) & (LANES - 1)
    c = lax.broadcasted_iota(jnp.int32, (2 * LANES, LANES), 1)
    return jnp.where((r < HEAD_DIM) == (c < HEAD_DIM), 1.0, 0.0).astype(BF16)


def _lane_lo(shape):
    return lax.broadcasted_iota(jnp.int32, shape, len(shape) - 1) < HEAD_DIM


_resident = functools.partial(pl.BlockSpec, pipeline_mode=pl.Buffered(1))


def _inproj_kernel(x_ref, sprev_ref, g1_ref, w_ref, qg_ref, kg_ref, cos_ref, sa_ref, sb_ref,
                   mu_ref, w0_ref, w2_ref, a0_ref, a2_ref, g2_ref, kk_ref, ka_ref,
                   q_ref, k_ref, v_ref, r_ref, km_ref, vv_ref, lw_ref, a_ref, b_ref, gate_ref,
                   tail_ref, carry_scr, *, seq):
    tm = x_ref.shape[0]
    rows = lax.broadcasted_iota(jnp.int32, (tm, 1), 0)
    rw0 = ATT_Q + 2 * ATT_KV
    ones_bd = _ones_bd()
    x = x_ref[...]
    ms = jnp.mean(x * x, axis=-1, keepdims=True)
    h = (x * lax.rsqrt(ms + NORM_EPS) * g1_ref[...]).astype(BF16)

    def project(c0, c1):
        return lax.dot_general(h, w_ref[:, c0:c1], NN, preferred_element_type=F32)

    if seq >= tm:
        @pl.when(pl.program_id(0) % (seq // tm) == 0)
        def _():
            carry_scr[0:1, :] = sprev_ref[...]

    def shifted(c0, c1):
        p = project(rw0 + c0, rw0 + c1)
        prev = pltpu.roll(p, 1, 0)
        if seq >= tm:
            prev = jnp.where(rows == 0, carry_scr[0:1, c0:c1], prev)
            carry_scr[0:1, c0:c1] = p[tm - 1:tm, :]
            tail_ref[:, c0:c1] = p[tm - SUBLANES:tm, :]
        else:
            for s in range(tm // seq):
                prev = jnp.where(rows == s * seq, sprev_ref[s, :, c0:c1], prev)
            tail_ref[:, c0:c1] = p
        return p + (prev - p) * mu_ref[:, c0:c1]

    lora = shifted(3 * RWKV_W, RW_COLS)
    wa = lora[:, :LORA_WA]
    gd = lora[:, LORA_WA:]
    k = shifted(RWKV_W, 2 * RWKV_W)
    r_ref[...] = shifted(0, RWKV_W)
    vv_ref[...] = shifted(2 * RWKV_W, 3 * RWKV_W)

    cos = cos_ref[...]
    sin_a = sa_ref[...]
    sin_b = sb_ref[...]

    def norm_rope(p, g):
        ss = _seg_sum(p * p, ones_bd)
        y = p * lax.rsqrt(ss * (1.0 / HEAD_DIM) + NORM_EPS) * g
        outs = []
        for j in range(p.shape[1] // LANES):
            yb = y[:, LANES * j:LANES * (j + 1)]
            outs.append(yb * cos + pltpu.roll(yb, LANES - 32, 1) * sin_a + pltpu.roll(yb, 32, 1) * sin_b)
        return outs[0] if len(outs) == 1 else jnp.concatenate(outs, axis=1)

    q_ref[...] = norm_rope(project(0, ATT_Q), qg_ref[...])
    pkv = project(ATT_Q, ATT_Q + 2 * ATT_KV)
    k_ref[...] = norm_rope(pkv[:, :ATT_KV], kg_ref[...])
    v_ref[...] = pkv[:, ATT_KV:]

    zw = w0_ref[...] + _dot(jnp.tanh(wa), w2_ref[...])
    lw_ref[...] = -DECAY_SCALE / (1.0 + jnp.exp(-zw))
    a_ic = 1.0 / (1.0 + jnp.exp(-(a0_ref[...] + _dot(wa, a2_ref[...]))))
    gate_ref[...] = _dot(1.0 / (1.0 + jnp.exp(-gd)), g2_ref[...])
    kk = k * kk_ref[...]
    kk = kk / jnp.maximum(jnp.sqrt(_seg_sum(kk * kk, ones_bd)), 1e-12)
    km_ref[...] = k * (1.0 + (a_ic - 1.0) * ka_ref[...])
    a_ref[...] = -kk
    b_ref[...] = kk * a_ic


def _inproj(x2d, sprev, att_w, rw_w, tabs, tm, seq):
    n, d = x2d.shape
    g1, w_in_p, qg, kg = att_w
    cos_t, sin_a, sin_b = tabs
    tab_blocks = cos_t.shape[0] // tm
    row = lambda i: (i, 0)
    const = lambda i: (0, 0)
    tab = lambda i: (i % tab_blocks, 0)
    if seq >= tm:
        tps = seq // tm
        sprev_spec = pl.BlockSpec((None, 1, RW_COLS), lambda i: (i // tps, 0, 0))
        tail_spec = pl.BlockSpec((None, SUBLANES, RW_COLS), lambda i: (i // tps, 0, 0))
        tail_shape = jax.ShapeDtypeStruct((n // seq, SUBLANES, RW_COLS), F32)
    else:
        sprev_spec = pl.BlockSpec((tm // seq, 1, RW_COLS), lambda i: (i, 0, 0))
        tail_spec = pl.BlockSpec((tm, RW_COLS), row)
        tail_shape = jax.ShapeDtypeStruct((n, RW_COLS), F32)
    wide = pl.BlockSpec((tm, RWKV_W), row)
    wide_shape = jax.ShapeDtypeStruct((n, RWKV_W), F32)
    return pl.pallas_call(
        functools.partial(_inproj_kernel, seq=seq),
        grid=(n // tm,),
        in_specs=[
            pl.BlockSpec((tm, d), row),
            sprev_spec,
            _resident((1, d), const),
            _resident((d, IN_COLS_PAD), const),
            _resident((1, ATT_Q), const),
            _resident((1, ATT_KV), const),
            pl.BlockSpec((tm, LANES), tab),
            pl.BlockSpec((tm, LANES), tab),
            pl.BlockSpec((tm, LANES), tab),
        ] + [_resident(w.shape, const) for w in rw_w],
        out_specs=[
            pl.BlockSpec((tm, ATT_Q), row),
            pl.BlockSpec((tm, ATT_KV), row),
            pl.BlockSpec((tm, ATT_KV), row),
        ] + [wide] * 7 + [tail_spec],
        out_shape=[
            jax.ShapeDtypeStruct((n, ATT_Q), F32),
            jax.ShapeDtypeStruct((n, ATT_KV), F32),
            jax.ShapeDtypeStruct((n, ATT_KV), F32),
        ] + [wide_shape] * 7 + [tail_shape],
        scratch_shapes=[pltpu.VMEM((SUBLANES, RW_COLS), F32)],
        compiler_params=pltpu.CompilerParams(
            dimension_semantics=("arbitrary",), vmem_limit_bytes=VMEM_LIMIT),
        name="inproj",
    )(x2d, sprev, g1, w_in_p, qg, kg, cos_t, sin_a, sin_b, *rw_w)


def _attend(q, kc, vc, sink_tile, first_key_pos):
    tq = q.shape[0]
    nk = kc.shape[0]
    pad = 2 * LANES - nk
    lo = _lane_lo((tq, LANES))
    blocks = [q[:, LANES * j:LANES * (j + 1)] * ATT_SCALE for j in range(ATT_HEADS // 2)]
    stacked = jnp.concatenate([jnp.where(lo, b, 0.0) for b in blocks]
                              + [jnp.where(lo, 0.0, b) for b in blocks], axis=0).astype(BF16)
    zpad = jnp.zeros((pad, LANES), BF16)
    k_ext = jnp.concatenate([kc, zpad], axis=0)
    v_ones = jnp.concatenate([jnp.concatenate([vc, zpad], axis=0),
                              jnp.ones((2 * LANES, LANES), BF16)], axis=1)
    s = lax.dot_general(stacked, k_ext, NT, preferred_element_type=F32)
    lane = lax.broadcasted_iota(jnp.int32, (ATT_HEADS * tq, LANES), 1)
    s0 = s[:, :LANES]
    s1 = s[:, LANES:]
    if first_key_pos is not None:
        s0 = jnp.where(lane + first_key_pos >= 0, s0, -jnp.inf)
        s1 = jnp.where(lane + (first_key_pos + LANES) >= 0, s1, -jnp.inf)
    s1 = jnp.where(lane < nk - LANES, s1, sink_tile)
    m = jnp.max(jnp.maximum(s0, s1), axis=-1, keepdims=True)
    p = jnp.concatenate([jnp.exp(s0 - m), jnp.exp(s1 - m)], axis=1).astype(BF16)
    od = lax.dot_general(p, v_ones, NN, preferred_element_type=F32)
    o = od[:, :LANES] / od[:, LANES:]
    half = (ATT_HEADS // 2) * tq
    outs = [jnp.where(lo, o[tq * j:tq * (j + 1)], o[half + tq * j:half + tq * (j + 1)])
            for j in range(ATT_HEADS // 2)]
    return jnp.concatenate(outs, axis=1)


def _attn_prompt_kernel(q_ref, k_ref, v_ref, kh_ref, vh_ref, sink_ref, o_ref, *, tq):
    i = pl.program_id(1)
    kcat = jnp.concatenate([kh_ref[...], k_ref[...]], axis=0).astype(BF16)
    vcat = jnp.concatenate([vh_ref[...], v_ref[...]], axis=0).astype(BF16)
    sink_tile = sink_ref[...]
    span = WINDOW + CHUNK
    for c in range(tq // CHUNK):
        first_key_pos = i * tq + c * CHUNK - WINDOW if c < WINDOW // CHUNK else None
        o_ref[CHUNK * c:CHUNK * (c + 1), :] = _attend(
            q_ref[CHUNK * c:CHUNK * (c + 1), :], kcat[CHUNK * c:CHUNK * c + span],
            vcat[CHUNK * c:CHUNK * c + span], sink_tile, first_key_pos)


def _sink_tile(sinks, rows_per_head, nk):
    lane = jnp.arange(LANES)[None, :]
    col = jnp.repeat(sinks.astype(F32), rows_per_head)[:, None]
    return jnp.where(lane == nk - LANES, col, jnp.where(lane > nk - LANES, -jnp.inf, 0.0))


def _attn_prompt(q, k, v, sinks, batch, seq, tq):
    sink_tile = _sink_tile(sinks, CHUNK, WINDOW + CHUNK)
    nt = seq // tq
    row = lambda b, i: (b * nt + i, 0)
    halo = lambda b, i: (jnp.maximum((b * nt + i) * (tq // WINDOW) - 1, 0), 0)
    return pl.pallas_call(
        functools.partial(_attn_prompt_kernel, tq=tq),
        grid=(batch, nt),
        in_specs=[
            pl.BlockSpec((tq, ATT_Q), row),
            pl.BlockSpec((tq, ATT_KV), row),
            pl.BlockSpec((tq, ATT_KV), row),
            pl.BlockSpec((WINDOW, ATT_KV), halo),
            pl.BlockSpec((WINDOW, ATT_KV), halo),
            pl.BlockSpec((ATT_HEADS * CHUNK, LANES), lambda b, i: (0, 0)),
        ],
        out_specs=pl.BlockSpec((tq, ATT_Q), row),
        out_shape=jax.ShapeDtypeStruct((batch * seq, ATT_Q), F32),
        compiler_params=pltpu.CompilerParams(
            dimension_semantics=("parallel", "parallel"), vmem_limit_bytes=VMEM_LIMIT),
        name="attn_prompt",
    )(q, k, v, k, v, sink_tile)


def _attn_sample_kernel(q_ref, k_ref, v_ref, kc_ref, vc_ref, sink_ref, o_ref):
    kall = jnp.concatenate([kc_ref[...], k_ref[...]], axis=0).astype(BF16)
    vall = jnp.concatenate([vc_ref[...], v_ref[...]], axis=0).astype(BF16)
    o_ref[...] = _attend(q_ref[...], kall, vall, sink_ref[...], None)


def _attn_sample(q, k, v, k_cache, v_cache, sinks, batch, t):
    rows = k_cache.shape[0] // batch
    sink_tile = _sink_tile(sinks, t, rows + t)
    row = lambda b: (b, 0)
    return pl.pallas_call(
        _attn_sample_kernel,
        grid=(batch,),
        in_specs=[
            pl.BlockSpec((t, ATT_Q), row),
            pl.BlockSpec((t, ATT_KV), row),
            pl.BlockSpec((t, ATT_KV), row),
            pl.BlockSpec((rows, ATT_KV), row),
            pl.BlockSpec((rows, ATT_KV), row),
            pl.BlockSpec((ATT_HEADS * t, LANES), lambda b: (0, 0)),
        ],
        out_specs=pl.BlockSpec((t, ATT_Q), row),
        out_shape=jax.ShapeDtypeStruct((batch * t, ATT_Q), F32),
        compiler_params=pltpu.CompilerParams(dimension_semantics=("parallel",)),
        name="attn_sample",
    )(q, k, v, k_cache, v_cache, sink_tile)


def _systems(x, rows):
    nc = x.shape[0] // rows
    return jnp.stack([x[rows * c:rows * (c + 1), LANES * j:LANES * (j + 1)]
                      for c in range(nc) for j in range(RWKV_HEADS // 2)])


def _bdot(a, b, dims):
    return lax.dot_general(a.astype(BF16), b.astype(BF16), dims, preferred_element_type=F32)


def _wkv_tile(s_bd, r, k, v, lw, a, b, chain):
    c = CHUNK
    tt = r.shape[0]
    nc = tt // c
    np_ = RWKV_HEADS // 2
    ri = lax.broadcasted_iota(jnp.int32, (tt, tt), 0)
    ci = lax.broadcasted_iota(jnp.int32, (tt, tt), 1)
    tri = jnp.where((ri >= ci) & ((ri & -c) == (ci & -c)), 1.0, 0.0).astype(BF16)
    hi = lw.astype(BF16)
    rem = lw - hi.astype(F32)
    mid = rem.astype(BF16)
    low = (rem - mid.astype(F32)).astype(BF16)
    cum = (lax.dot_general(tri, hi, NN, preferred_element_type=F32)
           + lax.dot_general(tri, mid, NN, preferred_element_type=F32)
           + lax.dot_general(tri, low, NN, preferred_element_type=F32))
    e = jnp.exp(cum)
    e_inv = jnp.exp(-cum)
    e_x = jnp.exp(cum - lw)
    rt = r * e
    at = a * e_x
    kt = k * e_inv
    bt = b * e_inv

    lo = _lane_lo((1, 1, LANES))
    at_p, rt_p, kt_p, bt_p, v_p = (_systems(t, c) for t in (at, rt, kt, bt, v))
    ec_p = jnp.stack([e[c * i + c - 1:c * i + c, LANES * j:LANES * (j + 1)]
                      for i in range(nc) for j in range(np_)])
    kh_p = kt_p * ec_p
    bh_p = bt_p * ec_p
    at0 = jnp.where(lo, at_p, 0.0)
    at1 = jnp.where(lo, 0.0, at_p)
    rt0 = jnp.where(lo, rt_p, 0.0)
    rt1 = jnp.where(lo, 0.0, rt_p)
    lhs = jnp.concatenate([at0, at1, rt0, rt1], axis=1)
    rhs = jnp.concatenate([bt_p, kt_p], axis=1)
    g = _bdot(lhs, rhs, BNT)

    r128 = lax.broadcasted_iota(jnp.int32, (1, LANES, LANES), 1)
    l128 = lax.broadcasted_iota(jnp.int32, (1, LANES, LANES), 2)
    t_idx = r128 & (c - 1)
    s_idx = l128 & (c - 1)
    ga = jnp.where(s_idx < t_idx, g[:, :LANES], 0.0)
    gr = jnp.where(s_idx <= t_idx, g[:, LANES:], 0.0)
    same = (r128 < c) == (l128 < c)
    ga_sw = jnp.concatenate(
        [ga[:, :c], jnp.stack([pltpu.roll(ga[j, c:], c, 1) for j in range(ga.shape[0])])], axis=1)
    a_bd = jnp.where(same, ga_sw, 0.0)

    eye = jnp.where(r128 == l128, 1.0, 0.0)
    t_inv = eye + a_bd
    pw = a_bd
    for _ in range(5):
        pw = _bdot(pw, pw, BNN)
        t_inv = t_inv + _bdot(t_inv, pw, BNN)

    zeros = jnp.zeros_like(v_p)
    xak = _bdot(ga, jnp.concatenate([zeros, v_p], axis=1), BNN)
    xak = jnp.where(same, xak, 0.0)
    z = jnp.concatenate([jnp.concatenate([at0, at1], axis=1), xak], axis=2)
    tz = _bdot(t_inv, z, BNN)
    tzs = tz[:, :c] + tz[:, c:]
    q_mat = jnp.concatenate(
        [tzs, jnp.concatenate([zeros, v_p], axis=2)], axis=1).astype(BF16)
    gq = _bdot(gr, q_mat, BNN)
    r_eff = rt_p + jnp.where(lo, gq[:, :c, :LANES], gq[:, c:, :LANES])
    y0 = jnp.where(lo, gq[:, :c, LANES:], gq[:, c:, LANES:])
    bk = jnp.concatenate([bh_p, kh_p], axis=1)
    mn = _bdot(q_mat, bk, BTN)
    wb = jnp.where(same, mn[:, :LANES], 0.0)
    n_mat = jnp.where(same, mn[:, LANES:], 0.0)

    flat = lambda y_i: jnp.concatenate([y_i[j] for j in range(np_)], axis=1)
    if chain:
        ys = []
        for i in range(nc):
            sl = slice(np_ * i, np_ * (i + 1))
            s_bf = s_bd.astype(BF16)
            ys.append(flat(_bdot(r_eff[sl], s_bf, BNT) + y0[sl]))
            s_bd = s_bd * ec_p[sl] + _bdot(s_bf, wb[sl], BNN) + n_mat[sl]
    else:
        s_bf = s_bd.astype(BF16)
        y_all = _bdot(r_eff, s_bf, BNT) + y0
        ys = [flat(y_all[np_ * i:np_ * (i + 1)]) for i in range(nc)]
        s_bd = s_bd * ec_p + _bdot(s_bf, wb, BNN) + n_mat
    y = ys[0] if nc == 1 else jnp.concatenate(ys, axis=0)
    return y, s_bd


def _heads_to_bd(s8):
    z = jnp.zeros((HEAD_DIM, HEAD_DIM), F32)
    return jnp.stack([jnp.concatenate([jnp.concatenate([s8[2 * j], z], axis=1),
                                       jnp.concatenate([z, s8[2 * j + 1]], axis=1)], axis=0)
                      for j in range(RWKV_HEADS // 2)])


def _rwkv_kernel(*refs, chain, zero_state, t):
    tok_refs, refs = refs[:7], refs[7:]
    if not zero_state:
        s0_ref, refs = refs[0], refs[1:]
    rk_ref, lng_ref, lnb_ref, o_ref, sout_ref, s_scr = refs
    if chain:
        r, k, v, lw, a, b, gate = (ref[...] for ref in tok_refs)
        @pl.when(pl.program_id(1) == 0)
        def _():
            s_scr[...] = jnp.zeros_like(s_scr) if zero_state else _heads_to_bd(s0_ref[...])
        s_in = s_scr[...]
    else:
        ns = tok_refs[0].shape[0] // t
        zpad = jnp.zeros((CHUNK - t, RWKV_W), F32)
        r, k, v, lw, a, b, gate = (
            jnp.concatenate([piece for s in range(ns) for piece in (ref[t * s:t * (s + 1), :], zpad)], axis=0)
            for ref in tok_refs)
        s_in = jnp.concatenate([_heads_to_bd(s0_ref[s]) for s in range(ns)], axis=0)
    y, s_out = _wkv_tile(s_in, r, k, v, lw, a, b, chain)
    np_ = RWKV_HEADS // 2

    def store_heads(dst, s_bd):
        for j in range(np_):
            dst[2 * j] = s_bd[j, :HEAD_DIM, :HEAD_DIM]
            dst[2 * j + 1] = s_bd[j, HEAD_DIM:, HEAD_DIM:]

    if chain:
        s_scr[...] = s_out
    else:
        for s in range(ns):
            store_heads(sout_ref.at[s], s_out[np_ * s:np_ * (s + 1)])

    ones_bd = _ones_bd()
    inv_n = 1.0 / HEAD_DIM
    mean = _seg_sum(y, ones_bd) * inv_n
    d = y - mean
    var = _seg_sum(d * d, ones_bd) * inv_n
    yn = d * lax.rsqrt(var + GN_EPS) * lng_ref[...] + lnb_ref[...]
    bonus = _seg_sum(r * k * rk_ref[...], ones_bd) * v
    out = (yn + bonus) * gate
    if chain:
        o_ref[...] = out

        @pl.when(pl.program_id(1) == pl.num_programs(1) - 1)
        def _():
            store_heads(sout_ref, s_scr[...])
    else:
        for s in range(ns):
            o_ref[t * s:t * (s + 1), :] = out[CHUNK * s:CHUNK * s + t, :]


def _rwkv(tok, s0, post_w, n_seq, seq, tile_rows, chain):
    if chain:
        groups, nt, ns = n_seq, seq // tile_rows, None
    else:
        groups, nt, ns = n_seq * seq // tile_rows, 1, tile_rows // seq
    row = lambda g, i: (g * nt + i, 0)
    const = lambda g, i: (0, 0)
    state_spec = pl.BlockSpec((ns, RWKV_HEADS, HEAD_DIM, HEAD_DIM), lambda g, i: (g, 0, 0, 0))
    wide = pl.BlockSpec((tile_rows, RWKV_W), row)
    state_in = [] if s0 is None else [s0]
    return pl.pallas_call(
        functools.partial(_rwkv_kernel, chain=chain, zero_state=s0 is None, t=seq),
        grid=(groups, nt),
        in_specs=[wide] * 7 + [state_spec] * len(state_in) + [pl.BlockSpec(w.shape, const) for w in post_w],
        out_specs=[wide, state_spec],
        out_shape=[
            jax.ShapeDtypeStruct((n_seq * seq, RWKV_W), F32),
            jax.ShapeDtypeStruct((n_seq, RWKV_HEADS, HEAD_DIM, HEAD_DIM), F32),
        ],
        scratch_shapes=[pltpu.VMEM((RWKV_HEADS // 2, LANES, LANES), F32)],
        compiler_params=pltpu.CompilerParams(
            dimension_semantics=("parallel", "arbitrary"), vmem_limit_bytes=VMEM_LIMIT),
        name="rwkv",
    )(*tok, *state_in, *post_w)


def _ffn_kernel(x_ref, att_ref, rw_ref, woa_ref, wor_ref, g2_ref, wup_ref, wdn_ref, o_ref, *, tf):
    x2 = x_ref[...] + _dot(att_ref[...], woa_ref[...]) + _dot(rw_ref[...], wor_ref[...])
    ms = jnp.mean(x2 * x2, axis=-1, keepdims=True)
    h = (x2 * lax.rsqrt(ms + NORM_EPS) * g2_ref[...]).astype(BF16)
    us = []
    for j in range(wup_ref.shape[1] // tf):
        u = jnp.maximum(lax.dot_general(h, wup_ref[:, tf * j:tf * (j + 1)], NN,
                                        preferred_element_type=F32), 0.0)
        us.append((u * u).astype(BF16))
    u_all = jnp.concatenate(us, axis=1)
    o_ref[...] = x2 + lax.dot_general(u_all, wdn_ref[...], NN, preferred_element_type=F32)


def _out_ffn(x2d, att, rw, wo_att, wo_rw, g2, w_up, w_down, tm, tf):
    n, d = x2d.shape
    dff = w_up.shape[1]
    row = lambda i: (i, 0)
    const = lambda i: (0, 0)
    return pl.pallas_call(
        functools.partial(_ffn_kernel, tf=tf),
        grid=(n // tm,),
        in_specs=[
            pl.BlockSpec((tm, d), row),
            pl.BlockSpec((tm, ATT_Q), row),
            pl.BlockSpec((tm, RWKV_W), row),
            _resident((ATT_Q, d), const),
            _resident((RWKV_W, d), const),
            _resident((1, d), const),
            _resident((d, dff), const),
            _resident((dff, d), const),
        ],
        out_specs=pl.BlockSpec((tm, d), row),
        out_shape=jax.ShapeDtypeStruct((n, d), F32),
        compiler_params=pltpu.CompilerParams(
            dimension_semantics=("parallel",), vmem_limit_bytes=VMEM_LIMIT),
        name="out_ffn",
    )(x2d, att, rw, wo_att, wo_rw, g2, w_up, w_down)


def _rope_tables(pos):
    half = HEAD_DIM // 2
    inv = ROPE_THETA ** (-jnp.arange(half, dtype=F32) / half)
    ang = pos.astype(F32)[:, None] * inv[None, :]
    cos = jnp.cos(ang)
    sin = jnp.sin(ang)
    zero = jnp.zeros_like(sin)
    tile = lambda t: jnp.concatenate([t, t], axis=1)
    return (tile(jnp.concatenate([cos, cos], axis=1)),
            tile(jnp.concatenate([-sin, zero], axis=1)),
            tile(jnp.concatenate([zero, sin], axis=1)))


def _pair_perm():
    idx = []
    for j in range(ATT_HEADS // 2):
        idx += list(range(HEAD_DIM * j, HEAD_DIM * (j + 1)))
        idx += list(range(HEAD_DIM * (j + 4), HEAD_DIM * (j + 5)))
    return np.asarray(idx, dtype=np.int32)


def _pad_rw(t):
    return jnp.pad(t, [(0, 0)] * (t.ndim - 1) + [(0, RW_COLS - RW_REAL)])


def _layer(x, tabs, k_past, v_past, shift_prev, wkv0, lw, tm):
    (att_w, rw_pre_w, rw_post_w, sink8, wo_att, wo_rw, g2, w_up, w_down) = lw
    b, t, d = x.shape
    n = b * t
    x2d = x.reshape(n, d)
    q, k, v, *tok, tail = _inproj(x2d, _pad_rw(shift_prev)[:, None, :], att_w, rw_pre_w, tabs, tm, t)

    if k_past is None:
        att = _attn_prompt(q, k, v, sink8, b, t, min(t, 512))
        rows = min(WINDOW, t)
        new_k = k.reshape(b, t, ATT_KV_HEADS, HEAD_DIM)[:, t - rows:]
        new_v = v.reshape(b, t, ATT_KV_HEADS, HEAD_DIM)[:, t - rows:]
    else:
        att = _attn_sample(q, k, v, k_past.reshape(-1, ATT_KV), v_past.reshape(-1, ATT_KV),
                           sink8, b, t)
        new_k = k.reshape(b, t, ATT_KV_HEADS, HEAD_DIM)
        new_v = v.reshape(b, t, ATT_KV_HEADS, HEAD_DIM)

    if t >= tm:
        shift_out = tail[:, SUBLANES - 1, :RW_REAL]
        rw, wkv = _rwkv(tok, wkv0, rw_post_w, b, t, min(t, 256), True)
    else:
        shift_out = tail.reshape(b, t, RW_COLS)[:, t - 1, :RW_REAL]
        per_tile = max(m for m in (4, 2, 1) if b % m == 0)
        rw, wkv = _rwkv(tok, wkv0, rw_post_w, b, t, per_tile * t, False)
    y = _out_ffn(x2d, att, rw, wo_att, wo_rw, g2, w_up, w_down, tm, 1024)
    return (y.reshape(b, t, d), new_k, new_v, wkv, shift_out)


def kernel(x_prompt, x_sample, cache_attn_k, cache_attn_v, state_rwkv_wkv, state_rwkv_shift, ln1_g, w_in, q_norm_g, k_norm_g, attn_sinks, shift_mu, decay_w0, decay_w2, iclr_a0, iclr_a2, gate_g2, k_k, k_a, r_k, lnx_g, lnx_b, w_out, ln2_g, w_up, w_down):
    bp, tp, d = x_prompt.shape
    bs, ts, _ = x_sample.shape
    depth = w_in.shape[0]
    perm = _pair_perm()
    tabs_p = _rope_tables(jnp.arange(tp))
    tabs_s = tuple(jnp.tile(t, (bs, 1)) for t in _rope_tables(PAST_LEN + jnp.arange(ts)))
    tm_p = min(512, bp * tp)
    tm_s = min(512, bs * ts)

    hp, hs = x_prompt, x_sample
    outs_p, outs_s = [], []
    for l in range(depth):
        wl = w_in[l]
        att_cols = ATT_Q + 2 * ATT_KV
        w_in_p = jnp.concatenate(
            [wl[:, :ATT_Q][:, perm], wl[:, ATT_Q:att_cols], _pad_rw(wl[:, att_cols:])], axis=1).astype(BF16)
        row2 = lambda t: t.reshape(1, -1)
        zeros64 = jnp.zeros((64, RWKV_W), F32)
        att_w = (row2(ln1_g[l]), w_in_p, row2(jnp.tile(q_norm_g[l], ATT_HEADS)),
                 row2(jnp.tile(k_norm_g[l], ATT_KV_HEADS)))
        rw_pre_w = (
            row2(_pad_rw(shift_mu[l])),
            row2(decay_w0[l]),
            jnp.concatenate([decay_w2[l], zeros64], axis=0).astype(BF16),
            row2(iclr_a0[l]),
            jnp.concatenate([zeros64, iclr_a2[l]], axis=0).astype(BF16),
            jnp.pad(gate_g2[l], ((0, GATE_PAD - gate_g2.shape[1]), (0, 0))).astype(BF16),
            row2(k_k[l]), row2(k_a[l]),
        )
        rw_post_w = (row2(r_k[l]), row2(lnx_g[l]), row2(lnx_b[l]))
        lw = (att_w, rw_pre_w, rw_post_w, attn_sinks[l],
              w_out[l][:ATT_Q][perm].astype(BF16), w_out[l][ATT_Q:].astype(BF16),
              row2(ln2_g[l]), w_up[l].astype(BF16), w_down[l].astype(BF16))
        zero_shift = jnp.zeros((bp, RW_REAL), F32)
        hp, *op = _layer(hp, tabs_p, None, None, zero_shift, None, lw, tm_p)
        hs, *os_ = _layer(hs, tabs_s, cache_attn_k[l], cache_attn_v[l], state_rwkv_shift[l],
                          state_rwkv_wkv[l], lw, tm_s)
        outs_p.append(op)
        outs_s.append(os_)
    stack = lambda outs, i: jnp.stack([o[i] for o in outs])
    return (hp, hs,
            stack(outs_p, 0), stack(outs_p, 1), stack(outs_p, 2), stack(outs_p, 3),
            stack(outs_s, 0), stack(outs_s, 1), stack(outs_s, 2), stack(outs_s, 3))
```

```python
import functools

import jax
import jax.numpy as jnp
import numpy as np
from jax import lax
from jax.experimental import pallas as pl
from jax.experimental.pallas import tpu as pltpu

F32 = jnp.float32
BF16 = jnp.bfloat16

CHUNK = 64
WINDOW = 128
HEAD_DIM = 64
ATT_HEADS = 8
ATT_KV_HEADS = 2
RWKV_HEADS = 8
RWKV_W = 512
ATT_Q = 512
ATT_KV = 128
LORA_WA = 128
GATE_PAD = 256
RW_COLS = 3 * RWKV_W + LORA_WA + GATE_PAD
RW_REAL = 3 * RWKV_W + 64 + 64 + 160
IN_COLS_PAD = ATT_Q + 2 * ATT_KV + RW_COLS
PAST_LEN = 4096
ROPE_THETA = 10000.0
ATT_SCALE = HEAD_DIM ** -0.5
NORM_EPS = 1e-6
GN_EPS = 64e-5
DECAY_SCALE = float(np.exp(-0.5))
LANES = 128
SUBLANES = 8
WKV_TILE = 256
VMEM_LIMIT = 52 * 1024 * 1024

NN = (((1,), (0,)), ((), ()))
NT = (((1,), (1,)), ((), ()))
BNN = (((2,), (1,)), ((0,), (0,)))
BNT = (((2,), (2,)), ((0,), (0,)))
BTN = (((1,), (1,)), ((0,), (0,)))


def _dot(a, b, dims=NN):
    return lax.dot_general(a.astype(BF16), b.astype(BF16), dims, preferred_element_type=F32)


def _split2(x):
    hi = x.astype(BF16)
    lo = (x - hi.astype(F32)).astype(BF16)
    return hi, lo


def _seg_sum(x, ones_bd):
    outs = []
    for j in range(x.shape[1] // LANES):
        hi, lo = _split2(x[:, LANES * j:LANES * (j + 1)])
        outs.append(lax.dot_general(jnp.concatenate([hi, lo], axis=1), ones_bd, NN,
                                    preferred_element_type=F32))
    return outs[0] if len(outs) == 1 else jnp.concatenate(outs, axis=1)


def _ones_bd():
    r = lax.broadcasted_iota(jnp.int32, (2 * LANES, LANES), 0) & (LANES - 1)
    c = lax.broadcasted_iota(jnp.int32, (2 * LANES, LANES), 1)
    return jnp.where((r < HEAD_DIM) == (c < HEAD_DIM), 1.0, 0.0).astype(BF16)


def _lane_lo(shape):
    return lax.broadcasted_iota(jnp.int32, shape, len(shape) - 1) < HEAD_DIM


_resident = functools.partial(pl.BlockSpec, pipeline_mode=pl.Buffered(1))


def _inproj_kernel(x_ref, sprev_ref, g1_ref, w_ref, qg_ref, kg_ref, cos_ref, sa_ref, sb_ref,
                   mu_ref, w0_ref, w2_ref, a0_ref, a2_ref, g2_ref, kk_ref, ka_ref,
                   q_ref, k_ref, v_ref, r_ref, km_ref, vv_ref, lw_ref, a_ref, b_ref, gate_ref,
                   tail_ref, carry_scr, *, seq):
    tm = x_ref.shape[0]
    rows = lax.broadcasted_iota(jnp.int32, (tm, 1), 0)
    rw0 = ATT_Q + 2 * ATT_KV
    ones_bd = _ones_bd()
    x = x_ref[...]
    ms = jnp.mean(x * x, axis=-1, keepdims=True)
    h = (x * lax.rsqrt(ms + NORM_EPS) * g1_ref[...]).astype(BF16)

    def project(c0, c1):
        return lax.dot_general(h, w_ref[:, c0:c1], NN, preferred_element_type=F32)

    if seq >= tm:
        @pl.when(pl.program_id(0) % (seq // tm) == 0)
        def _():
            carry_scr[0:1, :] = sprev_ref[...]

    def shifted(c0, c1):
        p = project(rw0 + c0, rw0 + c1)
        prev = pltpu.roll(p, 1, 0)
        if seq >= tm:
            prev = jnp.where(rows == 0, carry_scr[0:1, c0:c1], prev)
            carry_scr[0:1, c0:c1] = p[tm - 1:tm, :]
            tail_ref[:, c0:c1] = p[tm - SUBLANES:tm, :]
        else:
            for s in range(tm // seq):
                prev = jnp.where(rows == s * seq, sprev_ref[s, :, c0:c1], prev)
            tail_ref[:, c0:c1] = p
        return p + (prev - p) * mu_ref[:, c0:c1]

    lora = shifted(3 * RWKV_W, RW_COLS)
    wa = lora[:, :LORA_WA]
    gd = lora[:, LORA_WA:]
    k = shifted(RWKV_W, 2 * RWKV_W)
    r_ref[...] = shifted(0, RWKV_W)
    vv_ref[...] = shifted(2 * RWKV_W, 3 * RWKV_W)

    cos = cos_ref[...]
    sin_a = sa_ref[...]
    sin_b = sb_ref[...]

    def norm_rope(p, g):
        ss = _seg_sum(p * p, ones_bd)
        y = p * lax.rsqrt(ss * (1.0 / HEAD_DIM) + NORM_EPS) * g
        outs = []
        for j in range(p.shape[1] // LANES):
            yb = y[:, LANES * j:LANES * (j + 1)]
            outs.append(yb * cos + pltpu.roll(yb, LANES - 32, 1) * sin_a + pltpu.roll(yb, 32, 1) * sin_b)
        return outs[0] if len(outs) == 1 else jnp.concatenate(outs, axis=1)

    q_ref[...] = norm_rope(project(0, ATT_Q), qg_ref[...])
    pkv = project(ATT_Q, ATT_Q + 2 * ATT_KV)
    k_ref[...] = norm_rope(pkv[:, :ATT_KV], kg_ref[...])
    v_ref[...] = pkv[:, ATT_KV:]

    sigmoid = lambda z: 0.5 + 0.5 * jnp.tanh(0.5 * z)
    zw = w0_ref[...] + _dot(jnp.tanh(wa), w2_ref[...])
    lw_ref[...] = -DECAY_SCALE * sigmoid(zw)
    a_ic = sigmoid(a0_ref[...] + _dot(wa, a2_ref[...]))
    gate_ref[...] = _dot(sigmoid(gd), g2_ref[...])
    kk = k * kk_ref[...]
    kk = kk * lax.rsqrt(jnp.maximum(_seg_sum(kk * kk, ones_bd), 1e-24))
    km_ref[...] = k * (1.0 + (a_ic - 1.0) * ka_ref[...])
    a_ref[...] = -kk
    b_ref[...] = kk * a_ic


def _inproj(x2d, sprev, att_w, rw_w, tabs, tm, seq):
    n, d = x2d.shape
    g1, w_in_p, qg, kg = att_w
    cos_t, sin_a, sin_b = tabs
    tab_blocks = cos_t.shape[0] // tm
    row = lambda i: (i, 0)
    const = lambda i: (0, 0)
    tab = lambda i: (i % tab_blocks, 0)
    if seq >= tm:
        tps = seq // tm
        sprev_spec = pl.BlockSpec((None, 1, RW_COLS), lambda i: (i // tps, 0, 0))
        tail_spec = pl.BlockSpec((None, SUBLANES, RW_COLS), lambda i: (i // tps, 0, 0))
        tail_shape = jax.ShapeDtypeStruct((n // seq, SUBLANES, RW_COLS), F32)
    else:
        sprev_spec = pl.BlockSpec((tm // seq, 1, RW_COLS), lambda i: (i, 0, 0))
        tail_spec = pl.BlockSpec((tm, RW_COLS), row)
        tail_shape = jax.ShapeDtypeStruct((n, RW_COLS), F32)
    wide = pl.BlockSpec((tm, RWKV_W), row)
    wide_shape = jax.ShapeDtypeStruct((n, RWKV_W), F32)
    return pl.pallas_call(
        functools.partial(_inproj_kernel, seq=seq),
        grid=(n // tm,),
        in_specs=[
            pl.BlockSpec((tm, d), row),
            sprev_spec,
            _resident((1, d), const),
            _resident((d, IN_COLS_PAD), const),
            _resident((1, ATT_Q), const),
            _resident((1, ATT_KV), const),
            pl.BlockSpec((tm, LANES), tab),
            pl.BlockSpec((tm, LANES), tab),
            pl.BlockSpec((tm, LANES), tab),
        ] + [_resident(w.shape, const) for w in rw_w],
        out_specs=[
            pl.BlockSpec((tm, ATT_Q), row),
            pl.BlockSpec((tm, ATT_KV), row),
            pl.BlockSpec((tm, ATT_KV), row),
        ] + [wide] * 7 + [tail_spec],
        out_shape=[
            jax.ShapeDtypeStruct((n, ATT_Q), F32),
            jax.ShapeDtypeStruct((n, ATT_KV), F32),
            jax.ShapeDtypeStruct((n, ATT_KV), F32),
        ] + [wide_shape] * 7 + [tail_shape],
        scratch_shapes=[pltpu.VMEM((SUBLANES, RW_COLS), F32)],
        compiler_params=pltpu.CompilerParams(
            dimension_semantics=("arbitrary",), vmem_limit_bytes=VMEM_LIMIT),
        name="inproj",
    )(x2d, sprev, g1, w_in_p, qg, kg, cos_t, sin_a, sin_b, *rw_w)


def _attend(q, kc, vc, sink_tile, first_key_pos):
    tq = q.shape[0]
    nk = kc.shape[0]
    pad = 2 * LANES - nk
    lo = _lane_lo((tq, LANES))
    blocks = [q[:, LANES * j:LANES * (j + 1)] * ATT_SCALE for j in range(ATT_HEADS // 2)]
    stacked = jnp.concatenate([jnp.where(lo, b, 0.0) for b in blocks]
                              + [jnp.where(lo, 0.0, b) for b in blocks], axis=0).astype(BF16)
    zpad = jnp.zeros((pad, LANES), BF16)
    k_ext = jnp.concatenate([kc, zpad], axis=0)
    v_ones = jnp.concatenate([jnp.concatenate([vc, zpad], axis=0),
                              jnp.ones((2 * LANES, LANES), BF16)], axis=1)
    s = lax.dot_general(stacked, k_ext, NT, preferred_element_type=F32)
    lane = lax.broadcasted_iota(jnp.int32, (ATT_HEADS * tq, LANES), 1)
    s0 = s[:, :LANES]
    s1 = s[:, LANES:]
    if first_key_pos is not None:
        s0 = jnp.where(lane + first_key_pos >= 0, s0, -jnp.inf)
        s1 = jnp.where(lane + (first_key_pos + LANES) >= 0, s1, -jnp.inf)
    s1 = jnp.where(lane < nk - LANES, s1, sink_tile)
    m = jnp.max(jnp.maximum(s0, s1), axis=-1, keepdims=True)
    p = jnp.concatenate([jnp.exp(s0 - m), jnp.exp(s1 - m)], axis=1).astype(BF16)
    od = lax.dot_general(p, v_ones, NN, preferred_element_type=F32)
    o = od[:, :LANES] / od[:, LANES:]
    half = (ATT_HEADS // 2) * tq
    outs = [jnp.where(lo, o[tq * j:tq * (j + 1)], o[half + tq * j:half + tq * (j + 1)])
            for j in range(ATT_HEADS // 2)]
    return jnp.concatenate(outs, axis=1)


def _attn_prompt_kernel(q_ref, k_ref, v_ref, kh_ref, vh_ref, sink_ref, o_ref, *, tq):
    i = pl.program_id(1)
    kcat = jnp.concatenate([kh_ref[...], k_ref[...]], axis=0).astype(BF16)
    vcat = jnp.concatenate([vh_ref[...], v_ref[...]], axis=0).astype(BF16)
    sink_tile = sink_ref[...]
    span = WINDOW + CHUNK
    for c in range(tq // CHUNK):
        first_key_pos = i * tq + c * CHUNK - WINDOW if c < WINDOW // CHUNK else None
        o_ref[CHUNK * c:CHUNK * (c + 1), :] = _attend(
            q_ref[CHUNK * c:CHUNK * (c + 1), :], kcat[CHUNK * c:CHUNK * c + span],
            vcat[CHUNK * c:CHUNK * c + span], sink_tile, first_key_pos)


def _sink_tile(sinks, rows_per_head, nk):
    lane = jnp.arange(LANES)[None, :]
    col = jnp.repeat(sinks.astype(F32), rows_per_head)[:, None]
    return jnp.where(lane == nk - LANES, col, jnp.where(lane > nk - LANES, -jnp.inf, 0.0))


def _attn_sample_kernel(q_ref, k_ref, v_ref, kc_ref, vc_ref, sink_ref, o_ref):
    kall = jnp.concatenate([kc_ref[...], k_ref[...]], axis=0).astype(BF16)
    vall = jnp.concatenate([vc_ref[...], v_ref[...]], axis=0).astype(BF16)
    o_ref[...] = _attend(q_ref[...], kall, vall, sink_ref[...], None)


def _attn_sample(q, k, v, k_cache, v_cache, sinks, batch, t):
    rows = k_cache.shape[0] // batch
    sink_tile = _sink_tile(sinks, t, rows + t)
    row = lambda b: (b, 0)
    return pl.pallas_call(
        _attn_sample_kernel,
        grid=(batch,),
        in_specs=[
            pl.BlockSpec((t, ATT_Q), row),
            pl.BlockSpec((t, ATT_KV), row),
            pl.BlockSpec((t, ATT_KV), row),
            pl.BlockSpec((rows, ATT_KV), row),
            pl.BlockSpec((rows, ATT_KV), row),
            pl.BlockSpec((ATT_HEADS * t, LANES), lambda b: (0, 0)),
        ],
        out_specs=pl.BlockSpec((t, ATT_Q), row),
        out_shape=jax.ShapeDtypeStruct((batch * t, ATT_Q), F32),
        compiler_params=pltpu.CompilerParams(dimension_semantics=("parallel",)),
        name="attn_sample",
    )(q, k, v, k_cache, v_cache, sink_tile)


def _systems(x, rows):
    nc = x.shape[0] // rows
    return jnp.stack([x[rows * c:rows * (c + 1), LANES * j:LANES * (j + 1)]
                      for c in range(nc) for j in range(RWKV_HEADS // 2)])


def _bdot(a, b, dims):
    return lax.dot_general(a.astype(BF16), b.astype(BF16), dims, preferred_element_type=F32)


def _wkv_prepare(r, k, v, lw, a, b):
    c = CHUNK
    tt = r.shape[0]
    nc = tt // c
    np_ = RWKV_HEADS // 2
    ri = lax.broadcasted_iota(jnp.int32, (tt, tt), 0)
    ci = lax.broadcasted_iota(jnp.int32, (tt, tt), 1)
    tri = jnp.where((ri >= ci) & ((ri & -c) == (ci & -c)), 1.0, 0.0).astype(BF16)
    hi = lw.astype(BF16)
    rem = lw - hi.astype(F32)
    mid = rem.astype(BF16)
    low = (rem - mid.astype(F32)).astype(BF16)
    cum = (lax.dot_general(tri, hi, NN, preferred_element_type=F32)
           + lax.dot_general(tri, mid, NN, preferred_element_type=F32)
           + lax.dot_general(tri, low, NN, preferred_element_type=F32))
    e = jnp.exp(cum)
    e_inv = jnp.exp(-cum)
    e_x = jnp.exp(cum - lw)
    rt = r * e
    at = a * e_x
    kt = k * e_inv
    bt = b * e_inv

    lo = _lane_lo((1, 1, LANES))
    at_p, rt_p, kt_p, bt_p, v_p = (_systems(t, c) for t in (at, rt, kt, bt, v))
    ec_p = jnp.stack([e[c * i + c - 1:c * i + c, LANES * j:LANES * (j + 1)]
                      for i in range(nc) for j in range(np_)])
    kh_p = kt_p * ec_p
    bh_p = bt_p * ec_p
    at0 = jnp.where(lo, at_p, 0.0)
    at1 = jnp.where(lo, 0.0, at_p)
    rt0 = jnp.where(lo, rt_p, 0.0)
    rt1 = jnp.where(lo, 0.0, rt_p)
    lhs = jnp.concatenate([at0, at1, rt0, rt1], axis=1)
    rhs = jnp.concatenate([bt_p, kt_p], axis=1)
    g = _bdot(lhs, rhs, BNT)

    r128 = lax.broadcasted_iota(jnp.int32, (1, LANES, LANES), 1)
    l128 = lax.broadcasted_iota(jnp.int32, (1, LANES, LANES), 2)
    t_idx = r128 & (c - 1)
    s_idx = l128 & (c - 1)
    ga = jnp.where(s_idx < t_idx, g[:, :LANES], 0.0)
    gr = jnp.where(s_idx <= t_idx, g[:, LANES:], 0.0)
    same = (r128 < c) == (l128 < c)
    ga_sw = jnp.concatenate(
        [ga[:, :c], jnp.stack([pltpu.roll(ga[j, c:], c, 1) for j in range(ga.shape[0])])], axis=1)
    a_bd = jnp.where(same, ga_sw, 0.0)

    eye = jnp.where(r128 == l128, 1.0, 0.0)
    t_inv = eye + a_bd
    pw = a_bd
    for _ in range(5):
        pw = _bdot(pw, pw, BNN)
        t_inv = t_inv + _bdot(t_inv, pw, BNN)

    zeros = jnp.zeros_like(v_p)
    xak = _bdot(ga, jnp.concatenate([zeros, v_p], axis=1), BNN)
    xak = jnp.where(same, xak, 0.0)
    z = jnp.concatenate([jnp.concatenate([at0, at1], axis=1), xak], axis=2)
    tz = _bdot(t_inv, z, BNN)
    tzs = tz[:, :c] + tz[:, c:]
    q_mat = jnp.concatenate(
        [tzs, jnp.concatenate([zeros, v_p], axis=2)], axis=1).astype(BF16)
    gq = _bdot(gr, q_mat, BNN)
    r_eff = rt_p + jnp.where(lo, gq[:, :c, :LANES], gq[:, c:, :LANES])
    y0 = jnp.where(lo, gq[:, :c, LANES:], gq[:, c:, LANES:])
    bk = jnp.concatenate([bh_p, kh_p], axis=1)
    mn = _bdot(q_mat, bk, BTN)
    wb = jnp.where(same, mn[:, :LANES], 0.0)
    n_mat = jnp.where(same, mn[:, LANES:], 0.0)

    return r_eff, y0, wb, n_mat, ec_p


def _wkv_apply(s_bd, prep, chain):
    r_eff, y0, wb, n_mat, ec_p = prep
    np_ = RWKV_HEADS // 2
    nc = r_eff.shape[0] // np_
    flat = lambda y_i: jnp.concatenate([y_i[j] for j in range(np_)], axis=1)
    if chain:
        ys = []
        for i in range(nc):
            sl = slice(np_ * i, np_ * (i + 1))
            s_bf = s_bd.astype(BF16)
            ys.append(flat(_bdot(r_eff[sl], s_bf, BNT) + y0[sl]))
            s_bd = s_bd * ec_p[sl] + _bdot(s_bf, wb[sl], BNN) + n_mat[sl]
    else:
        s_bf = s_bd.astype(BF16)
        y_all = _bdot(r_eff, s_bf, BNT) + y0
        ys = [flat(y_all[np_ * i:np_ * (i + 1)]) for i in range(nc)]
        s_bd = s_bd * ec_p + _bdot(s_bf, wb, BNN) + n_mat
    y = ys[0] if nc == 1 else jnp.concatenate(ys, axis=0)
    return y, s_bd


def _heads_to_bd(s8):
    z = jnp.zeros((HEAD_DIM, HEAD_DIM), F32)
    return jnp.stack([jnp.concatenate([jnp.concatenate([s8[2 * j], z], axis=1),
                                       jnp.concatenate([z, s8[2 * j + 1]], axis=1)], axis=0)
                      for j in range(RWKV_HEADS // 2)])


def _rwkv_kernel(*refs, chain, zero_state, t, companion=None):
    tok_refs, refs = refs[:7], refs[7:]
    if not zero_state:
        s0_ref, refs = refs[0], refs[1:]
    rk_ref, lng_ref, lnb_ref, o_ref, sout_ref, s_scr = refs
    if chain:
        r, k, v, lw, a, b, gate = (ref[...] for ref in tok_refs)
        @pl.when(pl.program_id(1) == 0)
        def _():
            s_scr[...] = jnp.zeros_like(s_scr) if zero_state else _heads_to_bd(s0_ref[...])
        s_in = s_scr[...]
    else:
        ns = tok_refs[0].shape[0] // t
        zpad = jnp.zeros((CHUNK - t, RWKV_W), F32)
        r, k, v, lw, a, b, gate = (
            jnp.concatenate([piece for s in range(ns) for piece in (ref[t * s:t * (s + 1), :], zpad)], axis=0)
            for ref in tok_refs)
        s_in = jnp.concatenate([_heads_to_bd(s0_ref[s]) for s in range(ns)], axis=0)
    np_ = RWKV_HEADS // 2
    ones_bd = _ones_bd()
    inv_n = 1.0 / HEAD_DIM

    def store_heads(dst, s_bd):
        for j in range(np_):
            dst[2 * j] = s_bd[j, :HEAD_DIM, :HEAD_DIM]
            dst[2 * j + 1] = s_bd[j, HEAD_DIM:, HEAD_DIM:]

    def finish(y, sl):
        mean = _seg_sum(y, ones_bd) * inv_n
        d = y - mean
        var = _seg_sum(d * d, ones_bd) * inv_n
        yn = d * lax.rsqrt(var + GN_EPS) * lng_ref[...] + lnb_ref[...]
        bonus = _seg_sum(r[sl] * k[sl] * rk_ref[...], ones_bd) * v[sl]
        return (yn + bonus) * gate[sl]

    rows = r.shape[0]
    sub = min(rows, WKV_TILE) if chain else rows
    slices = [slice(s0, s0 + sub) for s0 in range(0, rows, sub)]
    prepare = lambda sl: _wkv_prepare(r[sl], k[sl], v[sl], lw[sl], a[sl], b[sl])
    s_out = s_in
    outs = []
    prep = prepare(slices[0])
    for j, sl in enumerate(slices):
        nxt = prepare(slices[j + 1]) if j + 1 < len(slices) else None
        y, s_out = _wkv_apply(s_out, prep, chain)
        outs.append(finish(y, sl))
        prep = nxt
        if companion is not None and j == 0:
            companion()
    out = outs[0] if len(outs) == 1 else jnp.concatenate(outs, axis=0)

    if chain:
        s_scr[...] = s_out
        o_ref[...] = out

        @pl.when(pl.program_id(1) == pl.num_programs(1) - 1)
        def _():
            store_heads(sout_ref, s_scr[...])
    else:
        for s in range(ns):
            store_heads(sout_ref.at[s], s_out[np_ * s:np_ * (s + 1)])
            o_ref[t * s:t * (s + 1), :] = out[CHUNK * s:CHUNK * s + t, :]


def _rwkv(tok, s0, post_w, n_seq, seq, tile_rows, chain):
    if chain:
        groups, nt, ns = n_seq, seq // tile_rows, None
    else:
        groups, nt, ns = n_seq * seq // tile_rows, 1, tile_rows // seq
    row = lambda g, i: (g * nt + i, 0)
    const = lambda g, i: (0, 0)
    state_spec = pl.BlockSpec((ns, RWKV_HEADS, HEAD_DIM, HEAD_DIM), lambda g, i: (g, 0, 0, 0))
    wide = pl.BlockSpec((tile_rows, RWKV_W), row)
    state_in = [] if s0 is None else [s0]
    return pl.pallas_call(
        functools.partial(_rwkv_kernel, chain=chain, zero_state=s0 is None, t=seq),
        grid=(groups, nt),
        in_specs=[wide] * 7 + [state_spec] * len(state_in) + [pl.BlockSpec(w.shape, const) for w in post_w],
        out_specs=[wide, state_spec],
        out_shape=[
            jax.ShapeDtypeStruct((n_seq * seq, RWKV_W), F32),
            jax.ShapeDtypeStruct((n_seq, RWKV_HEADS, HEAD_DIM, HEAD_DIM), F32),
        ],
        scratch_shapes=[pltpu.VMEM((RWKV_HEADS // 2, LANES, LANES), F32)],
        compiler_params=pltpu.CompilerParams(
            dimension_semantics=("parallel", "arbitrary"), vmem_limit_bytes=VMEM_LIMIT),
        name="rwkv",
    )(*tok, *state_in, *post_w)


def _mixers_prompt_kernel(q_ref, k_ref, v_ref, kh_ref, vh_ref, sink_ref, *refs, tq):
    *rwkv_refs, att_ref, rw_ref, sout_ref, s_scr = refs
    attention = functools.partial(_attn_prompt_kernel, q_ref, k_ref, v_ref, kh_ref, vh_ref, sink_ref,
                                  att_ref, tq=tq)
    _rwkv_kernel(*rwkv_refs, rw_ref, sout_ref, s_scr, chain=True, zero_state=True, t=tq,
                 companion=attention)


def _mixers_prompt(q, k, v, sinks, tok, post_w, batch, seq, tq):
    sink_tile = _sink_tile(sinks, CHUNK, WINDOW + CHUNK)
    nt = seq // tq
    row = lambda b, i: (b * nt + i, 0)
    const = lambda b, i: (0, 0)
    halo = lambda b, i: (jnp.maximum((b * nt + i) * (tq // WINDOW) - 1, 0), 0)
    wide = pl.BlockSpec((tq, RWKV_W), row)
    state_spec = pl.BlockSpec((None, RWKV_HEADS, HEAD_DIM, HEAD_DIM), lambda b, i: (b, 0, 0, 0))
    return pl.pallas_call(
        functools.partial(_mixers_prompt_kernel, tq=tq),
        grid=(batch, nt),
        in_specs=[
            pl.BlockSpec((tq, ATT_Q), row),
            pl.BlockSpec((tq, ATT_KV), row),
            pl.BlockSpec((tq, ATT_KV), row),
            pl.BlockSpec((WINDOW, ATT_KV), halo),
            pl.BlockSpec((WINDOW, ATT_KV), halo),
            pl.BlockSpec((ATT_HEADS * CHUNK, LANES), const),
        ] + [wide] * 7 + [pl.BlockSpec(w.shape, const) for w in post_w],
        out_specs=[pl.BlockSpec((tq, ATT_Q), row), wide, state_spec],
        out_shape=[
            jax.ShapeDtypeStruct((batch * seq, ATT_Q), F32),
            jax.ShapeDtypeStruct((batch * seq, RWKV_W), F32),
            jax.ShapeDtypeStruct((batch, RWKV_HEADS, HEAD_DIM, HEAD_DIM), F32),
        ],
        scratch_shapes=[pltpu.VMEM((RWKV_HEADS // 2, LANES, LANES), F32)],
        compiler_params=pltpu.CompilerParams(
            dimension_semantics=("parallel", "arbitrary"), vmem_limit_bytes=VMEM_LIMIT),
        name="mixers",
    )(q, k, v, k, v, sink_tile, *tok, *post_w)


def _ffn_kernel(x_ref, att_ref, rw_ref, woa_ref, wor_ref, g2_ref, wup_ref, wdn_ref, o_ref, *, tf):
    x2 = x_ref[...] + _dot(att_ref[...], woa_ref[...]) + _dot(rw_ref[...], wor_ref[...])
    ms = jnp.mean(x2 * x2, axis=-1, keepdims=True)
    h = (x2 * lax.rsqrt(ms + NORM_EPS) * g2_ref[...]).astype(BF16)
    us = []
    for j in range(wup_ref.shape[1] // tf):
        u = jnp.maximum(lax.dot_general(h, wup_ref[:, tf * j:tf * (j + 1)], NN,
                                        preferred_element_type=F32), 0.0)
        us.append((u * u).astype(BF16))
    u_all = jnp.concatenate(us, axis=1)
    o_ref[...] = x2 + lax.dot_general(u_all, wdn_ref[...], NN, preferred_element_type=F32)


def _out_ffn(x2d, att, rw, wo_att, wo_rw, g2, w_up, w_down, tm, tf):
    n, d = x2d.shape
    dff = w_up.shape[1]
    row = lambda i: (i, 0)
    const = lambda i: (0, 0)
    return pl.pallas_call(
        functools.partial(_ffn_kernel, tf=tf),
        grid=(n // tm,),
        in_specs=[
            pl.BlockSpec((tm, d), row),
            pl.BlockSpec((tm, ATT_Q), row),
            pl.BlockSpec((tm, RWKV_W), row),
            _resident((ATT_Q, d), const),
            _resident((RWKV_W, d), const),
            _resident((1, d), const),
            _resident((d, dff), const),
            _resident((dff, d), const),
        ],
        out_specs=pl.BlockSpec((tm, d), row),
        out_shape=jax.ShapeDtypeStruct((n, d), F32),
        compiler_params=pltpu.CompilerParams(
            dimension_semantics=("parallel",), vmem_limit_bytes=VMEM_LIMIT),
        name="out_ffn",
    )(x2d, att, rw, wo_att, wo_rw, g2, w_up, w_down)


def _rope_tables(pos):
    half = HEAD_DIM // 2
    inv = ROPE_THETA ** (-jnp.arange(half, dtype=F32) / half)
    ang = pos.astype(F32)[:, None] * inv[None, :]
    cos = jnp.cos(ang)
    sin = jnp.sin(ang)
    zero = jnp.zeros_like(sin)
    tile = lambda t: jnp.concatenate([t, t], axis=1)
    return (tile(jnp.concatenate([cos, cos], axis=1)),
            tile(jnp.concatenate([-sin, zero], axis=1)),
            tile(jnp.concatenate([zero, sin], axis=1)))


def _pair_perm():
    idx = []
    for j in range(ATT_HEADS // 2):
        idx += list(range(HEAD_DIM * j, HEAD_DIM * (j + 1)))
        idx += list(range(HEAD_DIM * (j + 4), HEAD_DIM * (j + 5)))
    return np.asarray(idx, dtype=np.int32)


def _pad_rw(t):
    return jnp.pad(t, [(0, 0)] * (t.ndim - 1) + [(0, RW_COLS - RW_REAL)])


def _layer(x, tabs, k_past, v_past, shift_prev, wkv0, lw, tm):
    (att_w, rw_pre_w, rw_post_w, sink8, wo_att, wo_rw, g2, w_up, w_down) = lw
    b, t, d = x.shape
    n = b * t
    x2d = x.reshape(n, d)
    q, k, v, *tok, tail = _inproj(x2d, _pad_rw(shift_prev)[:, None, :], att_w, rw_pre_w, tabs, tm, t)

    if k_past is None:
        att, rw, wkv = _mixers_prompt(q, k, v, sink8, tok, rw_post_w, b, t, min(t, 2 * WKV_TILE))
        rows = min(WINDOW, t)
        new_k = k.reshape(b, t, ATT_KV_HEADS, HEAD_DIM)[:, t - rows:]
        new_v = v.reshape(b, t, ATT_KV_HEADS, HEAD_DIM)[:, t - rows:]
        shift_out = tail[:, SUBLANES - 1, :RW_REAL]
    else:
        att = _attn_sample(q, k, v, k_past.reshape(-1, ATT_KV), v_past.reshape(-1, ATT_KV),
                           sink8, b, t)
        new_k = k.reshape(b, t, ATT_KV_HEADS, HEAD_DIM)
        new_v = v.reshape(b, t, ATT_KV_HEADS, HEAD_DIM)
        shift_out = tail.reshape(b, t, RW_COLS)[:, t - 1, :RW_REAL]
        per_tile = max(m for m in (4, 2, 1) if b % m == 0)
        rw, wkv = _rwkv(tok, wkv0, rw_post_w, b, t, per_tile * t, False)
    y = _out_ffn(x2d, att, rw, wo_att, wo_rw, g2, w_up, w_down, tm, 1024)
    return (y.reshape(b, t, d), new_k, new_v, wkv, shift_out)


def kernel(x_prompt, x_sample, cache_attn_k, cache_attn_v, state_rwkv_wkv, state_rwkv_shift, ln1_g, w_in, q_norm_g, k_norm_g, attn_sinks, shift_mu, decay_w0, decay_w2, iclr_a0, iclr_a2, gate_g2, k_k, k_a, r_k, lnx_g, lnx_b, w_out, ln2_g, w_up, w_down):
    bp, tp, d = x_prompt.shape
    bs, ts, _ = x_sample.shape
    depth = w_in.shape[0]
    perm = _pair_perm()
    tabs_p = _rope_tables(jnp.arange(tp))
    tabs_s = tuple(jnp.tile(t, (bs, 1)) for t in _rope_tables(PAST_LEN + jnp.arange(ts)))
    tm_p = min(512, bp * tp)
    tm_s = min(512, bs * ts)

    hp, hs = x_prompt, x_sample
    outs_p, outs_s = [], []
    for l in range(depth):
        wl = w_in[l]
        att_cols = ATT_Q + 2 * ATT_KV
        w_in_p = jnp.concatenate(
            [wl[:, :ATT_Q][:, perm], wl[:, ATT_Q:att_cols], _pad_rw(wl[:, att_cols:])], axis=1).astype(BF16)
        row2 = lambda t: t.reshape(1, -1)
        zeros64 = jnp.zeros((64, RWKV_W), F32)
        att_w = (row2(ln1_g[l]), w_in_p, row2(jnp.tile(q_norm_g[l], ATT_HEADS)),
                 row2(jnp.tile(k_norm_g[l], ATT_KV_HEADS)))
        rw_pre_w = (
            row2(_pad_rw(shift_mu[l])),
            row2(decay_w0[l]),
            jnp.concatenate([decay_w2[l], zeros64], axis=0).astype(BF16),
            row2(iclr_a0[l]),
            jnp.concatenate([zeros64, iclr_a2[l]], axis=0).astype(BF16),
            jnp.pad(gate_g2[l], ((0, GATE_PAD - gate_g2.shape[1]), (0, 0))).astype(BF16),
            row2(k_k[l]), row2(k_a[l]),
        )
        rw_post_w = (row2(r_k[l]), row2(lnx_g[l]), row2(lnx_b[l]))
        lw = (att_w, rw_pre_w, rw_post_w, attn_sinks[l],
              w_out[l][:ATT_Q][perm].astype(BF16), w_out[l][ATT_Q:].astype(BF16),
              row2(ln2_g[l]), w_up[l].astype(BF16), w_down[l].astype(BF16))
        zero_shift = jnp.zeros((bp, RW_REAL), F32)
        hp, *op = _layer(hp, tabs_p, None, None, zero_shift, None, lw, tm_p)
        hs, *os_ = _layer(hs, tabs_s, cache_attn_k[l], cache_attn_v[l], state_rwkv_shift[l],
                          state_rwkv_wkv[l], lw, tm_s)
        outs_p.append(op)
        outs_s.append(os_)
    stack = lambda outs, i: jnp.stack([o[i] for o in outs])
    return (hp, hs,
            stack(outs_p, 0), stack(outs_p, 1), stack(outs_p, 2), stack(outs_p, 3),
            stack(outs_s, 0), stack(outs_s, 1), stack(outs_s, 2), stack(outs_s, 3))
```

```python
import functools

import jax
import jax.numpy as jnp
import numpy as np
from jax import lax
from jax.experimental import pallas as pl
from jax.experimental.pallas import tpu as pltpu

F32 = jnp.float32
BF16 = jnp.bfloat16

CHUNK = 64
WINDOW = 128
HEAD_DIM = 64
ATT_HEADS = 8
ATT_KV_HEADS = 2
RWKV_HEADS = 8
RWKV_W = 512
ATT_Q = 512
ATT_KV = 128
LORA_WA = 128
GATE_PAD = 256
RW_COLS = 3 * RWKV_W + LORA_WA + GATE_PAD
RW_REAL = 3 * RWKV_W + 64 + 64 + 160
IN_COLS_PAD = ATT_Q + 2 * ATT_KV + RW_COLS
PAST_LEN = 4096
ROPE_THETA = 10000.0
ATT_SCALE = HEAD_DIM ** -0.5
NORM_EPS = 1e-6
GN_EPS = 64e-5
DECAY_SCALE = float(np.exp(-0.5))
LANES = 128
SUBLANES = 8
WKV_TILE = 256
VMEM_LIMIT = 52 * 1024 * 1024

NN = (((1,), (0,)), ((), ()))
NT = (((1,), (1,)), ((), ()))
BNN = (((2,), (1,)), ((0,), (0,)))
BNT = (((2,), (2,)), ((0,), (0,)))
BTN = (((1,), (1,)), ((0,), (0,)))


def _dot(a, b, dims=NN):
    return lax.dot_general(a.astype(BF16), b.astype(BF16), dims, preferred_element_type=F32)


def _split2(x):
    hi = x.astype(BF16)
    lo = (x - hi.astype(F32)).astype(BF16)
    return hi, lo


def _seg_sum(x, ones_bd):
    outs = []
    for j in range(x.shape[1] // LANES):
        hi, lo = _split2(x[:, LANES * j:LANES * (j + 1)])
        outs.append(lax.dot_general(jnp.concatenate([hi, lo], axis=1), ones_bd, NN,
                                    preferred_element_type=F32))
    return outs[0] if len(outs) == 1 else jnp.concatenate(outs, axis=1)


def _ones_bd():
    r = lax.broadcasted_iota(jnp.int32, (2 * LANES, LANES), 0) & (LANES - 1)
    c = lax.broadcasted_iota(jnp.int32, (2 * LANES, LANES), 1)
    return jnp.where((r < HEAD_DIM) == (c < HEAD_DIM), 1.0, 0.0).astype(BF16)


def _lane_lo(shape):
    return lax.broadcasted_iota(jnp.int32, shape, len(shape) - 1) < HEAD_DIM


_resident = functools.partial(pl.BlockSpec, pipeline_mode=pl.Buffered(1))


def _inproj_kernel(x_ref, sprev_ref, g1_ref, w_ref, qg_ref, kg_ref, cos_ref, sa_ref, sb_ref,
                   mu_ref, w0_ref, w2_ref, a0_ref, a2_ref, g2_ref, kk_ref, ka_ref,
                   q_ref, k_ref, v_ref, r_ref, km_ref, vv_ref, lw_ref, a_ref, b_ref, gate_ref,
                   tail_ref, carry_scr, *, seq):
    tm = x_ref.shape[0]
    rows = lax.broadcasted_iota(jnp.int32, (tm, 1), 0)
    rw0 = ATT_Q + 2 * ATT_KV
    ones_bd = _ones_bd()
    x = x_ref[...]
    ms = jnp.mean(x * x, axis=-1, keepdims=True)
    h = (x * lax.rsqrt(ms + NORM_EPS) * g1_ref[...]).astype(BF16)

    def project(c0, c1):
        return lax.dot_general(h, w_ref[:, c0:c1], NN, preferred_element_type=F32)

    if seq >= tm:
        @pl.when(pl.program_id(0) % (seq // tm) == 0)
        def _():
            carry_scr[0:1, :] = sprev_ref[...]

    def shifted(c0, c1):
        p = project(rw0 + c0, rw0 + c1)
        prev = pltpu.roll(p, 1, 0)
        if seq >= tm:
            prev = jnp.where(rows == 0, carry_scr[0:1, c0:c1], prev)
            carry_scr[0:1, c0:c1] = p[tm - 1:tm, :]
            tail_ref[:, c0:c1] = p[tm - SUBLANES:tm, :]
        else:
            for s in range(tm // seq):
                prev = jnp.where(rows == s * seq, sprev_ref[s, :, c0:c1], prev)
            tail_ref[:, c0:c1] = p
        return p + (prev - p) * mu_ref[:, c0:c1]

    cos = cos_ref[...]
    sin_a = sa_ref[...]
    sin_b = sb_ref[...]

    def norm_rope(p, g):
        ss = _seg_sum(p * p, ones_bd)
        y = p * lax.rsqrt(ss * (1.0 / HEAD_DIM) + NORM_EPS) * g
        outs = []
        for j in range(p.shape[1] // LANES):
            yb = y[:, LANES * j:LANES * (j + 1)]
            outs.append(yb * cos + pltpu.roll(yb, LANES - 32, 1) * sin_a + pltpu.roll(yb, 32, 1) * sin_b)
        return outs[0] if len(outs) == 1 else jnp.concatenate(outs, axis=1)

    lora = shifted(3 * RWKV_W, RW_COLS)
    k = shifted(RWKV_W, 2 * RWKV_W)
    sigmoid = lambda z: 0.5 + 0.5 * jnp.tanh(0.5 * z)

    def token_maps(rs):
        wa = lora[rs, :LORA_WA]
        gd = lora[rs, LORA_WA:]
        kr = k[rs]
        zw = w0_ref[...] + _dot(jnp.tanh(wa), w2_ref[...])
        lw_ref[rs, :] = -DECAY_SCALE * sigmoid(zw)
        a_ic = sigmoid(a0_ref[...] + _dot(wa, a2_ref[...]))
        gate_ref[rs, :] = _dot(sigmoid(gd), g2_ref[...])
        kk = kr * kk_ref[...]
        kk = kk * lax.rsqrt(jnp.maximum(_seg_sum(kk * kk, ones_bd), 1e-24))
        km_ref[rs, :] = kr * (1.0 + (a_ic - 1.0) * ka_ref[...])
        a_ref[rs, :] = -kk
        b_ref[rs, :] = kk * a_ic

    r_ref[...] = shifted(0, RWKV_W)
    vv_ref[...] = shifted(2 * RWKV_W, 3 * RWKV_W)
    q_ref[...] = norm_rope(project(0, ATT_Q), qg_ref[...])
    pkv = project(ATT_Q, ATT_Q + 2 * ATT_KV)
    k_ref[...] = norm_rope(pkv[:, :ATT_KV], kg_ref[...])
    v_ref[...] = pkv[:, ATT_KV:]
    token_maps(slice(0, tm))


def _inproj(x2d, sprev, att_w, rw_w, tabs, tm, seq):
    n, d = x2d.shape
    g1, w_in_p, qg, kg = att_w
    cos_t, sin_a, sin_b = tabs
    tab_blocks = cos_t.shape[0] // tm
    row = lambda i: (i, 0)
    const = lambda i: (0, 0)
    tab = lambda i: (i % tab_blocks, 0)
    if seq >= tm:
        tps = seq // tm
        sprev_spec = pl.BlockSpec((None, 1, RW_COLS), lambda i: (i // tps, 0, 0))
        tail_spec = pl.BlockSpec((None, SUBLANES, RW_COLS), lambda i: (i // tps, 0, 0))
        tail_shape = jax.ShapeDtypeStruct((n // seq, SUBLANES, RW_COLS), F32)
    else:
        sprev_spec = pl.BlockSpec((tm // seq, 1, RW_COLS), lambda i: (i, 0, 0))
        tail_spec = pl.BlockSpec((tm, RW_COLS), row)
        tail_shape = jax.ShapeDtypeStruct((n, RW_COLS), F32)
    wide = pl.BlockSpec((tm, RWKV_W), row)
    wide_shape = jax.ShapeDtypeStruct((n, RWKV_W), F32)
    return pl.pallas_call(
        functools.partial(_inproj_kernel, seq=seq),
        grid=(n // tm,),
        in_specs=[
            pl.BlockSpec((tm, d), row),
            sprev_spec,
            _resident((1, d), const),
            _resident((d, IN_COLS_PAD), const),
            _resident((1, ATT_Q), const),
            _resident((1, ATT_KV), const),
            pl.BlockSpec((tm, LANES), tab),
            pl.BlockSpec((tm, LANES), tab),
            pl.BlockSpec((tm, LANES), tab),
        ] + [_resident(w.shape, const) for w in rw_w],
        out_specs=[
            pl.BlockSpec((tm, ATT_Q), row),
            pl.BlockSpec((tm, ATT_KV), row),
            pl.BlockSpec((tm, ATT_KV), row),
        ] + [wide] * 7 + [tail_spec],
        out_shape=[
            jax.ShapeDtypeStruct((n, ATT_Q), F32),
            jax.ShapeDtypeStruct((n, ATT_KV), F32),
            jax.ShapeDtypeStruct((n, ATT_KV), F32),
        ] + [wide_shape] * 7 + [tail_shape],
        scratch_shapes=[pltpu.VMEM((SUBLANES, RW_COLS), F32)],
        compiler_params=pltpu.CompilerParams(
            dimension_semantics=("arbitrary",), vmem_limit_bytes=VMEM_LIMIT),
        name="inproj",
    )(x2d, sprev, g1, w_in_p, qg, kg, cos_t, sin_a, sin_b, *rw_w)


def _attend(q, kc, vc, sink_tile, first_key_pos):
    tq = q.shape[0]
    nk = kc.shape[0]
    pad = 2 * LANES - nk
    lo = _lane_lo((tq, LANES))
    blocks = [q[:, LANES * j:LANES * (j + 1)] * ATT_SCALE for j in range(ATT_HEADS // 2)]
    stacked = jnp.concatenate([jnp.where(lo, b, 0.0) for b in blocks]
                              + [jnp.where(lo, 0.0, b) for b in blocks], axis=0).astype(BF16)
    zpad = jnp.zeros((pad, LANES), BF16)
    k_ext = jnp.concatenate([kc, zpad], axis=0)
    v_ones = jnp.concatenate([jnp.concatenate([vc, zpad], axis=0),
                              jnp.ones((2 * LANES, LANES), BF16)], axis=1)
    s = lax.dot_general(stacked, k_ext, NT, preferred_element_type=F32)
    lane = lax.broadcasted_iota(jnp.int32, (ATT_HEADS * tq, LANES), 1)
    s0 = s[:, :LANES]
    s1 = s[:, LANES:]
    if first_key_pos is not None:
        s0 = jnp.where(lane + first_key_pos >= 0, s0, -jnp.inf)
        s1 = jnp.where(lane + (first_key_pos + LANES) >= 0, s1, -jnp.inf)
    s1 = jnp.where(lane < nk - LANES, s1, sink_tile)
    m = jnp.max(jnp.maximum(s0, s1), axis=-1, keepdims=True)
    p = jnp.concatenate([jnp.exp(s0 - m), jnp.exp(s1 - m)], axis=1).astype(BF16)
    od = lax.dot_general(p, v_ones, NN, preferred_element_type=F32)
    o = od[:, :LANES] / od[:, LANES:]
    half = (ATT_HEADS // 2) * tq
    outs = [jnp.where(lo, o[tq * j:tq * (j + 1)], o[half + tq * j:half + tq * (j + 1)])
            for j in range(ATT_HEADS // 2)]
    return jnp.concatenate(outs, axis=1)


def _attn_prompt_kernel(q_ref, k_ref, v_ref, kh_ref, vh_ref, sink_ref, o_ref, *, tq):
    i = pl.program_id(1)
    kcat = jnp.concatenate([kh_ref[...], k_ref[...]], axis=0).astype(BF16)
    vcat = jnp.concatenate([vh_ref[...], v_ref[...]], axis=0).astype(BF16)
    sink_tile = sink_ref[...]
    span = WINDOW + CHUNK
    for c in range(tq // CHUNK):
        first_key_pos = i * tq + c * CHUNK - WINDOW if c < WINDOW // CHUNK else None
        o_ref[CHUNK * c:CHUNK * (c + 1), :] = _attend(
            q_ref[CHUNK * c:CHUNK * (c + 1), :], kcat[CHUNK * c:CHUNK * c + span],
            vcat[CHUNK * c:CHUNK * c + span], sink_tile, first_key_pos)


def _sink_tile(sinks, rows_per_head, nk):
    lane = jnp.arange(LANES)[None, :]
    col = jnp.repeat(sinks.astype(F32), rows_per_head)[:, None]
    return jnp.where(lane == nk - LANES, col, jnp.where(lane > nk - LANES, -jnp.inf, 0.0))


def _attn_sample_kernel(q_ref, k_ref, v_ref, kc_ref, vc_ref, sink_ref, o_ref, *, t):
    ns = q_ref.shape[0] // t
    rows = kc_ref.shape[0] // ns
    for s in range(ns):
        new = slice(t * s, t * (s + 1))
        old = slice(rows * s, rows * (s + 1))
        kall = jnp.concatenate([kc_ref[old, :], k_ref[new, :]], axis=0).astype(BF16)
        vall = jnp.concatenate([vc_ref[old, :], v_ref[new, :]], axis=0).astype(BF16)
        o_ref[new, :] = _attend(q_ref[new, :], kall, vall, sink_ref[...], None)


def _attn_sample(q, k, v, k_cache, v_cache, sinks, batch, t, per_step):
    rows = k_cache.shape[0] // batch
    sink_tile = _sink_tile(sinks, t, rows + t)
    row = lambda b: (b, 0)
    return pl.pallas_call(
        functools.partial(_attn_sample_kernel, t=t),
        grid=(batch // per_step,),
        in_specs=[
            pl.BlockSpec((per_step * t, ATT_Q), row),
            pl.BlockSpec((per_step * t, ATT_KV), row),
            pl.BlockSpec((per_step * t, ATT_KV), row),
            pl.BlockSpec((per_step * rows, ATT_KV), row),
            pl.BlockSpec((per_step * rows, ATT_KV), row),
            pl.BlockSpec((ATT_HEADS * t, LANES), lambda b: (0, 0)),
        ],
        out_specs=pl.BlockSpec((per_step * t, ATT_Q), row),
        out_shape=jax.ShapeDtypeStruct((batch * t, ATT_Q), F32),
        compiler_params=pltpu.CompilerParams(dimension_semantics=("parallel",)),
        name="attn_sample",
    )(q, k, v, k_cache, v_cache, sink_tile)


def _systems(x, rows):
    nc = x.shape[0] // rows
    return jnp.stack([x[rows * c:rows * (c + 1), LANES * j:LANES * (j + 1)]
                      for c in range(nc) for j in range(RWKV_HEADS // 2)])


def _bdot(a, b, dims):
    return lax.dot_general(a.astype(BF16), b.astype(BF16), dims, preferred_element_type=F32)


def _wkv_prepare(r, k, v, lw, a, b):
    c = CHUNK
    tt = r.shape[0]
    nc = tt // c
    np_ = RWKV_HEADS // 2
    ri = lax.broadcasted_iota(jnp.int32, (tt, tt), 0)
    ci = lax.broadcasted_iota(jnp.int32, (tt, tt), 1)
    tri = jnp.where((ri >= ci) & ((ri & -c) == (ci & -c)), 1.0, 0.0).astype(BF16)
    hi = lw.astype(BF16)
    rem = lw - hi.astype(F32)
    mid = rem.astype(BF16)
    low = (rem - mid.astype(F32)).astype(BF16)
    cum = (lax.dot_general(tri, hi, NN, preferred_element_type=F32)
           + lax.dot_general(tri, mid, NN, preferred_element_type=F32)
           + lax.dot_general(tri, low, NN, preferred_element_type=F32))
    e = jnp.exp(cum)
    e_inv = jnp.exp(-cum)
    e_x = jnp.exp(cum - lw)
    rt = r * e
    at = a * e_x
    kt = k * e_inv
    bt = b * e_inv

    lo = _lane_lo((1, 1, LANES))
    at_p, rt_p, kt_p, bt_p, v_p = (_systems(t, c) for t in (at, rt, kt, bt, v))
    ec_p = jnp.stack([e[c * i + c - 1:c * i + c, LANES * j:LANES * (j + 1)]
                      for i in range(nc) for j in range(np_)])
    kh_p = kt_p * ec_p
    bh_p = bt_p * ec_p
    at0 = jnp.where(lo, at_p, 0.0)
    at1 = jnp.where(lo, 0.0, at_p)
    rt0 = jnp.where(lo, rt_p, 0.0)
    rt1 = jnp.where(lo, 0.0, rt_p)
    lhs = jnp.concatenate([at0, at1, rt0, rt1], axis=1)
    rhs = jnp.concatenate([bt_p, kt_p], axis=1)
    g = _bdot(lhs, rhs, BNT)

    r128 = lax.broadcasted_iota(jnp.int32, (1, LANES, LANES), 1)
    l128 = lax.broadcasted_iota(jnp.int32, (1, LANES, LANES), 2)
    t_idx = r128 & (c - 1)
    s_idx = l128 & (c - 1)
    ga = jnp.where(s_idx < t_idx, g[:, :LANES], 0.0)
    gr = jnp.where(s_idx <= t_idx, g[:, LANES:], 0.0)
    same = (r128 < c) == (l128 < c)
    ga_sw = jnp.concatenate(
        [ga[:, :c], jnp.stack([pltpu.roll(ga[j, c:], c, 1) for j in range(ga.shape[0])])], axis=1)
    a_bd = jnp.where(same, ga_sw, 0.0)

    eye = jnp.where(r128 == l128, 1.0, 0.0)
    t_inv = eye + a_bd
    pw = a_bd
    for _ in range(5):
        pw = _bdot(pw, pw, BNN)
        t_inv = t_inv + _bdot(t_inv, pw, BNN)

    zeros = jnp.zeros_like(v_p)
    xak = _bdot(ga, jnp.concatenate([zeros, v_p], axis=1), BNN)
    xak = jnp.where(same, xak, 0.0)
    z = jnp.concatenate([jnp.concatenate([at0, at1], axis=1), xak], axis=2)
    tz = _bdot(t_inv, z, BNN)
    tzs = tz[:, :c] + tz[:, c:]
    q_mat = jnp.concatenate(
        [tzs, jnp.concatenate([zeros, v_p], axis=2)], axis=1).astype(BF16)
    gq = _bdot(gr, q_mat, BNN)
    r_eff = rt_p + jnp.where(lo, gq[:, :c, :LANES], gq[:, c:, :LANES])
    y0 = jnp.where(lo, gq[:, :c, LANES:], gq[:, c:, LANES:])
    bk = jnp.concatenate([bh_p, kh_p], axis=1)
    mn = _bdot(q_mat, bk, BTN)
    wb = jnp.where(same, mn[:, :LANES], 0.0)
    n_mat = jnp.where(same, mn[:, LANES:], 0.0)

    return r_eff, y0, wb, n_mat, ec_p


def _wkv_apply(s_bd, prep, chain):
    r_eff, y0, wb, n_mat, ec_p = prep
    np_ = RWKV_HEADS // 2
    nc = r_eff.shape[0] // np_
    flat = lambda y_i: jnp.concatenate([y_i[j] for j in range(np_)], axis=1)
    if chain:
        ys = []
        for i in range(nc):
            sl = slice(np_ * i, np_ * (i + 1))
            s_bf = s_bd.astype(BF16)
            ys.append(flat(_bdot(r_eff[sl], s_bf, BNT) + y0[sl]))
            s_bd = s_bd * ec_p[sl] + _bdot(s_bf, wb[sl], BNN) + n_mat[sl]
    else:
        s_bf = s_bd.astype(BF16)
        y_all = _bdot(r_eff, s_bf, BNT) + y0
        ys = [flat(y_all[np_ * i:np_ * (i + 1)]) for i in range(nc)]
        s_bd = s_bd * ec_p + _bdot(s_bf, wb, BNN) + n_mat
    y = ys[0] if nc == 1 else jnp.concatenate(ys, axis=0)
    return y, s_bd


def _heads_to_bd(s8):
    z = jnp.zeros((HEAD_DIM, HEAD_DIM), F32)
    return jnp.stack([jnp.concatenate([jnp.concatenate([s8[2 * j], z], axis=1),
                                       jnp.concatenate([z, s8[2 * j + 1]], axis=1)], axis=0)
                      for j in range(RWKV_HEADS // 2)])


def _rwkv_kernel(*refs, chain, zero_state, t, companion=None):
    tok_refs, refs = refs[:7], refs[7:]
    if not zero_state:
        s0_ref, refs = refs[0], refs[1:]
    rk_ref, lng_ref, lnb_ref, o_ref, sout_ref, s_scr = refs
    if chain:
        r, k, v, lw, a, b, gate = (ref[...] for ref in tok_refs)
        @pl.when(pl.program_id(1) == 0)
        def _():
            s_scr[...] = jnp.zeros_like(s_scr) if zero_state else _heads_to_bd(s0_ref[...])
        s_in = s_scr[...]
    else:
        ns = tok_refs[0].shape[0] // t
        zpad = jnp.zeros((CHUNK - t, RWKV_W), F32)
        r, k, v, lw, a, b, gate = (
            jnp.concatenate([piece for s in range(ns) for piece in (ref[t * s:t * (s + 1), :], zpad)], axis=0)
            for ref in tok_refs)
        s_in = jnp.concatenate([_heads_to_bd(s0_ref[s]) for s in range(ns)], axis=0)
    np_ = RWKV_HEADS // 2
    ones_bd = _ones_bd()
    inv_n = 1.0 / HEAD_DIM

    def store_heads(dst, s_bd):
        for j in range(np_):
            dst[2 * j] = s_bd[j, :HEAD_DIM, :HEAD_DIM]
            dst[2 * j + 1] = s_bd[j, HEAD_DIM:, HEAD_DIM:]

    def finish(y, sl):
        mean = _seg_sum(y, ones_bd) * inv_n
        d = y - mean
        var = _seg_sum(d * d, ones_bd) * inv_n
        yn = d * lax.rsqrt(var + GN_EPS) * lng_ref[...] + lnb_ref[...]
        bonus = _seg_sum(r[sl] * k[sl] * rk_ref[...], ones_bd) * v[sl]
        return (yn + bonus) * gate[sl]

    rows = r.shape[0]
    sub = min(rows, WKV_TILE) if chain else rows
    slices = [slice(s0, s0 + sub) for s0 in range(0, rows, sub)]
    prepare = lambda sl: _wkv_prepare(r[sl], k[sl], v[sl], lw[sl], a[sl], b[sl])
    s_out = s_in
    outs = []
    prep = prepare(slices[0])
    for j, sl in enumerate(slices):
        nxt = prepare(slices[j + 1]) if j + 1 < len(slices) else None
        y, s_out = _wkv_apply(s_out, prep, chain)
        outs.append(finish(y, sl))
        prep = nxt
        if companion is not None and j == 0:
            companion()
    out = outs[0] if len(outs) == 1 else jnp.concatenate(outs, axis=0)

    if chain:
        s_scr[...] = s_out
        o_ref[...] = out

        @pl.when(pl.program_id(1) == pl.num_programs(1) - 1)
        def _():
            store_heads(sout_ref, s_scr[...])
    else:
        for s in range(ns):
            store_heads(sout_ref.at[s], s_out[np_ * s:np_ * (s + 1)])
            o_ref[t * s:t * (s + 1), :] = out[CHUNK * s:CHUNK * s + t, :]


def _rwkv(tok, s0, post_w, n_seq, seq, tile_rows, chain):
    if chain:
        groups, nt, ns = n_seq, seq // tile_rows, None
    else:
        groups, nt, ns = n_seq * seq // tile_rows, 1, tile_rows // seq
    row = lambda g, i: (g * nt + i, 0)
    const = lambda g, i: (0, 0)
    state_spec = pl.BlockSpec((ns, RWKV_HEADS, HEAD_DIM, HEAD_DIM), lambda g, i: (g, 0, 0, 0))
    wide = pl.BlockSpec((tile_rows, RWKV_W), row)
    state_in = [] if s0 is None else [s0]
    return pl.pallas_call(
        functools.partial(_rwkv_kernel, chain=chain, zero_state=s0 is None, t=seq),
        grid=(groups, nt),
        in_specs=[wide] * 7 + [state_spec] * len(state_in) + [pl.BlockSpec(w.shape, const) for w in post_w],
        out_specs=[wide, state_spec],
        out_shape=[
            jax.ShapeDtypeStruct((n_seq * seq, RWKV_W), F32),
            jax.ShapeDtypeStruct((n_seq, RWKV_HEADS, HEAD_DIM, HEAD_DIM), F32),
        ],
        scratch_shapes=[pltpu.VMEM((RWKV_HEADS // 2, LANES, LANES), F32)],
        compiler_params=pltpu.CompilerParams(
            dimension_semantics=("parallel", "arbitrary"), vmem_limit_bytes=VMEM_LIMIT),
        name="rwkv",
    )(*tok, *state_in, *post_w)


def _mixers_prompt_kernel(q_ref, k_ref, v_ref, kh_ref, vh_ref, sink_ref, *refs, tq):
    *rwkv_refs, att_ref, rw_ref, sout_ref, s_scr = refs
    attention = functools.partial(_attn_prompt_kernel, q_ref, k_ref, v_ref, kh_ref, vh_ref, sink_ref,
                                  att_ref, tq=tq)
    _rwkv_kernel(*rwkv_refs, rw_ref, sout_ref, s_scr, chain=True, zero_state=True, t=tq,
                 companion=attention)


def _mixers_prompt(q, k, v, sinks, tok, post_w, batch, seq, tq):
    sink_tile = _sink_tile(sinks, CHUNK, WINDOW + CHUNK)
    nt = seq // tq
    row = lambda b, i: (b * nt + i, 0)
    const = lambda b, i: (0, 0)
    halo = lambda b, i: (jnp.maximum((b * nt + i) * (tq // WINDOW) - 1, 0), 0)
    wide = pl.BlockSpec((tq, RWKV_W), row)
    state_spec = pl.BlockSpec((None, RWKV_HEADS, HEAD_DIM, HEAD_DIM), lambda b, i: (b, 0, 0, 0))
    return pl.pallas_call(
        functools.partial(_mixers_prompt_kernel, tq=tq),
        grid=(batch, nt),
        in_specs=[
            pl.BlockSpec((tq, ATT_Q), row),
            pl.BlockSpec((tq, ATT_KV), row),
            pl.BlockSpec((tq, ATT_KV), row),
            pl.BlockSpec((WINDOW, ATT_KV), halo),
            pl.BlockSpec((WINDOW, ATT_KV), halo),
            pl.BlockSpec((ATT_HEADS * CHUNK, LANES), const),
        ] + [wide] * 7 + [pl.BlockSpec(w.shape, const) for w in post_w],
        out_specs=[pl.BlockSpec((tq, ATT_Q), row), wide, state_spec],
        out_shape=[
            jax.ShapeDtypeStruct((batch * seq, ATT_Q), F32),
            jax.ShapeDtypeStruct((batch * seq, RWKV_W), F32),
            jax.ShapeDtypeStruct((batch, RWKV_HEADS, HEAD_DIM, HEAD_DIM), F32),
        ],
        scratch_shapes=[pltpu.VMEM((RWKV_HEADS // 2, LANES, LANES), F32)],
        compiler_params=pltpu.CompilerParams(
            dimension_semantics=("parallel", "arbitrary"), vmem_limit_bytes=VMEM_LIMIT),
        name="mixers",
    )(q, k, v, k, v, sink_tile, *tok, *post_w)


def _ffn_kernel(x_ref, att_ref, rw_ref, woa_ref, wor_ref, g2_ref, wup_ref, wdn_ref, o_ref, *, tf):
    x2 = x_ref[...] + _dot(att_ref[...], woa_ref[...]) + _dot(rw_ref[...], wor_ref[...])
    ms = jnp.mean(x2 * x2, axis=-1, keepdims=True)
    h = (x2 * lax.rsqrt(ms + NORM_EPS) * g2_ref[...]).astype(BF16)
    us = []
    for j in range(wup_ref.shape[1] // tf):
        u = jnp.maximum(lax.dot_general(h, wup_ref[:, tf * j:tf * (j + 1)], NN,
                                        preferred_element_type=F32), 0.0)
        us.append((u * u).astype(BF16))
    u_all = jnp.concatenate(us, axis=1)
    o_ref[...] = x2 + lax.dot_general(u_all, wdn_ref[...], NN, preferred_element_type=F32)


def _out_ffn(x2d, att, rw, wo_att, wo_rw, g2, w_up, w_down, tm, tf):
    n, d = x2d.shape
    dff = w_up.shape[1]
    row = lambda i: (i, 0)
    const = lambda i: (0, 0)
    return pl.pallas_call(
        functools.partial(_ffn_kernel, tf=tf),
        grid=(n // tm,),
        in_specs=[
            pl.BlockSpec((tm, d), row),
            pl.BlockSpec((tm, ATT_Q), row),
            pl.BlockSpec((tm, RWKV_W), row),
            _resident((ATT_Q, d), const),
            _resident((RWKV_W, d), const),
            _resident((1, d), const),
            _resident((d, dff), const),
            _resident((dff, d), const),
        ],
        out_specs=pl.BlockSpec((tm, d), row),
        out_shape=jax.ShapeDtypeStruct((n, d), F32),
        compiler_params=pltpu.CompilerParams(
            dimension_semantics=("parallel",), vmem_limit_bytes=VMEM_LIMIT),
        name="out_ffn",
    )(x2d, att, rw, wo_att, wo_rw, g2, w_up, w_down)


def _rope_tables(pos):
    half = HEAD_DIM // 2
    inv = ROPE_THETA ** (-jnp.arange(half, dtype=F32) / half)
    ang = pos.astype(F32)[:, None] * inv[None, :]
    cos = jnp.cos(ang)
    sin = jnp.sin(ang)
    zero = jnp.zeros_like(sin)
    tile = lambda t: jnp.concatenate([t, t], axis=1)
    return (tile(jnp.concatenate([cos, cos], axis=1)),
            tile(jnp.concatenate([-sin, zero], axis=1)),
            tile(jnp.concatenate([zero, sin], axis=1)))


def _pair_perm():
    idx = []
    for j in range(ATT_HEADS // 2):
        idx += list(range(HEAD_DIM * j, HEAD_DIM * (j + 1)))
        idx += list(range(HEAD_DIM * (j + 4), HEAD_DIM * (j + 5)))
    return np.asarray(idx, dtype=np.int32)


def _pad_rw(t):
    return jnp.pad(t, [(0, 0)] * (t.ndim - 1) + [(0, RW_COLS - RW_REAL)])


def _layer(x, tabs, k_past, v_past, shift_prev, wkv0, lw, tm):
    (att_w, rw_pre_w, rw_post_w, sink8, wo_att, wo_rw, g2, w_up, w_down) = lw
    b, t, d = x.shape
    n = b * t
    x2d = x.reshape(n, d)
    q, k, v, *tok, tail = _inproj(x2d, _pad_rw(shift_prev)[:, None, :], att_w, rw_pre_w, tabs, tm, t)

    if k_past is None:
        att, rw, wkv = _mixers_prompt(q, k, v, sink8, tok, rw_post_w, b, t, min(t, 2 * WKV_TILE))
        rows = min(WINDOW, t)
        last = lambda a: a.reshape(b, t, ATT_KV)[:, t - rows:].reshape(b, rows, ATT_KV_HEADS, HEAD_DIM)
        new_k, new_v = last(k), last(v)
        shift_out = tail[:, SUBLANES - 1, :RW_REAL]
    else:
        per_tile = max(m for m in (4, 2, 1) if b % m == 0)
        att = _attn_sample(q, k, v, k_past.reshape(-1, ATT_KV), v_past.reshape(-1, ATT_KV),
                           sink8, b, t, per_tile)
        new_k = k.reshape(b, t, ATT_KV_HEADS, HEAD_DIM)
        new_v = v.reshape(b, t, ATT_KV_HEADS, HEAD_DIM)
        shift_out = tail.reshape(b, t, RW_COLS)[:, t - 1, :RW_REAL]
        rw, wkv = _rwkv(tok, wkv0, rw_post_w, b, t, per_tile * t, False)
    y = _out_ffn(x2d, att, rw, wo_att, wo_rw, g2, w_up, w_down, tm, 1024)
    return (y.reshape(b, t, d), new_k, new_v, wkv, shift_out)


def kernel(x_prompt, x_sample, cache_attn_k, cache_attn_v, state_rwkv_wkv, state_rwkv_shift, ln1_g, w_in, q_norm_g, k_norm_g, attn_sinks, shift_mu, decay_w0, decay_w2, iclr_a0, iclr_a2, gate_g2, k_k, k_a, r_k, lnx_g, lnx_b, w_out, ln2_g, w_up, w_down):
    bp, tp, d = x_prompt.shape
    bs, ts, _ = x_sample.shape
    depth = w_in.shape[0]
    perm = _pair_perm()
    tabs_p = _rope_tables(jnp.arange(tp))
    tabs_s = tuple(jnp.tile(t, (bs, 1)) for t in _rope_tables(PAST_LEN + jnp.arange(ts)))
    tm_p = min(512, bp * tp)
    tm_s = min(512, bs * ts)

    hp, hs = x_prompt, x_sample
    outs_p, outs_s = [], []
    for l in range(depth):
        wl = w_in[l]
        att_cols = ATT_Q + 2 * ATT_KV
        w_in_p = jnp.concatenate(
            [wl[:, :ATT_Q][:, perm], wl[:, ATT_Q:att_cols], _pad_rw(wl[:, att_cols:])], axis=1).astype(BF16)
        row2 = lambda t: t.reshape(1, -1)
        zeros64 = jnp.zeros((64, RWKV_W), F32)
        att_w = (row2(ln1_g[l]), w_in_p, row2(jnp.tile(q_norm_g[l], ATT_HEADS)),
                 row2(jnp.tile(k_norm_g[l], ATT_KV_HEADS)))
        rw_pre_w = (
            row2(_pad_rw(shift_mu[l])),
            row2(decay_w0[l]),
            jnp.concatenate([decay_w2[l], zeros64], axis=0).astype(BF16),
            row2(iclr_a0[l]),
            jnp.concatenate([zeros64, iclr_a2[l]], axis=0).astype(BF16),
            jnp.pad(gate_g2[l], ((0, GATE_PAD - gate_g2.shape[1]), (0, 0))).astype(BF16),
            row2(k_k[l]), row2(k_a[l]),
        )
        rw_post_w = (row2(r_k[l]), row2(lnx_g[l]), row2(lnx_b[l]))
        lw = (att_w, rw_pre_w, rw_post_w, attn_sinks[l],
              w_out[l][:ATT_Q][perm].astype(BF16), w_out[l][ATT_Q:].astype(BF16),
              row2(ln2_g[l]), w_up[l].astype(BF16), w_down[l].astype(BF16))
        zero_shift = jnp.zeros((bp, RW_REAL), F32)
        hp, *op = _layer(hp, tabs_p, None, None, zero_shift, None, lw, tm_p)
        hs, *os_ = _layer(hs, tabs_s, cache_attn_k[l], cache_attn_v[l], state_rwkv_shift[l],
                          state_rwkv_wkv[l], lw, tm_s)
        outs_p.append(op)
        outs_s.append(os_)
    stack = lambda outs, i: jnp.stack([o[i] for o in outs])
    return (hp, hs,
            stack(outs_p, 0), stack(outs_p, 1), stack(outs_p, 2), stack(outs_p, 3),
            stack(outs_s, 0), stack(outs_s, 1), stack(outs_s, 2), stack(outs_s, 3))
```

```python
import functools

import jax
import jax.numpy as jnp
import numpy as np
from jax import lax
from jax.experimental import pallas as pl
from jax.experimental.pallas import tpu as pltpu

F32 = jnp.float32
BF16 = jnp.bfloat16

CHUNK = 64
WINDOW = 128
HEAD_DIM = 64
ATT_HEADS = 8
ATT_KV_HEADS = 2
RWKV_HEADS = 8
RWKV_W = 512
ATT_Q = 512
ATT_KV = 128
LORA_WA = 128
GATE_PAD = 256
RW_COLS = 3 * RWKV_W + LORA_WA + GATE_PAD
RW_REAL = 3 * RWKV_W + 64 + 64 + 160
IN_COLS_PAD = ATT_Q + 2 * ATT_KV + RW_COLS
PAST_LEN = 4096
ROPE_THETA = 10000.0
ATT_SCALE = HEAD_DIM ** -0.5
NORM_EPS = 1e-6
GN_EPS = 64e-5
DECAY_SCALE = float(np.exp(-0.5))
LANES = 128
SUBLANES = 8
WKV_TILE = 256
VMEM_LIMIT = 52 * 1024 * 1024

NN = (((1,), (0,)), ((), ()))
NT = (((1,), (1,)), ((), ()))
BNN = (((2,), (1,)), ((0,), (0,)))
BNT = (((2,), (2,)), ((0,), (0,)))
BTN = (((1,), (1,)), ((0,), (0,)))


def _dot(a, b, dims=NN):
    return lax.dot_general(a.astype(BF16), b.astype(BF16), dims, preferred_element_type=F32)


def _split2(x):
    hi = x.astype(BF16)
    lo = (x - hi.astype(F32)).astype(BF16)
    return hi, lo


def _seg_sum(x, ones_bd):
    outs = []
    for j in range(x.shape[1] // LANES):
        hi, lo = _split2(x[:, LANES * j:LANES * (j + 1)])
        outs.append(lax.dot_general(jnp.concatenate([hi, lo], axis=1), ones_bd, NN,
                                    preferred_element_type=F32))
    return outs[0] if len(outs) == 1 else jnp.concatenate(outs, axis=1)


def _ones_bd():
    r = lax.broadcasted_iota(jnp.int32, (2 * LANES, LANES), 0) & (LANES - 1)
    c = lax.broadcasted_iota(jnp.int32, (2 * LANES, LANES), 1)
    return jnp.where((r < HEAD_DIM) == (c < HEAD_DIM), 1.0, 0.0).astype(BF16)


def _lane_lo(shape):
    return lax.broadcasted_iota(jnp.int32, shape, len(shape) - 1) < HEAD_DIM


_resident = functools.partial(pl.BlockSpec, pipeline_mode=pl.Buffered(1))


def _inproj_kernel(x_ref, sprev_ref, g1_ref, w_ref, qg_ref, kg_ref, cos_ref, sa_ref, sb_ref,
                   mu_ref, w0_ref, w2_ref, a0_ref, a2_ref, g2_ref, kk_ref, ka_ref,
                   q_ref, k_ref, v_ref, r_ref, km_ref, vv_ref, lw_ref, a_ref, b_ref, gate_ref,
                   tail_ref, carry_scr, *, seq):
    tm = x_ref.shape[0]
    rows = lax.broadcasted_iota(jnp.int32, (tm, 1), 0)
    rw0 = ATT_Q + 2 * ATT_KV
    ones_bd = _ones_bd()
    x = x_ref[...]
    ms = jnp.mean(x * x, axis=-1, keepdims=True)
    h = (x * lax.rsqrt(ms + NORM_EPS) * g1_ref[...]).astype(BF16)

    def project(c0, c1):
        return lax.dot_general(h, w_ref[:, c0:c1], NN, preferred_element_type=F32)

    if seq >= tm:
        @pl.when(pl.program_id(0) % (seq // tm) == 0)
        def _():
            carry_scr[0:1, :] = sprev_ref[...]

    def shifted(c0, c1):
        p = project(rw0 + c0, rw0 + c1)
        prev = pltpu.roll(p, 1, 0)
        if seq >= tm:
            prev = jnp.where(rows == 0, carry_scr[0:1, c0:c1], prev)
            carry_scr[0:1, c0:c1] = p[tm - 1:tm, :]
            tail_ref[:, c0:c1] = p[tm - SUBLANES:tm, :]
        else:
            for s in range(tm // seq):
                prev = jnp.where(rows == s * seq, sprev_ref[s, :, c0:c1], prev)
            tail_ref[:, c0:c1] = p
        return p + (prev - p) * mu_ref[:, c0:c1]

    cos = cos_ref[...]
    sin_a = sa_ref[...]
    sin_b = sb_ref[...]

    def norm_rope(p, g):
        ss = _seg_sum(p * p, ones_bd)
        y = p * lax.rsqrt(ss * (1.0 / HEAD_DIM) + NORM_EPS) * g
        outs = []
        for j in range(p.shape[1] // LANES):
            yb = y[:, LANES * j:LANES * (j + 1)]
            outs.append(yb * cos + pltpu.roll(yb, LANES - 32, 1) * sin_a + pltpu.roll(yb, 32, 1) * sin_b)
        return outs[0] if len(outs) == 1 else jnp.concatenate(outs, axis=1)

    lora = shifted(3 * RWKV_W, RW_COLS)
    k = shifted(RWKV_W, 2 * RWKV_W)
    sigmoid = lambda z: 0.5 + 0.5 * jnp.tanh(0.5 * z)

    def token_maps(rs):
        wa = lora[rs, :LORA_WA]
        gd = lora[rs, LORA_WA:]
        kr = k[rs]
        zw = w0_ref[...] + _dot(jnp.tanh(wa), w2_ref[...])
        lw_ref[rs, :] = -DECAY_SCALE * sigmoid(zw)
        a_ic = sigmoid(a0_ref[...] + _dot(wa, a2_ref[...]))
        gate_ref[rs, :] = _dot(sigmoid(gd), g2_ref[...])
        kk = kr * kk_ref[...]
        kk = kk * lax.rsqrt(jnp.maximum(_seg_sum(kk * kk, ones_bd), 1e-24))
        km_ref[rs, :] = kr * (1.0 + (a_ic - 1.0) * ka_ref[...])
        a_ref[rs, :] = -kk
        b_ref[rs, :] = kk * a_ic

    r_ref[...] = shifted(0, RWKV_W)
    vv_ref[...] = shifted(2 * RWKV_W, 3 * RWKV_W)
    q_ref[...] = norm_rope(project(0, ATT_Q), qg_ref[...])
    pkv = project(ATT_Q, ATT_Q + 2 * ATT_KV)
    k_ref[...] = norm_rope(pkv[:, :ATT_KV], kg_ref[...])
    v_ref[...] = pkv[:, ATT_KV:]
    token_maps(slice(0, tm))


def _inproj(x2d, sprev, att_w, rw_w, tabs, tm, seq):
    n, d = x2d.shape
    g1, w_in_p, qg, kg = att_w
    cos_t, sin_a, sin_b = tabs
    tab_blocks = cos_t.shape[0] // tm
    row = lambda i: (i, 0)
    const = lambda i: (0, 0)
    tab = lambda i: (i % tab_blocks, 0)
    if seq >= tm:
        tps = seq // tm
        sprev_spec = pl.BlockSpec((None, 1, RW_COLS), lambda i: (i // tps, 0, 0))
        tail_spec = pl.BlockSpec((None, SUBLANES, RW_COLS), lambda i: (i // tps, 0, 0))
        tail_shape = jax.ShapeDtypeStruct((n // seq, SUBLANES, RW_COLS), F32)
    else:
        sprev_spec = pl.BlockSpec((tm // seq, 1, RW_COLS), lambda i: (i, 0, 0))
        tail_spec = pl.BlockSpec((tm, RW_COLS), row)
        tail_shape = jax.ShapeDtypeStruct((n, RW_COLS), F32)
    wide = pl.BlockSpec((tm, RWKV_W), row)
    wide_shape = jax.ShapeDtypeStruct((n, RWKV_W), F32)
    return pl.pallas_call(
        functools.partial(_inproj_kernel, seq=seq),
        grid=(n // tm,),
        in_specs=[
            pl.BlockSpec((tm, d), row),
            sprev_spec,
            _resident((1, d), const),
            _resident((d, IN_COLS_PAD), const),
            _resident((1, ATT_Q), const),
            _resident((1, ATT_KV), const),
            pl.BlockSpec((tm, LANES), tab),
            pl.BlockSpec((tm, LANES), tab),
            pl.BlockSpec((tm, LANES), tab),
        ] + [_resident(w.shape, const) for w in rw_w],
        out_specs=[
            pl.BlockSpec((tm, ATT_Q), row),
            pl.BlockSpec((tm, ATT_KV), row),
            pl.BlockSpec((tm, ATT_KV), row),
        ] + [wide] * 7 + [tail_spec],
        out_shape=[
            jax.ShapeDtypeStruct((n, ATT_Q), F32),
            jax.ShapeDtypeStruct((n, ATT_KV), F32),
            jax.ShapeDtypeStruct((n, ATT_KV), F32),
        ] + [wide_shape] * 7 + [tail_shape],
        scratch_shapes=[pltpu.VMEM((SUBLANES, RW_COLS), F32)],
        compiler_params=pltpu.CompilerParams(
            dimension_semantics=("arbitrary",), vmem_limit_bytes=VMEM_LIMIT),
        name="inproj",
    )(x2d, sprev, g1, w_in_p, qg, kg, cos_t, sin_a, sin_b, *rw_w)


def _attend(q, kc, vc, sink_tile, first_key_pos):
    tq = q.shape[0]
    nk = kc.shape[0]
    pad = 2 * LANES - nk
    lo = _lane_lo((tq, LANES))
    blocks = [q[:, LANES * j:LANES * (j + 1)] * ATT_SCALE for j in range(ATT_HEADS // 2)]
    stacked = jnp.concatenate([jnp.where(lo, b, 0.0) for b in blocks]
                              + [jnp.where(lo, 0.0, b) for b in blocks], axis=0).astype(BF16)
    zpad = jnp.zeros((pad, LANES), BF16)
    k_ext = jnp.concatenate([kc, zpad], axis=0)
    v_ones = jnp.concatenate([jnp.concatenate([vc, zpad], axis=0),
                              jnp.ones((2 * LANES, LANES), BF16)], axis=1)
    s = lax.dot_general(stacked, k_ext, NT, preferred_element_type=F32)
    lane = lax.broadcasted_iota(jnp.int32, (ATT_HEADS * tq, LANES), 1)
    s0 = s[:, :LANES]
    s1 = s[:, LANES:]
    if first_key_pos is not None:
        s0 = jnp.where(lane + first_key_pos >= 0, s0, -jnp.inf)
        s1 = jnp.where(lane + (first_key_pos + LANES) >= 0, s1, -jnp.inf)
    s1 = jnp.where(lane < nk - LANES, s1, sink_tile)
    m = jnp.max(jnp.maximum(s0, s1), axis=-1, keepdims=True)
    p = jnp.concatenate([jnp.exp(s0 - m), jnp.exp(s1 - m)], axis=1).astype(BF16)
    od = lax.dot_general(p, v_ones, NN, preferred_element_type=F32)
    o = od[:, :LANES] / od[:, LANES:]
    half = (ATT_HEADS // 2) * tq
    outs = [jnp.where(lo, o[tq * j:tq * (j + 1)], o[half + tq * j:half + tq * (j + 1)])
            for j in range(ATT_HEADS // 2)]
    return jnp.concatenate(outs, axis=1)


def _attn_prompt_kernel(q_ref, k_ref, v_ref, kh_ref, vh_ref, sink_ref, o_ref, *, tq):
    i = pl.program_id(1)
    kcat = jnp.concatenate([kh_ref[...], k_ref[...]], axis=0).astype(BF16)
    vcat = jnp.concatenate([vh_ref[...], v_ref[...]], axis=0).astype(BF16)
    sink_tile = sink_ref[...]
    span = WINDOW + CHUNK
    for c in range(tq // CHUNK):
        first_key_pos = i * tq + c * CHUNK - WINDOW if c < WINDOW // CHUNK else None
        o_ref[CHUNK * c:CHUNK * (c + 1), :] = _attend(
            q_ref[CHUNK * c:CHUNK * (c + 1), :], kcat[CHUNK * c:CHUNK * c + span],
            vcat[CHUNK * c:CHUNK * c + span], sink_tile, first_key_pos)


def _sink_tile(sinks, rows_per_head, nk):
    lane = jnp.arange(LANES)[None, :]
    col = jnp.repeat(sinks.astype(F32), rows_per_head)[:, None]
    return jnp.where(lane == nk - LANES, col, jnp.where(lane > nk - LANES, -jnp.inf, 0.0))


def _attn_sample_kernel(q_ref, k_ref, v_ref, kc_ref, vc_ref, sink_ref, o_ref, *, t):
    ns = q_ref.shape[0] // t
    rows = kc_ref.shape[0] // ns
    for s in range(ns):
        new = slice(t * s, t * (s + 1))
        old = slice(rows * s, rows * (s + 1))
        kall = jnp.concatenate([kc_ref[old, :], k_ref[new, :]], axis=0).astype(BF16)
        vall = jnp.concatenate([vc_ref[old, :], v_ref[new, :]], axis=0).astype(BF16)
        o_ref[new, :] = _attend(q_ref[new, :], kall, vall, sink_ref[...], None)


def _attn_sample(q, k, v, k_cache, v_cache, sinks, batch, t, per_step):
    rows = k_cache.shape[0] // batch
    sink_tile = _sink_tile(sinks, t, rows + t)
    row = lambda b: (b, 0)
    return pl.pallas_call(
        functools.partial(_attn_sample_kernel, t=t),
        grid=(batch // per_step,),
        in_specs=[
            pl.BlockSpec((per_step * t, ATT_Q), row),
            pl.BlockSpec((per_step * t, ATT_KV), row),
            pl.BlockSpec((per_step * t, ATT_KV), row),
            pl.BlockSpec((per_step * rows, ATT_KV), row),
            pl.BlockSpec((per_step * rows, ATT_KV), row),
            pl.BlockSpec((ATT_HEADS * t, LANES), lambda b: (0, 0)),
        ],
        out_specs=pl.BlockSpec((per_step * t, ATT_Q), row),
        out_shape=jax.ShapeDtypeStruct((batch * t, ATT_Q), F32),
        compiler_params=pltpu.CompilerParams(dimension_semantics=("parallel",)),
        name="attn_sample",
    )(q, k, v, k_cache, v_cache, sink_tile)


def _systems(x, rows):
    nc = x.shape[0] // rows
    return jnp.stack([x[rows * c:rows * (c + 1), LANES * j:LANES * (j + 1)]
                      for c in range(nc) for j in range(RWKV_HEADS // 2)])


def _bdot(a, b, dims):
    return lax.dot_general(a.astype(BF16), b.astype(BF16), dims, preferred_element_type=F32)


def _wkv_prepare(r, k, v, lw, a, b):
    c = CHUNK
    tt = r.shape[0]
    nc = tt // c
    np_ = RWKV_HEADS // 2
    ri = lax.broadcasted_iota(jnp.int32, (tt, tt), 0)
    ci = lax.broadcasted_iota(jnp.int32, (tt, tt), 1)
    tri = jnp.where((ri >= ci) & ((ri & -c) == (ci & -c)), 1.0, 0.0).astype(BF16)
    hi = lw.astype(BF16)
    rem = lw - hi.astype(F32)
    mid = rem.astype(BF16)
    low = (rem - mid.astype(F32)).astype(BF16)
    cum = (lax.dot_general(tri, hi, NN, preferred_element_type=F32)
           + lax.dot_general(tri, mid, NN, preferred_element_type=F32)
           + lax.dot_general(tri, low, NN, preferred_element_type=F32))
    e = jnp.exp(cum)
    e_inv = jnp.exp(-cum)
    e_x = jnp.exp(cum - lw)
    rt = r * e
    at = a * e_x
    kt = k * e_inv
    bt = b * e_inv

    lo = _lane_lo((1, 1, LANES))
    at_p, rt_p, kt_p, bt_p, v_p = (_systems(t, c) for t in (at, rt, kt, bt, v))
    ec_p = jnp.stack([e[c * i + c - 1:c * i + c, LANES * j:LANES * (j + 1)]
                      for i in range(nc) for j in range(np_)])
    kh_p = kt_p * ec_p
    bh_p = bt_p * ec_p
    at0 = jnp.where(lo, at_p, 0.0)
    at1 = jnp.where(lo, 0.0, at_p)
    rt0 = jnp.where(lo, rt_p, 0.0)
    rt1 = jnp.where(lo, 0.0, rt_p)
    lhs = jnp.concatenate([at0, at1, rt0, rt1], axis=1)
    rhs = jnp.concatenate([bt_p, kt_p], axis=1)
    g = _bdot(lhs, rhs, BNT)

    r128 = lax.broadcasted_iota(jnp.int32, (1, LANES, LANES), 1)
    l128 = lax.broadcasted_iota(jnp.int32, (1, LANES, LANES), 2)
    t_idx = r128 & (c - 1)
    s_idx = l128 & (c - 1)
    ga = jnp.where(s_idx < t_idx, g[:, :LANES], 0.0)
    gr = jnp.where(s_idx <= t_idx, g[:, LANES:], 0.0)
    same = (r128 < c) == (l128 < c)

    def block_diag(xc):
        return jnp.concatenate([jnp.where(lo, xc, 0), jnp.where(lo, 0, xc)], axis=1)

    def head_rows(x):
        return jnp.concatenate([x[:, :c], x[:, c:]], axis=2)

    a_c = jnp.where(lo, ga[:, :c], jnp.stack(
        [pltpu.roll(ga[j, c:], c, 1) for j in range(ga.shape[0])]))
    r64 = lax.broadcasted_iota(jnp.int32, (1, c, LANES), 1)
    l64 = lax.broadcasted_iota(jnp.int32, (1, c, LANES), 2)
    t_c = jnp.where((l64 & (c - 1)) == r64, 1.0, 0.0) + a_c
    p_c = _bdot(a_c, block_diag(a_c), BNN)
    for level in range(5):
        last = level == 4
        lhs_tp = t_c if last else jnp.concatenate([t_c, p_c], axis=1)
        prod = _bdot(lhs_tp, block_diag(p_c), BNN)
        t_c = t_c + prod[:, :c]
        if not last:
            p_c = prod[:, c:]

    zeros = jnp.zeros_like(v_p)
    zv = jnp.concatenate([zeros, jnp.where(lo, v_p, 0.0), zeros, jnp.where(lo, 0.0, v_p)], axis=1)
    xak = _bdot(head_rows(ga), zv, BNN)
    z = jnp.concatenate([jnp.concatenate([at0, at1], axis=1), block_diag(xak)], axis=2)
    tzs = _bdot(t_c, z, BNN)
    q_f32 = jnp.concatenate([tzs, jnp.concatenate([zeros, v_p], axis=2)], axis=1)
    q_mat = q_f32.astype(BF16)
    gq = _bdot(gr, q_mat, BNN)
    r_eff = rt_p + jnp.where(lo, gq[:, :c, :LANES], gq[:, c:, :LANES])
    y0 = jnp.where(lo, gq[:, :c, LANES:], gq[:, c:, LANES:])
    bk = jnp.concatenate([bh_p, kh_p], axis=1)
    mn = _bdot(q_mat, bk, BTN)
    wb = jnp.where(same, mn[:, :LANES], 0.0)
    n_mat = jnp.where(same, mn[:, LANES:], 0.0)

    return r_eff, y0, wb, n_mat, ec_p


def _wkv_apply(s_bd, prep, chain):
    r_eff, y0, wb, n_mat, ec_p = prep
    np_ = RWKV_HEADS // 2
    nc = r_eff.shape[0] // np_
    flat = lambda y_i: jnp.concatenate([y_i[j] for j in range(np_)], axis=1)
    if chain:
        ys = []
        for i in range(nc):
            sl = slice(np_ * i, np_ * (i + 1))
            s_bf = s_bd.astype(BF16)
            ys.append(flat(_bdot(r_eff[sl], s_bf, BNT) + y0[sl]))
            s_bd = s_bd * ec_p[sl] + _bdot(s_bf, wb[sl], BNN) + n_mat[sl]
    else:
        s_bf = s_bd.astype(BF16)
        y_all = _bdot(r_eff, s_bf, BNT) + y0
        ys = [flat(y_all[np_ * i:np_ * (i + 1)]) for i in range(nc)]
        s_bd = s_bd * ec_p + _bdot(s_bf, wb, BNN) + n_mat
    y = ys[0] if nc == 1 else jnp.concatenate(ys, axis=0)
    return y, s_bd


def _heads_to_bd(s8):
    z = jnp.zeros((HEAD_DIM, HEAD_DIM), F32)
    return jnp.stack([jnp.concatenate([jnp.concatenate([s8[2 * j], z], axis=1),
                                       jnp.concatenate([z, s8[2 * j + 1]], axis=1)], axis=0)
                      for j in range(RWKV_HEADS // 2)])


def _rwkv_kernel(*refs, chain, zero_state, t, companion=None):
    tok_refs, refs = refs[:7], refs[7:]
    if not zero_state:
        s0_ref, refs = refs[0], refs[1:]
    rk_ref, lng_ref, lnb_ref, o_ref, sout_ref, s_scr = refs
    if chain:
        r, k, v, lw, a, b, gate = (ref[...] for ref in tok_refs)
        @pl.when(pl.program_id(1) == 0)
        def _():
            s_scr[...] = jnp.zeros_like(s_scr) if zero_state else _heads_to_bd(s0_ref[...])
        s_in = s_scr[...]
    else:
        ns = tok_refs[0].shape[0] // t
        zpad = jnp.zeros((CHUNK - t, RWKV_W), F32)
        r, k, v, lw, a, b, gate = (
            jnp.concatenate([piece for s in range(ns) for piece in (ref[t * s:t * (s + 1), :], zpad)], axis=0)
            for ref in tok_refs)
        s_in = jnp.concatenate([_heads_to_bd(s0_ref[s]) for s in range(ns)], axis=0)
    np_ = RWKV_HEADS // 2
    ones_bd = _ones_bd()
    inv_n = 1.0 / HEAD_DIM

    def store_heads(dst, s_bd):
        for j in range(np_):
            dst[2 * j] = s_bd[j, :HEAD_DIM, :HEAD_DIM]
            dst[2 * j + 1] = s_bd[j, HEAD_DIM:, HEAD_DIM:]

    def finish(y, sl):
        mean = _seg_sum(y, ones_bd) * inv_n
        d = y - mean
        var = _seg_sum(d * d, ones_bd) * inv_n
        yn = d * lax.rsqrt(var + GN_EPS) * lng_ref[...] + lnb_ref[...]
        bonus = _seg_sum(r[sl] * k[sl] * rk_ref[...], ones_bd) * v[sl]
        return (yn + bonus) * gate[sl]

    rows = r.shape[0]
    sub = min(rows, WKV_TILE) if chain else rows
    slices = [slice(s0, s0 + sub) for s0 in range(0, rows, sub)]
    prepare = lambda sl: _wkv_prepare(r[sl], k[sl], v[sl], lw[sl], a[sl], b[sl])
    s_out = s_in
    outs = []
    prep = prepare(slices[0])
    for j, sl in enumerate(slices):
        nxt = prepare(slices[j + 1]) if j + 1 < len(slices) else None
        y, s_out = _wkv_apply(s_out, prep, chain)
        outs.append(finish(y, sl))
        prep = nxt
        if companion is not None and j == 0:
            companion()
    out = outs[0] if len(outs) == 1 else jnp.concatenate(outs, axis=0)

    if chain:
        s_scr[...] = s_out
        o_ref[...] = out

        @pl.when(pl.program_id(1) == pl.num_programs(1) - 1)
        def _():
            store_heads(sout_ref, s_scr[...])
    else:
        for s in range(ns):
            store_heads(sout_ref.at[s], s_out[np_ * s:np_ * (s + 1)])
            o_ref[t * s:t * (s + 1), :] = out[CHUNK * s:CHUNK * s + t, :]


def _rwkv(tok, s0, post_w, n_seq, seq, tile_rows, chain):
    if chain:
        groups, nt, ns = n_seq, seq // tile_rows, None
    else:
        groups, nt, ns = n_seq * seq // tile_rows, 1, tile_rows // seq
    row = lambda g, i: (g * nt + i, 0)
    const = lambda g, i: (0, 0)
    state_spec = pl.BlockSpec((ns, RWKV_HEADS, HEAD_DIM, HEAD_DIM), lambda g, i: (g, 0, 0, 0))
    wide = pl.BlockSpec((tile_rows, RWKV_W), row)
    state_in = [] if s0 is None else [s0]
    return pl.pallas_call(
        functools.partial(_rwkv_kernel, chain=chain, zero_state=s0 is None, t=seq),
        grid=(groups, nt),
        in_specs=[wide] * 7 + [state_spec] * len(state_in) + [pl.BlockSpec(w.shape, const) for w in post_w],
        out_specs=[wide, state_spec],
        out_shape=[
            jax.ShapeDtypeStruct((n_seq * seq, RWKV_W), F32),
            jax.ShapeDtypeStruct((n_seq, RWKV_HEADS, HEAD_DIM, HEAD_DIM), F32),
        ],
        scratch_shapes=[pltpu.VMEM((RWKV_HEADS // 2, LANES, LANES), F32)],
        compiler_params=pltpu.CompilerParams(
            dimension_semantics=("parallel", "arbitrary"), vmem_limit_bytes=VMEM_LIMIT),
        name="rwkv",
    )(*tok, *state_in, *post_w)


def _mixers_prompt_kernel(q_ref, k_ref, v_ref, kh_ref, vh_ref, sink_ref, *refs, tq):
    *rwkv_refs, att_ref, rw_ref, sout_ref, s_scr = refs
    attention = functools.partial(_attn_prompt_kernel, q_ref, k_ref, v_ref, kh_ref, vh_ref, sink_ref,
                                  att_ref, tq=tq)
    _rwkv_kernel(*rwkv_refs, rw_ref, sout_ref, s_scr, chain=True, zero_state=True, t=tq,
                 companion=attention)


def _mixers_prompt(q, k, v, sinks, tok, post_w, batch, seq, tq):
    sink_tile = _sink_tile(sinks, CHUNK, WINDOW + CHUNK)
    nt = seq // tq
    row = lambda b, i: (b * nt + i, 0)
    const = lambda b, i: (0, 0)
    halo = lambda b, i: (jnp.maximum((b * nt + i) * (tq // WINDOW) - 1, 0), 0)
    wide = pl.BlockSpec((tq, RWKV_W), row)
    state_spec = pl.BlockSpec((None, RWKV_HEADS, HEAD_DIM, HEAD_DIM), lambda b, i: (b, 0, 0, 0))
    return pl.pallas_call(
        functools.partial(_mixers_prompt_kernel, tq=tq),
        grid=(batch, nt),
        in_specs=[
            pl.BlockSpec((tq, ATT_Q), row),
            pl.BlockSpec((tq, ATT_KV), row),
            pl.BlockSpec((tq, ATT_KV), row),
            pl.BlockSpec((WINDOW, ATT_KV), halo),
            pl.BlockSpec((WINDOW, ATT_KV), halo),
            pl.BlockSpec((ATT_HEADS * CHUNK, LANES), const),
        ] + [wide] * 7 + [pl.BlockSpec(w.shape, const) for w in post_w],
        out_specs=[pl.BlockSpec((tq, ATT_Q), row), wide, state_spec],
        out_shape=[
            jax.ShapeDtypeStruct((batch * seq, ATT_Q), F32),
            jax.ShapeDtypeStruct((batch * seq, RWKV_W), F32),
            jax.ShapeDtypeStruct((batch, RWKV_HEADS, HEAD_DIM, HEAD_DIM), F32),
        ],
        scratch_shapes=[pltpu.VMEM((RWKV_HEADS // 2, LANES, LANES), F32)],
        compiler_params=pltpu.CompilerParams(
            dimension_semantics=("parallel", "arbitrary"), vmem_limit_bytes=VMEM_LIMIT),
        name="mixers",
    )(q, k, v, k, v, sink_tile, *tok, *post_w)


def _ffn_kernel(x_ref, att_ref, rw_ref, woa_ref, wor_ref, g2_ref, wup_ref, wdn_ref, o_ref, *, tf):
    x2 = x_ref[...] + _dot(att_ref[...], woa_ref[...]) + _dot(rw_ref[...], wor_ref[...])
    ms = jnp.mean(x2 * x2, axis=-1, keepdims=True)
    h = (x2 * lax.rsqrt(ms + NORM_EPS) * g2_ref[...]).astype(BF16)
    us = []
    for j in range(wup_ref.shape[1] // tf):
        u = jnp.maximum(lax.dot_general(h, wup_ref[:, tf * j:tf * (j + 1)], NN,
                                        preferred_element_type=F32), 0.0)
        us.append((u * u).astype(BF16))
    u_all = jnp.concatenate(us, axis=1)
    o_ref[...] = x2 + lax.dot_general(u_all, wdn_ref[...], NN, preferred_element_type=F32)


def _out_ffn(x2d, att, rw, wo_att, wo_rw, g2, w_up, w_down, tm, tf):
    n, d = x2d.shape
    dff = w_up.shape[1]
    row = lambda i: (i, 0)
    const = lambda i: (0, 0)
    return pl.pallas_call(
        functools.partial(_ffn_kernel, tf=tf),
        grid=(n // tm,),
        in_specs=[
            pl.BlockSpec((tm, d), row),
            pl.BlockSpec((tm, ATT_Q), row),
            pl.BlockSpec((tm, RWKV_W), row),
            _resident((ATT_Q, d), const),
            _resident((RWKV_W, d), const),
            _resident((1, d), const),
            _resident((d, dff), const),
            _resident((dff, d), const),
        ],
        out_specs=pl.BlockSpec((tm, d), row),
        out_shape=jax.ShapeDtypeStruct((n, d), F32),
        compiler_params=pltpu.CompilerParams(
            dimension_semantics=("parallel",), vmem_limit_bytes=VMEM_LIMIT),
        name="out_ffn",
    )(x2d, att, rw, wo_att, wo_rw, g2, w_up, w_down)


def _rope_tables(pos):
    half = HEAD_DIM // 2
    inv = ROPE_THETA ** (-jnp.arange(half, dtype=F32) / half)
    ang = pos.astype(F32)[:, None] * inv[None, :]
    cos = jnp.cos(ang)
    sin = jnp.sin(ang)
    zero = jnp.zeros_like(sin)
    tile = lambda t: jnp.concatenate([t, t], axis=1)
    return (tile(jnp.concatenate([cos, cos], axis=1)),
            tile(jnp.concatenate([-sin, zero], axis=1)),
            tile(jnp.concatenate([zero, sin], axis=1)))


def _pair_perm():
    idx = []
    for j in range(ATT_HEADS // 2):
        idx += list(range(HEAD_DIM * j, HEAD_DIM * (j + 1)))
        idx += list(range(HEAD_DIM * (j + 4), HEAD_DIM * (j + 5)))
    return np.asarray(idx, dtype=np.int32)


def _pad_rw(t):
    return jnp.pad(t, [(0, 0)] * (t.ndim - 1) + [(0, RW_COLS - RW_REAL)])


def _layer(x, tabs, k_past, v_past, shift_prev, wkv0, lw, tm):
    (att_w, rw_pre_w, rw_post_w, sink8, wo_att, wo_rw, g2, w_up, w_down) = lw
    b, t, d = x.shape
    n = b * t
    x2d = x.reshape(n, d)
    q, k, v, *tok, tail = _inproj(x2d, _pad_rw(shift_prev)[:, None, :], att_w, rw_pre_w, tabs, tm, t)

    if k_past is None:
        att, rw, wkv = _mixers_prompt(q, k, v, sink8, tok, rw_post_w, b, t, min(t, 2 * WKV_TILE))
        rows = min(WINDOW, t)
        last = lambda a: a.reshape(b, t, ATT_KV)[:, t - rows:].reshape(b, rows, ATT_KV_HEADS, HEAD_DIM)
        new_k, new_v = last(k), last(v)
        shift_out = tail[:, SUBLANES - 1, :RW_REAL]
    else:
        per_tile = max(m for m in (8, 4, 2, 1) if b % m == 0)
        att = _attn_sample(q, k, v, k_past.reshape(-1, ATT_KV), v_past.reshape(-1, ATT_KV),
                           sink8, b, t, per_tile)
        new_k = k.reshape(b, t, ATT_KV_HEADS, HEAD_DIM)
        new_v = v.reshape(b, t, ATT_KV_HEADS, HEAD_DIM)
        shift_out = tail.reshape(b, t, RW_COLS)[:, t - 1, :RW_REAL]
        rw, wkv = _rwkv(tok, wkv0, rw_post_w, b, t, per_tile * t, False)
    y = _out_ffn(x2d, att, rw, wo_att, wo_rw, g2, w_up, w_down, tm, 1024)
    return (y.reshape(b, t, d), new_k, new_v, wkv, shift_out)


def kernel(x_prompt, x_sample, cache_attn_k, cache_attn_v, state_rwkv_wkv, state_rwkv_shift, ln1_g, w_in, q_norm_g, k_norm_g, attn_sinks, shift_mu, decay_w0, decay_w2, iclr_a0, iclr_a2, gate_g2, k_k, k_a, r_k, lnx_g, lnx_b, w_out, ln2_g, w_up, w_down):
    bp, tp, d = x_prompt.shape
    bs, ts, _ = x_sample.shape
    depth = w_in.shape[0]
    perm = _pair_perm()
    tabs_p = _rope_tables(jnp.arange(tp))
    tabs_s = tuple(jnp.tile(t, (bs, 1)) for t in _rope_tables(PAST_LEN + jnp.arange(ts)))
    tm_p = min(512, bp * tp)
    tm_s = min(512, bs * ts)

    hp, hs = x_prompt, x_sample
    outs_p, outs_s = [], []
    for l in range(depth):
        wl = w_in[l]
        att_cols = ATT_Q + 2 * ATT_KV
        w_in_p = jnp.concatenate(
            [wl[:, :ATT_Q][:, perm], wl[:, ATT_Q:att_cols], _pad_rw(wl[:, att_cols:])], axis=1).astype(BF16)
        row2 = lambda t: t.reshape(1, -1)
        zeros64 = jnp.zeros((64, RWKV_W), F32)
        att_w = (row2(ln1_g[l]), w_in_p, row2(jnp.tile(q_norm_g[l], ATT_HEADS)),
                 row2(jnp.tile(k_norm_g[l], ATT_KV_HEADS)))
        rw_pre_w = (
            row2(_pad_rw(shift_mu[l])),
            row2(decay_w0[l]),
            jnp.concatenate([decay_w2[l], zeros64], axis=0).astype(BF16),
            row2(iclr_a0[l]),
            jnp.concatenate([zeros64, iclr_a2[l]], axis=0).astype(BF16),
            jnp.pad(gate_g2[l], ((0, GATE_PAD - gate_g2.shape[1]), (0, 0))).astype(BF16),
            row2(k_k[l]), row2(k_a[l]),
        )
        rw_post_w = (row2(r_k[l]), row2(lnx_g[l]), row2(lnx_b[l]))
        lw = (att_w, rw_pre_w, rw_post_w, attn_sinks[l],
              w_out[l][:ATT_Q][perm].astype(BF16), w_out[l][ATT_Q:].astype(BF16),
              row2(ln2_g[l]), w_up[l].astype(BF16), w_down[l].astype(BF16))
        zero_shift = jnp.zeros((bp, RW_REAL), F32)
        hp, *op = _layer(hp, tabs_p, None, None, zero_shift, None, lw, tm_p)
        hs, *os_ = _layer(hs, tabs_s, cache_attn_k[l], cache_attn_v[l], state_rwkv_shift[l],
                          state_rwkv_wkv[l], lw, tm_s)
        outs_p.append(op)
        outs_s.append(os_)
    stack = lambda outs, i: jnp.stack([o[i] for o in outs])
    return (hp, hs,
            stack(outs_p, 0), stack(outs_p, 1), stack(outs_p, 2), stack(outs_p, 3),
            stack(outs_s, 0), stack(outs_s, 1), stack(outs_s, 2), stack(outs_s, 3))
```

```python
import functools

import jax
import jax.numpy as jnp
import numpy as np
from jax import lax
from jax.experimental import pallas as pl
from jax.experimental.pallas import tpu as pltpu

F32 = jnp.float32
BF16 = jnp.bfloat16

CHUNK = 64
WINDOW = 128
HEAD_DIM = 64
ATT_HEADS = 8
ATT_KV_HEADS = 2
RWKV_HEADS = 8
RWKV_W = 512
ATT_Q = 512
ATT_KV = 128
LORA_WA = 128
GATE_PAD = 256
RW_COLS = 3 * RWKV_W + LORA_WA + GATE_PAD
RW_REAL = 3 * RWKV_W + 64 + 64 + 160
IN_COLS_PAD = ATT_Q + 2 * ATT_KV + RW_COLS
PAST_LEN = 4096
ROPE_THETA = 10000.0
ATT_SCALE = HEAD_DIM ** -0.5
NORM_EPS = 1e-6
GN_EPS = 64e-5
DECAY_SCALE = float(np.exp(-0.5))
LANES = 128
SUBLANES = 8
WKV_TILE = 256
VMEM_LIMIT = 52 * 1024 * 1024

NN = (((1,), (0,)), ((), ()))
NT = (((1,), (1,)), ((), ()))
BNN = (((2,), (1,)), ((0,), (0,)))
BNT = (((2,), (2,)), ((0,), (0,)))
BTN = (((1,), (1,)), ((0,), (0,)))


def _dot(a, b, dims=NN):
    return lax.dot_general(a.astype(BF16), b.astype(BF16), dims, preferred_element_type=F32)


def _split2(x):
    hi = x.astype(BF16)
    lo = (x - hi.astype(F32)).astype(BF16)
    return hi, lo


def _seg_sum(x, ones_bd, split=True):
    outs = []
    for j in range(x.shape[1] // LANES):
        xb = x[:, LANES * j:LANES * (j + 1)]
        if split:
            lhs, rhs = jnp.concatenate(_split2(xb), axis=1), ones_bd
        else:
            lhs, rhs = xb.astype(BF16), ones_bd[:LANES]
        outs.append(lax.dot_general(lhs, rhs, NN, preferred_element_type=F32))
    return outs[0] if len(outs) == 1 else jnp.concatenate(outs, axis=1)


def _ones_bd():
    r = lax.broadcasted_iota(jnp.int32, (2 * LANES, LANES), 0) & (LANES - 1)
    c = lax.broadcasted_iota(jnp.int32, (2 * LANES, LANES), 1)
    return jnp.where((r < HEAD_DIM) == (c < HEAD_DIM), 1.0, 0.0).astype(BF16)


def _lane_lo(shape):
    return lax.broadcasted_iota(jnp.int32, shape, len(shape) - 1) < HEAD_DIM


_resident = functools.partial(pl.BlockSpec, pipeline_mode=pl.Buffered(1))


def _inproj_kernel(x_ref, sprev_ref, g1_ref, w_ref, qg_ref, kg_ref, cos_ref, sa_ref, sb_ref,
                   mu_ref, w0_ref, w2_ref, a0_ref, a2_ref, g2_ref, kk_ref, ka_ref,
                   q_ref, k_ref, v_ref, r_ref, km_ref, vv_ref, lw_ref, a_ref, b_ref, gate_ref,
                   tail_ref, carry_scr, *, seq):
    tm = x_ref.shape[0]
    rows = lax.broadcasted_iota(jnp.int32, (tm, 1), 0)
    rw0 = ATT_Q + 2 * ATT_KV
    ones_bd = _ones_bd()
    x = x_ref[...]
    ms = jnp.mean(x * x, axis=-1, keepdims=True)
    h = (x * lax.rsqrt(ms + NORM_EPS) * g1_ref[...]).astype(BF16)

    def project(c0, c1):
        return lax.dot_general(h, w_ref[:, c0:c1], NN, preferred_element_type=F32)

    if seq >= tm:
        @pl.when(pl.program_id(0) % (seq // tm) == 0)
        def _():
            carry_scr[0:1, :] = sprev_ref[...]

    def shifted(c0, c1):
        p = project(rw0 + c0, rw0 + c1)
        prev = pltpu.roll(p, 1, 0)
        if seq >= tm:
            prev = jnp.where(rows == 0, carry_scr[0:1, c0:c1], prev)
            carry_scr[0:1, c0:c1] = p[tm - 1:tm, :]
            tail_ref[:, c0:c1] = p[tm - SUBLANES:tm, :]
        else:
            for s in range(tm // seq):
                prev = jnp.where(rows == s * seq, sprev_ref[s, :, c0:c1], prev)
            tail_ref[:, c0:c1] = p
        return p + (prev - p) * mu_ref[:, c0:c1]

    cos = cos_ref[...]
    sin_a = sa_ref[...]
    sin_b = sb_ref[...]

    def norm_rope(p, g):
        ss = _seg_sum(p * p, ones_bd, split=False)
        y = p * lax.rsqrt(ss * (1.0 / HEAD_DIM) + NORM_EPS) * g
        outs = []
        for j in range(p.shape[1] // LANES):
            yb = y[:, LANES * j:LANES * (j + 1)]
            outs.append(yb * cos + pltpu.roll(yb, LANES - 32, 1) * sin_a + pltpu.roll(yb, 32, 1) * sin_b)
        return outs[0] if len(outs) == 1 else jnp.concatenate(outs, axis=1)

    lora = shifted(3 * RWKV_W, RW_COLS)
    k = shifted(RWKV_W, 2 * RWKV_W)
    sigmoid = lambda z: 0.5 + 0.5 * jnp.tanh(0.5 * z)

    def token_maps(rs):
        wa = lora[rs, :LORA_WA]
        gd = lora[rs, LORA_WA:]
        kr = k[rs]
        zw = w0_ref[...] + _dot(jnp.tanh(wa), w2_ref[...])
        lw_ref[rs, :] = -DECAY_SCALE * sigmoid(zw)
        a_ic = sigmoid(a0_ref[...] + _dot(wa, a2_ref[...]))
        gate_ref[rs, :] = _dot(sigmoid(gd), g2_ref[...])
        kk = kr * kk_ref[...]
        kk = kk * lax.rsqrt(jnp.maximum(_seg_sum(kk * kk, ones_bd, split=False), 1e-24))
        km_ref[rs, :] = kr * (1.0 + (a_ic - 1.0) * ka_ref[...])
        a_ref[rs, :] = -kk
        b_ref[rs, :] = kk * a_ic

    r_ref[...] = shifted(0, RWKV_W)
    vv_ref[...] = shifted(2 * RWKV_W, 3 * RWKV_W)
    q_ref[...] = norm_rope(project(0, ATT_Q), qg_ref[...])
    pkv = project(ATT_Q, ATT_Q + 2 * ATT_KV)
    k_ref[...] = norm_rope(pkv[:, :ATT_KV], kg_ref[...])
    v_ref[...] = pkv[:, ATT_KV:]
    token_maps(slice(0, tm))


def _inproj(x2d, sprev, att_w, rw_w, tabs, tm, seq):
    n, d = x2d.shape
    g1, w_in_p, qg, kg = att_w
    cos_t, sin_a, sin_b = tabs
    tab_blocks = cos_t.shape[0] // tm
    row = lambda i: (i, 0)
    const = lambda i: (0, 0)
    tab = lambda i: (i % tab_blocks, 0)
    if seq >= tm:
        tps = seq // tm
        sprev_spec = pl.BlockSpec((None, 1, RW_COLS), lambda i: (i // tps, 0, 0))
        tail_spec = pl.BlockSpec((None, SUBLANES, RW_COLS), lambda i: (i // tps, 0, 0))
        tail_shape = jax.ShapeDtypeStruct((n // seq, SUBLANES, RW_COLS), F32)
    else:
        sprev_spec = pl.BlockSpec((tm // seq, 1, RW_COLS), lambda i: (i, 0, 0))
        tail_spec = pl.BlockSpec((tm, RW_COLS), row)
        tail_shape = jax.ShapeDtypeStruct((n, RW_COLS), F32)
    wide = pl.BlockSpec((tm, RWKV_W), row)
    wide_shape = jax.ShapeDtypeStruct((n, RWKV_W), F32)
    return pl.pallas_call(
        functools.partial(_inproj_kernel, seq=seq),
        grid=(n // tm,),
        in_specs=[
            pl.BlockSpec((tm, d), row),
            sprev_spec,
            _resident((1, d), const),
            _resident((d, IN_COLS_PAD), const),
            _resident((1, ATT_Q), const),
            _resident((1, ATT_KV), const),
            pl.BlockSpec((tm, LANES), tab),
            pl.BlockSpec((tm, LANES), tab),
            pl.BlockSpec((tm, LANES), tab),
        ] + [_resident(w.shape, const) for w in rw_w],
        out_specs=[
            pl.BlockSpec((tm, ATT_Q), row),
            pl.BlockSpec((tm, ATT_KV), row),
            pl.BlockSpec((tm, ATT_KV), row),
        ] + [wide] * 7 + [tail_spec],
        out_shape=[
            jax.ShapeDtypeStruct((n, ATT_Q), F32),
            jax.ShapeDtypeStruct((n, ATT_KV), F32),
            jax.ShapeDtypeStruct((n, ATT_KV), F32),
        ] + [wide_shape] * 7 + [tail_shape],
        scratch_shapes=[pltpu.VMEM((SUBLANES, RW_COLS), F32)],
        compiler_params=pltpu.CompilerParams(
            dimension_semantics=("arbitrary",), vmem_limit_bytes=VMEM_LIMIT),
        name="inproj",
    )(x2d, sprev, g1, w_in_p, qg, kg, cos_t, sin_a, sin_b, *rw_w)


def _attend(q, kc, vc, sink_tile, first_key_pos):
    tq = q.shape[0]
    nk = kc.shape[0]
    pad = 2 * LANES - nk
    lo = _lane_lo((tq, LANES))
    blocks = [q[:, LANES * j:LANES * (j + 1)] * ATT_SCALE for j in range(ATT_HEADS // 2)]
    stacked = jnp.concatenate([jnp.where(lo, b, 0.0) for b in blocks]
                              + [jnp.where(lo, 0.0, b) for b in blocks], axis=0).astype(BF16)
    zpad = jnp.zeros((pad, LANES), BF16)
    k_ext = jnp.concatenate([kc, zpad], axis=0)
    v_ones = jnp.concatenate([jnp.concatenate([vc, zpad], axis=0),
                              jnp.ones((2 * LANES, LANES), BF16)], axis=1)
    s = lax.dot_general(stacked, k_ext, NT, preferred_element_type=F32)
    lane = lax.broadcasted_iota(jnp.int32, (ATT_HEADS * tq, LANES), 1)
    s0 = s[:, :LANES]
    s1 = s[:, LANES:]
    if first_key_pos is not None:
        s0 = jnp.where(lane + first_key_pos >= 0, s0, -jnp.inf)
        s1 = jnp.where(lane + (first_key_pos + LANES) >= 0, s1, -jnp.inf)
    s1 = jnp.where(lane < nk - LANES, s1, sink_tile)
    m = jnp.max(jnp.maximum(s0, s1), axis=-1, keepdims=True)
    p = jnp.concatenate([jnp.exp(s0 - m), jnp.exp(s1 - m)], axis=1).astype(BF16)
    od = lax.dot_general(p, v_ones, NN, preferred_element_type=F32)
    o = od[:, :LANES] / od[:, LANES:]
    half = (ATT_HEADS // 2) * tq
    outs = [jnp.where(lo, o[tq * j:tq * (j + 1)], o[half + tq * j:half + tq * (j + 1)])
            for j in range(ATT_HEADS // 2)]
    return jnp.concatenate(outs, axis=1)


def _attn_prompt_kernel(q_ref, k_ref, v_ref, kh_ref, vh_ref, sink_ref, o_ref, *, tq):
    i = pl.program_id(1)
    kcat = jnp.concatenate([kh_ref[...], k_ref[...]], axis=0).astype(BF16)
    vcat = jnp.concatenate([vh_ref[...], v_ref[...]], axis=0).astype(BF16)
    sink_tile = sink_ref[...]
    span = WINDOW + CHUNK
    for c in range(tq // CHUNK):
        first_key_pos = i * tq + c * CHUNK - WINDOW if c < WINDOW // CHUNK else None
        o_ref[CHUNK * c:CHUNK * (c + 1), :] = _attend(
            q_ref[CHUNK * c:CHUNK * (c + 1), :], kcat[CHUNK * c:CHUNK * c + span],
            vcat[CHUNK * c:CHUNK * c + span], sink_tile, first_key_pos)


def _sink_tile(sinks, rows_per_head, nk):
    lane = jnp.arange(LANES)[None, :]
    col = jnp.repeat(sinks.astype(F32), rows_per_head)[:, None]
    return jnp.where(lane == nk - LANES, col, jnp.where(lane > nk - LANES, -jnp.inf, 0.0))


def _attn_sample_kernel(q_ref, k_ref, v_ref, kc_ref, vc_ref, sink_ref, o_ref, *, t):
    ns = q_ref.shape[0] // t
    rows = kc_ref.shape[0] // ns
    for s in range(ns):
        new = slice(t * s, t * (s + 1))
        old = slice(rows * s, rows * (s + 1))
        kall = jnp.concatenate([kc_ref[old, :], k_ref[new, :]], axis=0).astype(BF16)
        vall = jnp.concatenate([vc_ref[old, :], v_ref[new, :]], axis=0).astype(BF16)
        o_ref[new, :] = _attend(q_ref[new, :], kall, vall, sink_ref[...], None)


def _attn_sample(q, k, v, k_cache, v_cache, sinks, batch, t, per_step):
    rows = k_cache.shape[0] // batch
    sink_tile = _sink_tile(sinks, t, rows + t)
    row = lambda b: (b, 0)
    return pl.pallas_call(
        functools.partial(_attn_sample_kernel, t=t),
        grid=(batch // per_step,),
        in_specs=[
            pl.BlockSpec((per_step * t, ATT_Q), row),
            pl.BlockSpec((per_step * t, ATT_KV), row),
            pl.BlockSpec((per_step * t, ATT_KV), row),
            pl.BlockSpec((per_step * rows, ATT_KV), row),
            pl.BlockSpec((per_step * rows, ATT_KV), row),
            pl.BlockSpec((ATT_HEADS * t, LANES), lambda b: (0, 0)),
        ],
        out_specs=pl.BlockSpec((per_step * t, ATT_Q), row),
        out_shape=jax.ShapeDtypeStruct((batch * t, ATT_Q), F32),
        compiler_params=pltpu.CompilerParams(dimension_semantics=("parallel",)),
        name="attn_sample",
    )(q, k, v, k_cache, v_cache, sink_tile)


def _systems(x, rows):
    nc = x.shape[0] // rows
    return jnp.stack([x[rows * c:rows * (c + 1), LANES * j:LANES * (j + 1)]
                      for c in range(nc) for j in range(RWKV_HEADS // 2)])


def _bdot(a, b, dims):
    return lax.dot_general(a.astype(BF16), b.astype(BF16), dims, preferred_element_type=F32)


def _wkv_prepare(r, k, v, lw, a, b):
    c = CHUNK
    tt = r.shape[0]
    nc = tt // c
    np_ = RWKV_HEADS // 2
    ri = lax.broadcasted_iota(jnp.int32, (tt, tt), 0)
    ci = lax.broadcasted_iota(jnp.int32, (tt, tt), 1)
    tri = jnp.where((ri >= ci) & ((ri & -c) == (ci & -c)), 1.0, 0.0).astype(BF16)
    hi = lw.astype(BF16)
    rem = lw - hi.astype(F32)
    mid = rem.astype(BF16)
    low = (rem - mid.astype(F32)).astype(BF16)
    cum = (lax.dot_general(tri, hi, NN, preferred_element_type=F32)
           + lax.dot_general(tri, mid, NN, preferred_element_type=F32)
           + lax.dot_general(tri, low, NN, preferred_element_type=F32))
    e = jnp.exp(cum)
    e_inv = jnp.exp(-cum)
    e_x = jnp.exp(cum - lw)
    rt = r * e
    at = a * e_x
    kt = k * e_inv
    bt = b * e_inv

    lo = _lane_lo((1, 1, LANES))
    at_p, rt_p, kt_p, bt_p, v_p = (_systems(t, c) for t in (at, rt, kt, bt, v))
    ec_p = jnp.stack([e[c * i + c - 1:c * i + c, LANES * j:LANES * (j + 1)]
                      for i in range(nc) for j in range(np_)])
    kh_p = kt_p * ec_p
    bh_p = bt_p * ec_p
    at0 = jnp.where(lo, at_p, 0.0)
    at1 = jnp.where(lo, 0.0, at_p)
    rt0 = jnp.where(lo, rt_p, 0.0)
    rt1 = jnp.where(lo, 0.0, rt_p)
    lhs = jnp.concatenate([at0, at1, rt0, rt1], axis=1)
    rhs = jnp.concatenate([bt_p, kt_p], axis=1)
    g = _bdot(lhs, rhs, BNT)

    r128 = lax.broadcasted_iota(jnp.int32, (1, LANES, LANES), 1)
    l128 = lax.broadcasted_iota(jnp.int32, (1, LANES, LANES), 2)
    t_idx = r128 & (c - 1)
    s_idx = l128 & (c - 1)
    ga = jnp.where(s_idx < t_idx, g[:, :LANES], 0.0)
    gr = jnp.where(s_idx <= t_idx, g[:, LANES:], 0.0)
    same = (r128 < c) == (l128 < c)

    block_diag = _block_diag

    def head_rows(x):
        return jnp.concatenate([x[:, :c], x[:, c:]], axis=2)

    a_c = jnp.where(lo, ga[:, :c], jnp.stack(
        [pltpu.roll(ga[j, c:], c, 1) for j in range(ga.shape[0])]))
    r64 = lax.broadcasted_iota(jnp.int32, (1, c, LANES), 1)
    l64 = lax.broadcasted_iota(jnp.int32, (1, c, LANES), 2)
    t_c = jnp.where((l64 & (c - 1)) == r64, 1.0, 0.0) + a_c
    p_c = _bdot(a_c, block_diag(a_c), BNN)
    for level in range(5):
        last = level == 4
        lhs_tp = t_c if last else jnp.concatenate([t_c, p_c], axis=1)
        prod = _bdot(lhs_tp, block_diag(p_c), BNN)
        t_c = t_c + prod[:, :c]
        if not last:
            p_c = prod[:, c:]

    zeros = jnp.zeros_like(v_p)
    zv = jnp.concatenate([zeros, jnp.where(lo, v_p, 0.0), zeros, jnp.where(lo, 0.0, v_p)], axis=1)
    xak = _bdot(head_rows(ga), zv, BNN)
    z = jnp.concatenate([jnp.concatenate([at0, at1], axis=1), block_diag(xak)], axis=2)
    tzs = _bdot(t_c, z, BNN)
    q_f32 = jnp.concatenate([tzs, jnp.concatenate([zeros, v_p], axis=2)], axis=1)
    q_mat = q_f32.astype(BF16)
    gq = _bdot(gr, q_mat, BNN)
    r_eff = rt_p + jnp.where(lo, gq[:, :c, :LANES], gq[:, c:, :LANES])
    y0 = jnp.where(lo, gq[:, :c, LANES:], gq[:, c:, LANES:])
    bk = jnp.concatenate([bh_p, kh_p], axis=1)
    mn = _bdot(q_mat, bk, BTN)
    wb = jnp.where(same, mn[:, :LANES], 0.0)
    n_c = jnp.where(lo, mn[:, LANES:LANES + c], mn[:, LANES + c:])

    return r_eff, y0, wb, n_c, ec_p


def _block_diag(xc):
    lo = _lane_lo((1, 1, LANES))
    return jnp.concatenate([jnp.where(lo, xc, 0), jnp.where(lo, 0, xc)], axis=1)


def _wkv_apply(s_c, prep, chain):
    r_eff, y0, wb, n_c, ec_p = prep
    np_ = RWKV_HEADS // 2
    nc = r_eff.shape[0] // np_
    flat = lambda y_i: jnp.concatenate([y_i[j] for j in range(np_)], axis=1)
    if chain:
        ys = []
        for i in range(nc):
            sl = slice(np_ * i, np_ * (i + 1))
            ys.append(flat(_bdot(r_eff[sl], _block_diag(s_c), BNT) + y0[sl]))
            s_c = s_c * ec_p[sl] + _bdot(s_c, wb[sl], BNN) + n_c[sl]
    else:
        y_all = _bdot(r_eff, _block_diag(s_c), BNT) + y0
        ys = [flat(y_all[np_ * i:np_ * (i + 1)]) for i in range(nc)]
        s_c = s_c * ec_p + _bdot(s_c, wb, BNN) + n_c
    y = ys[0] if nc == 1 else jnp.concatenate(ys, axis=0)
    return y, s_c


def _heads_to_pairs(s8):
    return jnp.stack([jnp.concatenate([s8[2 * j], s8[2 * j + 1]], axis=1)
                      for j in range(RWKV_HEADS // 2)])


def _rwkv_kernel(*refs, chain, zero_state, t, companion=None):
    tok_refs, refs = refs[:7], refs[7:]
    if not zero_state:
        s0_ref, refs = refs[0], refs[1:]
    rk_ref, lng_ref, lnb_ref, o_ref, sout_ref, s_scr = refs
    if chain:
        r, k, v, lw, a, b, gate = (ref[...] for ref in tok_refs)
        @pl.when(pl.program_id(1) == 0)
        def _():
            s_scr[...] = jnp.zeros_like(s_scr) if zero_state else _heads_to_pairs(s0_ref[...])
        s_in = s_scr[...]
    else:
        ns = tok_refs[0].shape[0] // t
        zpad = jnp.zeros((CHUNK - t, RWKV_W), F32)
        r, k, v, lw, a, b, gate = (
            jnp.concatenate([piece for s in range(ns) for piece in (ref[t * s:t * (s + 1), :], zpad)], axis=0)
            for ref in tok_refs)
        s_in = jnp.concatenate([_heads_to_pairs(s0_ref[s]) for s in range(ns)], axis=0)
    np_ = RWKV_HEADS // 2
    ones_bd = _ones_bd()
    inv_n = 1.0 / HEAD_DIM

    def store_heads(dst, s_bd):
        for j in range(np_):
            dst[2 * j] = s_bd[j, :, :HEAD_DIM]
            dst[2 * j + 1] = s_bd[j, :, HEAD_DIM:]

    def finish(y, sl):
        mean = _seg_sum(y, ones_bd) * inv_n
        d = y - mean
        var = _seg_sum(d * d, ones_bd) * inv_n
        yn = d * lax.rsqrt(var + GN_EPS) * lng_ref[...] + lnb_ref[...]
        bonus = _seg_sum(r[sl] * k[sl] * rk_ref[...], ones_bd) * v[sl]
        return (yn + bonus) * gate[sl]

    rows = r.shape[0]
    sub = min(rows, WKV_TILE) if chain else rows
    slices = [slice(s0, s0 + sub) for s0 in range(0, rows, sub)]
    prepare = lambda sl: _wkv_prepare(r[sl], k[sl], v[sl], lw[sl], a[sl], b[sl])
    s_out = s_in
    outs = []
    prep = prepare(slices[0])
    for j, sl in enumerate(slices):
        nxt = prepare(slices[j + 1]) if j + 1 < len(slices) else None
        y, s_out = _wkv_apply(s_out, prep, chain)
        outs.append(finish(y, sl))
        prep = nxt
        if companion is not None and j == 0:
            companion()
    out = outs[0] if len(outs) == 1 else jnp.concatenate(outs, axis=0)

    if chain:
        s_scr[...] = s_out
        o_ref[...] = out

        @pl.when(pl.program_id(1) == pl.num_programs(1) - 1)
        def _():
            store_heads(sout_ref, s_scr[...])
    else:
        for s in range(ns):
            store_heads(sout_ref.at[s], s_out[np_ * s:np_ * (s + 1)])
            o_ref[t * s:t * (s + 1), :] = out[CHUNK * s:CHUNK * s + t, :]


def _rwkv(tok, s0, post_w, n_seq, seq, tile_rows, chain):
    if chain:
        groups, nt, ns = n_seq, seq // tile_rows, None
    else:
        groups, nt, ns = n_seq * seq // tile_rows, 1, tile_rows // seq
    row = lambda g, i: (g * nt + i, 0)
    const = lambda g, i: (0, 0)
    state_spec = pl.BlockSpec((ns, RWKV_HEADS, HEAD_DIM, HEAD_DIM), lambda g, i: (g, 0, 0, 0))
    wide = pl.BlockSpec((tile_rows, RWKV_W), row)
    state_in = [] if s0 is None else [s0]
    return pl.pallas_call(
        functools.partial(_rwkv_kernel, chain=chain, zero_state=s0 is None, t=seq),
        grid=(groups, nt),
        in_specs=[wide] * 7 + [state_spec] * len(state_in) + [pl.BlockSpec(w.shape, const) for w in post_w],
        out_specs=[wide, state_spec],
        out_shape=[
            jax.ShapeDtypeStruct((n_seq * seq, RWKV_W), F32),
            jax.ShapeDtypeStruct((n_seq, RWKV_HEADS, HEAD_DIM, HEAD_DIM), F32),
        ],
        scratch_shapes=[pltpu.VMEM((RWKV_HEADS // 2, HEAD_DIM, LANES), F32)],
        compiler_params=pltpu.CompilerParams(
            dimension_semantics=("parallel", "arbitrary"), vmem_limit_bytes=VMEM_LIMIT),
        name="rwkv",
    )(*tok, *state_in, *post_w)


def _mixers_prompt_kernel(q_ref, k_ref, v_ref, kh_ref, vh_ref, sink_ref, *refs, tq):
    *rwkv_refs, att_ref, rw_ref, sout_ref, s_scr = refs
    attention = functools.partial(_attn_prompt_kernel, q_ref, k_ref, v_ref, kh_ref, vh_ref, sink_ref,
                                  att_ref, tq=tq)
    _rwkv_kernel(*rwkv_refs, rw_ref, sout_ref, s_scr, chain=True, zero_state=True, t=tq,
                 companion=attention)


def _mixers_prompt(q, k, v, sinks, tok, post_w, batch, seq, tq):
    sink_tile = _sink_tile(sinks, CHUNK, WINDOW + CHUNK)
    nt = seq // tq
    row = lambda b, i: (b * nt + i, 0)
    const = lambda b, i: (0, 0)
    halo = lambda b, i: (jnp.maximum((b * nt + i) * (tq // WINDOW) - 1, 0), 0)
    wide = pl.BlockSpec((tq, RWKV_W), row)
    state_spec = pl.BlockSpec((None, RWKV_HEADS, HEAD_DIM, HEAD_DIM), lambda b, i: (b, 0, 0, 0))
    return pl.pallas_call(
        functools.partial(_mixers_prompt_kernel, tq=tq),
        grid=(batch, nt),
        in_specs=[
            pl.BlockSpec((tq, ATT_Q), row),
            pl.BlockSpec((tq, ATT_KV), row),
            pl.BlockSpec((tq, ATT_KV), row),
            pl.BlockSpec((WINDOW, ATT_KV), halo),
            pl.BlockSpec((WINDOW, ATT_KV), halo),
            pl.BlockSpec((ATT_HEADS * CHUNK, LANES), const),
        ] + [wide] * 7 + [pl.BlockSpec(w.shape, const) for w in post_w],
        out_specs=[pl.BlockSpec((tq, ATT_Q), row), wide, state_spec],
        out_shape=[
            jax.ShapeDtypeStruct((batch * seq, ATT_Q), F32),
            jax.ShapeDtypeStruct((batch * seq, RWKV_W), F32),
            jax.ShapeDtypeStruct((batch, RWKV_HEADS, HEAD_DIM, HEAD_DIM), F32),
        ],
        scratch_shapes=[pltpu.VMEM((RWKV_HEADS // 2, HEAD_DIM, LANES), F32)],
        compiler_params=pltpu.CompilerParams(
            dimension_semantics=("parallel", "arbitrary"), vmem_limit_bytes=VMEM_LIMIT),
        name="mixers",
    )(q, k, v, k, v, sink_tile, *tok, *post_w)


def _ffn_kernel(x_ref, att_ref, rw_ref, woa_ref, wor_ref, g2_ref, wup_ref, wdn_ref, o_ref, *, tf):
    x2 = x_ref[...] + _dot(att_ref[...], woa_ref[...]) + _dot(rw_ref[...], wor_ref[...])
    h = (x2 * g2_ref[...]).astype(BF16)
    us = []
    for j in range(wup_ref.shape[1] // tf):
        u = jnp.maximum(lax.dot_general(h, wup_ref[:, tf * j:tf * (j + 1)], NN,
                                        preferred_element_type=F32), 0.0)
        us.append((u * u).astype(BF16))
    u_all = jnp.concatenate(us, axis=1)
    inv_ms = 1.0 / (jnp.mean(x2 * x2, axis=-1, keepdims=True) + NORM_EPS)
    o_ref[...] = x2 + inv_ms * lax.dot_general(u_all, wdn_ref[...], NN, preferred_element_type=F32)


def _out_ffn(x2d, att, rw, wo_att, wo_rw, g2, w_up, w_down, tm, tf):
    n, d = x2d.shape
    dff = w_up.shape[1]
    row = lambda i: (i, 0)
    const = lambda i: (0, 0)
    return pl.pallas_call(
        functools.partial(_ffn_kernel, tf=tf),
        grid=(n // tm,),
        in_specs=[
            pl.BlockSpec((tm, d), row),
            pl.BlockSpec((tm, ATT_Q), row),
            pl.BlockSpec((tm, RWKV_W), row),
            _resident((ATT_Q, d), const),
            _resident((RWKV_W, d), const),
            _resident((1, d), const),
            _resident((d, dff), const),
            _resident((dff, d), const),
        ],
        out_specs=pl.BlockSpec((tm, d), row),
        out_shape=jax.ShapeDtypeStruct((n, d), F32),
        compiler_params=pltpu.CompilerParams(
            dimension_semantics=("parallel",), vmem_limit_bytes=VMEM_LIMIT),
        name="out_ffn",
    )(x2d, att, rw, wo_att, wo_rw, g2, w_up, w_down)


def _rope_tables(pos):
    half = HEAD_DIM // 2
    inv = ROPE_THETA ** (-jnp.arange(half, dtype=F32) / half)
    ang = pos.astype(F32)[:, None] * inv[None, :]
    cos = jnp.cos(ang)
    sin = jnp.sin(ang)
    zero = jnp.zeros_like(sin)
    tile = lambda t: jnp.concatenate([t, t], axis=1)
    return (tile(jnp.concatenate([cos, cos], axis=1)),
            tile(jnp.concatenate([-sin, zero], axis=1)),
            tile(jnp.concatenate([zero, sin], axis=1)))


def _pair_perm():
    idx = []
    for j in range(ATT_HEADS // 2):
        idx += list(range(HEAD_DIM * j, HEAD_DIM * (j + 1)))
        idx += list(range(HEAD_DIM * (j + 4), HEAD_DIM * (j + 5)))
    return np.asarray(idx, dtype=np.int32)


def _pad_rw(t):
    return jnp.pad(t, [(0, 0)] * (t.ndim - 1) + [(0, RW_COLS - RW_REAL)])


def _layer(x, tabs, k_past, v_past, shift_prev, wkv0, lw, tm):
    (att_w, rw_pre_w, rw_post_w, sink8, wo_att, wo_rw, g2, w_up, w_down) = lw
    b, t, d = x.shape
    n = b * t
    x2d = x.reshape(n, d)
    q, k, v, *tok, tail = _inproj(x2d, _pad_rw(shift_prev)[:, None, :], att_w, rw_pre_w, tabs, tm, t)

    if k_past is None:
        att, rw, wkv = _mixers_prompt(q, k, v, sink8, tok, rw_post_w, b, t, min(t, 2 * WKV_TILE))
        rows = min(WINDOW, t)
        last = lambda a: a.reshape(b, t, ATT_KV)[:, t - rows:].reshape(b, rows, ATT_KV_HEADS, HEAD_DIM)
        new_k, new_v = last(k), last(v)
        shift_out = tail[:, SUBLANES - 1, :RW_REAL]
    else:
        per_tile = max(m for m in (8, 4, 2, 1) if b % m == 0)
        att = _attn_sample(q, k, v, k_past.reshape(-1, ATT_KV), v_past.reshape(-1, ATT_KV),
                           sink8, b, t, per_tile)
        new_k = k.reshape(b, t, ATT_KV_HEADS, HEAD_DIM)
        new_v = v.reshape(b, t, ATT_KV_HEADS, HEAD_DIM)
        shift_out = tail.reshape(b, t, RW_COLS)[:, t - 1, :RW_REAL]
        rw, wkv = _rwkv(tok, wkv0, rw_post_w, b, t, per_tile * t, False)
    y = _out_ffn(x2d, att, rw, wo_att, wo_rw, g2, w_up, w_down, tm, 1024)
    return (y.reshape(b, t, d), new_k, new_v, wkv, shift_out)


def kernel(x_prompt, x_sample, cache_attn_k, cache_attn_v, state_rwkv_wkv, state_rwkv_shift, ln1_g, w_in, q_norm_g, k_norm_g, attn_sinks, shift_mu, decay_w0, decay_w2, iclr_a0, iclr_a2, gate_g2, k_k, k_a, r_k, lnx_g, lnx_b, w_out, ln2_g, w_up, w_down):
    bp, tp, d = x_prompt.shape
    bs, ts, _ = x_sample.shape
    depth = w_in.shape[0]
    perm = _pair_perm()
    tabs_p = _rope_tables(jnp.arange(tp))
    tabs_s = tuple(jnp.tile(t, (bs, 1)) for t in _rope_tables(PAST_LEN + jnp.arange(ts)))
    tm_p = min(512, bp * tp)
    tm_s = min(512, bs * ts)

    hp, hs = x_prompt, x_sample
    outs_p, outs_s = [], []
    for l in range(depth):
        wl = w_in[l]
        att_cols = ATT_Q + 2 * ATT_KV
        w_in_p = jnp.concatenate(
            [wl[:, :ATT_Q][:, perm], wl[:, ATT_Q:att_cols], _pad_rw(wl[:, att_cols:])], axis=1).astype(BF16)
        row2 = lambda t: t.reshape(1, -1)
        zeros64 = jnp.zeros((64, RWKV_W), F32)
        att_w = (row2(ln1_g[l]), w_in_p, row2(jnp.tile(q_norm_g[l], ATT_HEADS)),
                 row2(jnp.tile(k_norm_g[l], ATT_KV_HEADS)))
        rw_pre_w = (
            row2(_pad_rw(shift_mu[l])),
            row2(decay_w0[l]),
            jnp.concatenate([decay_w2[l], zeros64], axis=0).astype(BF16),
            row2(iclr_a0[l]),
            jnp.concatenate([zeros64, iclr_a2[l]], axis=0).astype(BF16),
            jnp.pad(gate_g2[l], ((0, GATE_PAD - gate_g2.shape[1]), (0, 0))).astype(BF16),
            row2(k_k[l]), row2(k_a[l]),
        )
        rw_post_w = (row2(r_k[l]), row2(lnx_g[l]), row2(lnx_b[l]))
        lw = (att_w, rw_pre_w, rw_post_w, attn_sinks[l],
              w_out[l][:ATT_Q][perm].astype(BF16), w_out[l][ATT_Q:].astype(BF16),
              row2(ln2_g[l]), w_up[l].astype(BF16), w_down[l].astype(BF16))
        zero_shift = jnp.zeros((bp, RW_REAL), F32)
        hp, *op = _layer(hp, tabs_p, None, None, zero_shift, None, lw, tm_p)
        hs, *os_ = _layer(hs, tabs_s, cache_attn_k[l], cache_attn_v[l], state_rwkv_shift[l],
                          state_rwkv_wkv[l], lw, tm_s)
        outs_p.append(op)
        outs_s.append(os_)
    stack = lambda outs, i: jnp.stack([o[i] for o in outs])
    return (hp, hs,
            stack(outs_p, 0), stack(outs_p, 1), stack(outs_p, 2), stack(outs_p, 3),
            stack(outs_s, 0), stack(outs_s, 1), stack(outs_s, 2), stack(outs_s, 3))
```

```python
import functools

import jax
import jax.numpy as jnp
import numpy as np
from jax import lax
from jax.experimental import pallas as pl
from jax.experimental.pallas import tpu as pltpu

F32 = jnp.float32
BF16 = jnp.bfloat16

CHUNK = 64
WINDOW = 128
HEAD_DIM = 64
ATT_HEADS = 8
ATT_KV_HEADS = 2
RWKV_HEADS = 8
RWKV_W = 512
ATT_Q = 512
ATT_KV = 128
LORA_WA = 128
GATE_PAD = 256
RW_COLS = 3 * RWKV_W + LORA_WA + GATE_PAD
RW_REAL = 3 * RWKV_W + 64 + 64 + 160
IN_COLS_PAD = ATT_Q + 2 * ATT_KV + RW_COLS
PAST_LEN = 4096
ROPE_THETA = 10000.0
ATT_SCALE = HEAD_DIM ** -0.5
NORM_EPS = 1e-6
GN_EPS = 64e-5
DECAY_SCALE = float(np.exp(-0.5))
LANES = 128
SUBLANES = 8
WKV_TILE = 256
VMEM_LIMIT = 52 * 1024 * 1024

NN = (((1,), (0,)), ((), ()))
NT = (((1,), (1,)), ((), ()))
BNN = (((2,), (1,)), ((0,), (0,)))
BNT = (((2,), (2,)), ((0,), (0,)))
BTN = (((1,), (1,)), ((0,), (0,)))


def _dot(a, b, dims=NN):
    return lax.dot_general(a.astype(BF16), b.astype(BF16), dims, preferred_element_type=F32)


def _split2(x):
    hi = x.astype(BF16)
    lo = (x - hi.astype(F32)).astype(BF16)
    return hi, lo


def _seg_sum(x, ones_bd, split=True):
    outs = []
    for j in range(x.shape[1] // LANES):
        xb = x[:, LANES * j:LANES * (j + 1)]
        if split:
            lhs, rhs = jnp.concatenate(_split2(xb), axis=1), ones_bd
        else:
            lhs, rhs = xb.astype(BF16), ones_bd[:LANES]
        outs.append(lax.dot_general(lhs, rhs, NN, preferred_element_type=F32))
    return outs[0] if len(outs) == 1 else jnp.concatenate(outs, axis=1)


def _ones_bd():
    r = lax.broadcasted_iota(jnp.int32, (2 * LANES, LANES), 0) & (LANES - 1)
    c = lax.broadcasted_iota(jnp.int32, (2 * LANES, LANES), 1)
    return jnp.where((r < HEAD_DIM) == (c < HEAD_DIM), 1.0, 0.0).astype(BF16)


def _lane_lo(shape):
    return lax.broadcasted_iota(jnp.int32, shape, len(shape) - 1) < HEAD_DIM


_resident = functools.partial(pl.BlockSpec, pipeline_mode=pl.Buffered(1))


def _inproj_kernel(x_ref, sprev_ref, g1_ref, w_ref, qg_ref, kg_ref, cos_ref, sa_ref, sb_ref,
                   mu_ref, w0_ref, w2_ref, a0_ref, a2_ref, g2_ref, kk_ref, ka_ref,
                   q_ref, k_ref, v_ref, r_ref, km_ref, vv_ref, lw_ref, a_ref, b_ref, gate_ref,
                   tail_ref, carry_scr, *, seq):
    tm = x_ref.shape[0]
    rows = lax.broadcasted_iota(jnp.int32, (tm, 1), 0)
    rw0 = ATT_Q + 2 * ATT_KV
    ones_bd = _ones_bd()
    x = x_ref[...]
    ms = jnp.mean(x * x, axis=-1, keepdims=True)
    h = (x * lax.rsqrt(ms + NORM_EPS) * g1_ref[...]).astype(BF16)

    def project(c0, c1):
        return lax.dot_general(h, w_ref[:, c0:c1], NN, preferred_element_type=F32)

    if seq >= tm:
        @pl.when(pl.program_id(0) % (seq // tm) == 0)
        def _():
            carry_scr[0:1, :] = sprev_ref[...]

    def shifted(c0, c1):
        p = project(rw0 + c0, rw0 + c1)
        prev = pltpu.roll(p, 1, 0)
        if seq >= tm:
            prev = jnp.where(rows == 0, carry_scr[0:1, c0:c1], prev)
            carry_scr[0:1, c0:c1] = p[tm - 1:tm, :]
            tail_ref[:, c0:c1] = p[tm - SUBLANES:tm, :]
        else:
            for s in range(tm // seq):
                prev = jnp.where(rows == s * seq, sprev_ref[s, :, c0:c1], prev)
            tail_ref[:, c0:c1] = p
        return p + (prev - p) * mu_ref[:, c0:c1]

    cos = cos_ref[...]
    sin_a = sa_ref[...]
    sin_b = sb_ref[...]

    def norm_rope(p, g):
        ss = _seg_sum(p * p, ones_bd, split=False)
        y = p * lax.rsqrt(ss * (1.0 / HEAD_DIM) + NORM_EPS) * g
        outs = []
        for j in range(p.shape[1] // LANES):
            yb = y[:, LANES * j:LANES * (j + 1)]
            outs.append(yb * cos + pltpu.roll(yb, LANES - 32, 1) * sin_a + pltpu.roll(yb, 32, 1) * sin_b)
        return outs[0] if len(outs) == 1 else jnp.concatenate(outs, axis=1)

    lora = shifted(3 * RWKV_W, RW_COLS)
    k = shifted(RWKV_W, 2 * RWKV_W)
    sigmoid = lambda z: 0.5 + 0.5 * jnp.tanh(0.5 * z)

    def token_maps(rs):
        wa = lora[rs, :LORA_WA]
        gd = lora[rs, LORA_WA:]
        kr = k[rs]
        zw = w0_ref[...] + _dot(jnp.tanh(wa), w2_ref[...])
        lw_ref[rs, :] = -DECAY_SCALE * sigmoid(zw)
        a_ic = sigmoid(a0_ref[...] + _dot(wa, a2_ref[...]))
        gate_ref[rs, :] = _dot(sigmoid(gd), g2_ref[...])
        kk = kr * kk_ref[...]
        kk = kk * lax.rsqrt(jnp.maximum(_seg_sum(kk * kk, ones_bd, split=False), 1e-24))
        km_ref[rs, :] = kr * (1.0 + (a_ic - 1.0) * ka_ref[...])
        a_ref[rs, :] = -kk
        b_ref[rs, :] = kk * a_ic

    token_maps(slice(0, tm))
    q_ref[...] = norm_rope(project(0, ATT_Q), qg_ref[...])
    pkv = project(ATT_Q, ATT_Q + 2 * ATT_KV)
    k_ref[...] = norm_rope(pkv[:, :ATT_KV], kg_ref[...])
    v_ref[...] = pkv[:, ATT_KV:]
    r_ref[...] = shifted(0, RWKV_W)
    vv_ref[...] = shifted(2 * RWKV_W, 3 * RWKV_W)


def _inproj(x2d, sprev, att_w, rw_w, tabs, tm, seq):
    n, d = x2d.shape
    g1, w_in_p, qg, kg = att_w
    cos_t, sin_a, sin_b = tabs
    tab_blocks = cos_t.shape[0] // tm
    row = lambda i: (i, 0)
    const = lambda i: (0, 0)
    tab = lambda i: (i % tab_blocks, 0)
    if seq >= tm:
        tps = seq // tm
        sprev_spec = pl.BlockSpec((None, 1, RW_COLS), lambda i: (i // tps, 0, 0))
        tail_spec = pl.BlockSpec((None, SUBLANES, RW_COLS), lambda i: (i // tps, 0, 0))
        tail_shape = jax.ShapeDtypeStruct((n // seq, SUBLANES, RW_COLS), F32)
    else:
        sprev_spec = pl.BlockSpec((tm // seq, 1, RW_COLS), lambda i: (i, 0, 0))
        tail_spec = pl.BlockSpec((tm, RW_COLS), row)
        tail_shape = jax.ShapeDtypeStruct((n, RW_COLS), F32)
    wide = pl.BlockSpec((tm, RWKV_W), row)
    wide_shape = jax.ShapeDtypeStruct((n, RWKV_W), F32)
    return pl.pallas_call(
        functools.partial(_inproj_kernel, seq=seq),
        grid=(n // tm,),
        in_specs=[
            pl.BlockSpec((tm, d), row),
            sprev_spec,
            _resident((1, d), const),
            _resident((d, IN_COLS_PAD), const),
            _resident((1, ATT_Q), const),
            _resident((1, ATT_KV), const),
            pl.BlockSpec((tm, LANES), tab),
            pl.BlockSpec((tm, LANES), tab),
            pl.BlockSpec((tm, LANES), tab),
        ] + [_resident(w.shape, const) for w in rw_w],
        out_specs=[
            pl.BlockSpec((tm, ATT_Q), row),
            pl.BlockSpec((tm, ATT_KV), row),
            pl.BlockSpec((tm, ATT_KV), row),
        ] + [wide] * 7 + [tail_spec],
        out_shape=[
            jax.ShapeDtypeStruct((n, ATT_Q), F32),
            jax.ShapeDtypeStruct((n, ATT_KV), F32),
            jax.ShapeDtypeStruct((n, ATT_KV), F32),
        ] + [wide_shape] * 7 + [tail_shape],
        scratch_shapes=[pltpu.VMEM((SUBLANES, RW_COLS), F32)],
        compiler_params=pltpu.CompilerParams(
            dimension_semantics=("arbitrary",), vmem_limit_bytes=VMEM_LIMIT),
        name="inproj",
    )(x2d, sprev, g1, w_in_p, qg, kg, cos_t, sin_a, sin_b, *rw_w)


def _attend(q, kc, vc, sink_tile, first_key_pos):
    tq = q.shape[0]
    nk = kc.shape[0]
    pad = 2 * LANES - nk
    lo = _lane_lo((tq, LANES))
    blocks = [q[:, LANES * j:LANES * (j + 1)] * ATT_SCALE for j in range(ATT_HEADS // 2)]
    stacked = jnp.concatenate([jnp.where(lo, b, 0.0) for b in blocks]
                              + [jnp.where(lo, 0.0, b) for b in blocks], axis=0).astype(BF16)
    zpad = jnp.zeros((pad, LANES), BF16)
    k_ext = jnp.concatenate([kc, zpad], axis=0)
    v_ones = jnp.concatenate([jnp.concatenate([vc, zpad], axis=0),
                              jnp.ones((2 * LANES, LANES), BF16)], axis=1)
    s = lax.dot_general(stacked, k_ext, NT, preferred_element_type=F32)
    lane = lax.broadcasted_iota(jnp.int32, (ATT_HEADS * tq, LANES), 1)
    s0 = s[:, :LANES]
    s1 = s[:, LANES:]
    if first_key_pos is not None:
        s0 = jnp.where(lane + first_key_pos >= 0, s0, -jnp.inf)
        s1 = jnp.where(lane + (first_key_pos + LANES) >= 0, s1, -jnp.inf)
    s1 = jnp.where(lane < nk - LANES, s1, sink_tile)
    m = jnp.max(jnp.maximum(s0, s1), axis=-1, keepdims=True)
    p = jnp.concatenate([jnp.exp(s0 - m), jnp.exp(s1 - m)], axis=1).astype(BF16)
    od = lax.dot_general(p, v_ones, NN, preferred_element_type=F32)
    o = od[:, :LANES] / od[:, LANES:]
    half = (ATT_HEADS // 2) * tq
    outs = [jnp.where(lo, o[tq * j:tq * (j + 1)], o[half + tq * j:half + tq * (j + 1)])
            for j in range(ATT_HEADS // 2)]
    return jnp.concatenate(outs, axis=1)


def _attn_prompt_kernel(q_ref, k_ref, v_ref, kh_ref, vh_ref, sink_ref, o_ref, *, tq):
    i = pl.program_id(1)
    kcat = jnp.concatenate([kh_ref[...], k_ref[...]], axis=0).astype(BF16)
    vcat = jnp.concatenate([vh_ref[...], v_ref[...]], axis=0).astype(BF16)
    sink_tile = sink_ref[...]
    span = WINDOW + CHUNK
    for c in range(tq // CHUNK):
        first_key_pos = i * tq + c * CHUNK - WINDOW if c < WINDOW // CHUNK else None
        o_ref[CHUNK * c:CHUNK * (c + 1), :] = _attend(
            q_ref[CHUNK * c:CHUNK * (c + 1), :], kcat[CHUNK * c:CHUNK * c + span],
            vcat[CHUNK * c:CHUNK * c + span], sink_tile, first_key_pos)


def _sink_tile(sinks, rows_per_head, nk):
    lane = jnp.arange(LANES)[None, :]
    col = jnp.repeat(sinks.astype(F32), rows_per_head)[:, None]
    return jnp.where(lane == nk - LANES, col, jnp.where(lane > nk - LANES, -jnp.inf, 0.0))


def _attn_sample_kernel(q_ref, k_ref, v_ref, kc_ref, vc_ref, sink_ref, o_ref, *, t):
    ns = q_ref.shape[0] // t
    rows = kc_ref.shape[0] // ns
    for s in range(ns):
        new = slice(t * s, t * (s + 1))
        old = slice(rows * s, rows * (s + 1))
        kall = jnp.concatenate([kc_ref[old, :], k_ref[new, :]], axis=0).astype(BF16)
        vall = jnp.concatenate([vc_ref[old, :], v_ref[new, :]], axis=0).astype(BF16)
        o_ref[new, :] = _attend(q_ref[new, :], kall, vall, sink_ref[...], None)


def _attn_sample(q, k, v, k_cache, v_cache, sinks, batch, t, per_step):
    rows = k_cache.shape[0] // batch
    sink_tile = _sink_tile(sinks, t, rows + t)
    row = lambda b: (b, 0)
    return pl.pallas_call(
        functools.partial(_attn_sample_kernel, t=t),
        grid=(batch // per_step,),
        in_specs=[
            pl.BlockSpec((per_step * t, ATT_Q), row),
            pl.BlockSpec((per_step * t, ATT_KV), row),
            pl.BlockSpec((per_step * t, ATT_KV), row),
            pl.BlockSpec((per_step * rows, ATT_KV), row),
            pl.BlockSpec((per_step * rows, ATT_KV), row),
            pl.BlockSpec((ATT_HEADS * t, LANES), lambda b: (0, 0)),
        ],
        out_specs=pl.BlockSpec((per_step * t, ATT_Q), row),
        out_shape=jax.ShapeDtypeStruct((batch * t, ATT_Q), F32),
        compiler_params=pltpu.CompilerParams(dimension_semantics=("parallel",)),
        name="attn_sample",
    )(q, k, v, k_cache, v_cache, sink_tile)


def _systems(x, rows):
    nc = x.shape[0] // rows
    return jnp.stack([x[rows * c:rows * (c + 1), LANES * j:LANES * (j + 1)]
                      for c in range(nc) for j in range(RWKV_HEADS // 2)])


def _bdot(a, b, dims):
    return lax.dot_general(a.astype(BF16), b.astype(BF16), dims, preferred_element_type=F32)


def _wkv_prepare(r, k, v, lw, a, b):
    c = CHUNK
    tt = r.shape[0]
    nc = tt // c
    np_ = RWKV_HEADS // 2
    ri = lax.broadcasted_iota(jnp.int32, (tt, tt), 0)
    ci = lax.broadcasted_iota(jnp.int32, (tt, tt), 1)
    tri = jnp.where((ri >= ci) & ((ri & -c) == (ci & -c)), 1.0, 0.0).astype(BF16)
    hi = lw.astype(BF16)
    rem = lw - hi.astype(F32)
    mid = rem.astype(BF16)
    low = (rem - mid.astype(F32)).astype(BF16)
    cum = (lax.dot_general(tri, hi, NN, preferred_element_type=F32)
           + lax.dot_general(tri, mid, NN, preferred_element_type=F32)
           + lax.dot_general(tri, low, NN, preferred_element_type=F32))
    e = jnp.exp(cum)
    e_inv = jnp.exp(-cum)
    e_x = jnp.exp(cum - lw)
    rt = r * e
    at = a * e_x
    kt = k * e_inv
    bt = b * e_inv

    lo = _lane_lo((1, 1, LANES))
    at_p, rt_p, kt_p, bt_p, v_p = (_systems(t, c) for t in (at, rt, kt, bt, v))
    ec_p = jnp.stack([e[c * i + c - 1:c * i + c, LANES * j:LANES * (j + 1)]
                      for i in range(nc) for j in range(np_)])
    kh_p = kt_p * ec_p
    bh_p = bt_p * ec_p
    at0 = jnp.where(lo, at_p, 0.0)
    at1 = jnp.where(lo, 0.0, at_p)
    rt0 = jnp.where(lo, rt_p, 0.0)
    rt1 = jnp.where(lo, 0.0, rt_p)
    lhs = jnp.concatenate([at0, at1, rt0, rt1], axis=1)
    rhs = jnp.concatenate([bt_p, kt_p], axis=1)
    g = _bdot(lhs, rhs, BNT)

    r128 = lax.broadcasted_iota(jnp.int32, (1, LANES, LANES), 1)
    l128 = lax.broadcasted_iota(jnp.int32, (1, LANES, LANES), 2)
    t_idx = r128 & (c - 1)
    s_idx = l128 & (c - 1)
    ga = jnp.where(s_idx < t_idx, g[:, :LANES], 0.0)
    gr = jnp.where(s_idx <= t_idx, g[:, LANES:], 0.0)
    same = (r128 < c) == (l128 < c)

    block_diag = _block_diag

    def head_rows(x):
        return jnp.concatenate([x[:, :c], x[:, c:]], axis=2)

    a_c = jnp.where(lo, ga[:, :c], jnp.stack(
        [pltpu.roll(ga[j, c:], c, 1) for j in range(ga.shape[0])]))
    r64 = lax.broadcasted_iota(jnp.int32, (1, c, LANES), 1)
    l64 = lax.broadcasted_iota(jnp.int32, (1, c, LANES), 2)
    t_c = jnp.where((l64 & (c - 1)) == r64, 1.0, 0.0) + a_c
    p_c = _bdot(a_c, block_diag(a_c), BNN)
    for level in range(5):
        last = level == 4
        lhs_tp = t_c if last else jnp.concatenate([t_c, p_c], axis=1)
        prod = _bdot(lhs_tp, block_diag(p_c), BNN)
        t_c = t_c + prod[:, :c]
        if not last:
            p_c = prod[:, c:]

    zeros = jnp.zeros_like(v_p)
    zv = jnp.concatenate([zeros, jnp.where(lo, v_p, 0.0), zeros, jnp.where(lo, 0.0, v_p)], axis=1)
    xak = _bdot(head_rows(ga), zv, BNN)
    z = jnp.concatenate([jnp.concatenate([at0, at1], axis=1), block_diag(xak)], axis=2)
    tzs = _bdot(t_c, z, BNN)
    q_f32 = jnp.concatenate([tzs, jnp.concatenate([zeros, v_p], axis=2)], axis=1)
    q_mat = q_f32.astype(BF16)
    gq = _bdot(gr, q_mat, BNN)
    r_eff = rt_p + jnp.where(lo, gq[:, :c, :LANES], gq[:, c:, :LANES])
    y0 = jnp.where(lo, gq[:, :c, LANES:], gq[:, c:, LANES:])
    bk = jnp.concatenate([bh_p, kh_p], axis=1)
    mn = _bdot(q_mat, bk, BTN)
    wb = jnp.where(same, mn[:, :LANES], 0.0)
    n_c = jnp.where(lo, mn[:, LANES:LANES + c], mn[:, LANES + c:])

    return r_eff, y0, wb, n_c, ec_p


def _block_diag(xc):
    lo = _lane_lo((1, 1, LANES))
    return jnp.concatenate([jnp.where(lo, xc, 0), jnp.where(lo, 0, xc)], axis=1)


def _wkv_apply(s_c, prep, chain):
    r_eff, y0, wb, n_c, ec_p = prep
    np_ = RWKV_HEADS // 2
    nc = r_eff.shape[0] // np_
    flat = lambda y_i: jnp.concatenate([y_i[j] for j in range(np_)], axis=1)
    if chain:
        ys = []
        for i in range(nc):
            sl = slice(np_ * i, np_ * (i + 1))
            ys.append(flat(_bdot(r_eff[sl], _block_diag(s_c), BNT) + y0[sl]))
            s_c = s_c * ec_p[sl] + _bdot(s_c, wb[sl], BNN) + n_c[sl]
    else:
        y_all = _bdot(r_eff, _block_diag(s_c), BNT) + y0
        ys = [flat(y_all[np_ * i:np_ * (i + 1)]) for i in range(nc)]
        s_c = s_c * ec_p + _bdot(s_c, wb, BNN) + n_c
    y = ys[0] if nc == 1 else jnp.concatenate(ys, axis=0)
    return y, s_c


def _heads_to_pairs(s8):
    return jnp.stack([jnp.concatenate([s8[2 * j], s8[2 * j + 1]], axis=1)
                      for j in range(RWKV_HEADS // 2)])


def _rwkv_kernel(*refs, chain, zero_state, t, companion=None):
    tok_refs, refs = refs[:7], refs[7:]
    if not zero_state:
        s0_ref, refs = refs[0], refs[1:]
    rk_ref, lng_ref, lnb_ref, o_ref, sout_ref, s_scr = refs
    if chain:
        r, k, v, lw, a, b, gate = (ref[...] for ref in tok_refs)
        @pl.when(pl.program_id(1) == 0)
        def _():
            s_scr[...] = jnp.zeros_like(s_scr) if zero_state else _heads_to_pairs(s0_ref[...])
        s_in = s_scr[...]
    else:
        ns = tok_refs[0].shape[0] // t
        zpad = jnp.zeros((CHUNK - t, RWKV_W), F32)
        r, k, v, lw, a, b, gate = (
            jnp.concatenate([piece for s in range(ns) for piece in (ref[t * s:t * (s + 1), :], zpad)], axis=0)
            for ref in tok_refs)
        s_in = jnp.concatenate([_heads_to_pairs(s0_ref[s]) for s in range(ns)], axis=0)
    np_ = RWKV_HEADS // 2
    ones_bd = _ones_bd()
    inv_n = 1.0 / HEAD_DIM

    def store_heads(dst, s_c):
        for j in range(np_):
            dst[2 * j] = s_c[j, :, :HEAD_DIM]
            dst[2 * j + 1] = s_c[j, :, HEAD_DIM:]

    def finish(y, sl):
        mean = _seg_sum(y, ones_bd) * inv_n
        d = y - mean
        var = _seg_sum(d * d, ones_bd) * inv_n
        yn = d * lax.rsqrt(var + GN_EPS) * lng_ref[...] + lnb_ref[...]
        bonus = _seg_sum(r[sl] * k[sl] * rk_ref[...], ones_bd) * v[sl]
        return (yn + bonus) * gate[sl]

    rows = r.shape[0]
    sub = min(rows, WKV_TILE) if chain else rows
    slices = [slice(s0, s0 + sub) for s0 in range(0, rows, sub)]
    prepare = lambda sl: _wkv_prepare(r[sl], k[sl], v[sl], lw[sl], a[sl], b[sl])
    s_out = s_in
    outs = []
    for j, sl in enumerate(slices):
        y, s_out = _wkv_apply(s_out, prepare(sl), chain)
        outs.append(finish(y, sl))
        if companion is not None and j == 0:
            companion()
    out = outs[0] if len(outs) == 1 else jnp.concatenate(outs, axis=0)

    if chain:
        s_scr[...] = s_out
        o_ref[...] = out

        @pl.when(pl.program_id(1) == pl.num_programs(1) - 1)
        def _():
            store_heads(sout_ref, s_scr[...])
    else:
        for s in range(ns):
            store_heads(sout_ref.at[s], s_out[np_ * s:np_ * (s + 1)])
            o_ref[t * s:t * (s + 1), :] = out[CHUNK * s:CHUNK * s + t, :]


def _rwkv(tok, s0, post_w, n_seq, seq, tile_rows, chain):
    if chain:
        groups, nt, ns = n_seq, seq // tile_rows, None
    else:
        groups, nt, ns = n_seq * seq // tile_rows, 1, tile_rows // seq
    row = lambda g, i: (g * nt + i, 0)
    const = lambda g, i: (0, 0)
    state_spec = pl.BlockSpec((ns, RWKV_HEADS, HEAD_DIM, HEAD_DIM), lambda g, i: (g, 0, 0, 0))
    wide = pl.BlockSpec((tile_rows, RWKV_W), row)
    state_in = [] if s0 is None else [s0]
    return pl.pallas_call(
        functools.partial(_rwkv_kernel, chain=chain, zero_state=s0 is None, t=seq),
        grid=(groups, nt),
        in_specs=[wide] * 7 + [state_spec] * len(state_in) + [pl.BlockSpec(w.shape, const) for w in post_w],
        out_specs=[wide, state_spec],
        out_shape=[
            jax.ShapeDtypeStruct((n_seq * seq, RWKV_W), F32),
            jax.ShapeDtypeStruct((n_seq, RWKV_HEADS, HEAD_DIM, HEAD_DIM), F32),
        ],
        scratch_shapes=[pltpu.VMEM((RWKV_HEADS // 2, HEAD_DIM, LANES), F32)],
        compiler_params=pltpu.CompilerParams(
            dimension_semantics=("parallel", "arbitrary"), vmem_limit_bytes=VMEM_LIMIT),
        name="rwkv",
    )(*tok, *state_in, *post_w)


def _mixers_prompt_kernel(q_ref, k_ref, v_ref, kh_ref, vh_ref, sink_ref, *refs, tq):
    *rwkv_refs, att_ref, rw_ref, sout_ref, s_scr = refs
    attention = functools.partial(_attn_prompt_kernel, q_ref, k_ref, v_ref, kh_ref, vh_ref, sink_ref,
                                  att_ref, tq=tq)
    _rwkv_kernel(*rwkv_refs, rw_ref, sout_ref, s_scr, chain=True, zero_state=True, t=tq,
                 companion=attention)


def _mixers_prompt(q, k, v, sinks, tok, post_w, batch, seq, tq):
    sink_tile = _sink_tile(sinks, CHUNK, WINDOW + CHUNK)
    nt = seq // tq
    row = lambda b, i: (b * nt + i, 0)
    const = lambda b, i: (0, 0)
    halo = lambda b, i: (jnp.maximum((b * nt + i) * (tq // WINDOW) - 1, 0), 0)
    wide = pl.BlockSpec((tq, RWKV_W), row)
    state_spec = pl.BlockSpec((None, RWKV_HEADS, HEAD_DIM, HEAD_DIM), lambda b, i: (b, 0, 0, 0))
    return pl.pallas_call(
        functools.partial(_mixers_prompt_kernel, tq=tq),
        grid=(batch, nt),
        in_specs=[
            pl.BlockSpec((tq, ATT_Q), row),
            pl.BlockSpec((tq, ATT_KV), row),
            pl.BlockSpec((tq, ATT_KV), row),
            pl.BlockSpec((WINDOW, ATT_KV), halo),
            pl.BlockSpec((WINDOW, ATT_KV), halo),
            pl.BlockSpec((ATT_HEADS * CHUNK, LANES), const),
        ] + [wide] * 7 + [pl.BlockSpec(w.shape, const) for w in post_w],
        out_specs=[pl.BlockSpec((tq, ATT_Q), row), wide, state_spec],
        out_shape=[
            jax.ShapeDtypeStruct((batch * seq, ATT_Q), F32),
            jax.ShapeDtypeStruct((batch * seq, RWKV_W), F32),
            jax.ShapeDtypeStruct((batch, RWKV_HEADS, HEAD_DIM, HEAD_DIM), F32),
        ],
        scratch_shapes=[pltpu.VMEM((RWKV_HEADS // 2, HEAD_DIM, LANES), F32)],
        compiler_params=pltpu.CompilerParams(
            dimension_semantics=("parallel", "arbitrary"), vmem_limit_bytes=VMEM_LIMIT),
        name="mixers",
    )(q, k, v, k, v, sink_tile, *tok, *post_w)


def _ffn_kernel(x_ref, att_ref, rw_ref, woa_ref, wor_ref, g2_ref, wup_ref, wdn_ref, o_ref, *, tf):
    x2 = x_ref[...] + _dot(att_ref[...], woa_ref[...]) + _dot(rw_ref[...], wor_ref[...])
    h = (x2 * g2_ref[...]).astype(BF16)
    us = []
    for j in range(wup_ref.shape[1] // tf):
        u = jnp.maximum(lax.dot_general(h, wup_ref[:, tf * j:tf * (j + 1)], NN,
                                        preferred_element_type=F32), 0.0)
        us.append((u * u).astype(BF16))
    u_all = jnp.concatenate(us, axis=1)
    inv_ms = 1.0 / (jnp.mean(x2 * x2, axis=-1, keepdims=True) + NORM_EPS)
    o_ref[...] = x2 + inv_ms * lax.dot_general(u_all, wdn_ref[...], NN, preferred_element_type=F32)


def _out_ffn(x2d, att, rw, wo_att, wo_rw, g2, w_up, w_down, tm, tf):
    n, d = x2d.shape
    dff = w_up.shape[1]
    row = lambda i: (i, 0)
    const = lambda i: (0, 0)
    return pl.pallas_call(
        functools.partial(_ffn_kernel, tf=tf),
        grid=(n // tm,),
        in_specs=[
            pl.BlockSpec((tm, d), row),
            pl.BlockSpec((tm, ATT_Q), row),
            pl.BlockSpec((tm, RWKV_W), row),
            _resident((ATT_Q, d), const),
            _resident((RWKV_W, d), const),
            _resident((1, d), const),
            _resident((d, dff), const),
            _resident((dff, d), const),
        ],
        out_specs=pl.BlockSpec((tm, d), row),
        out_shape=jax.ShapeDtypeStruct((n, d), F32),
        compiler_params=pltpu.CompilerParams(
            dimension_semantics=("parallel",), vmem_limit_bytes=VMEM_LIMIT),
        name="out_ffn",
    )(x2d, att, rw, wo_att, wo_rw, g2, w_up, w_down)


def _rope_tables(pos):
    half = HEAD_DIM // 2
    inv = ROPE_THETA ** (-jnp.arange(half, dtype=F32) / half)
    ang = pos.astype(F32)[:, None] * inv[None, :]
    cos = jnp.cos(ang)
    sin = jnp.sin(ang)
    zero = jnp.zeros_like(sin)
    tile = lambda t: jnp.concatenate([t, t], axis=1)
    return (tile(jnp.concatenate([cos, cos], axis=1)),
            tile(jnp.concatenate([-sin, zero], axis=1)),
            tile(jnp.concatenate([zero, sin], axis=1)))


def _pair_perm():
    idx = []
    for j in range(ATT_HEADS // 2):
        idx += list(range(HEAD_DIM * j, HEAD_DIM * (j + 1)))
        idx += list(range(HEAD_DIM * (j + 4), HEAD_DIM * (j + 5)))
    return np.asarray(idx, dtype=np.int32)


def _pad_rw(t):
    return jnp.pad(t, [(0, 0)] * (t.ndim - 1) + [(0, RW_COLS - RW_REAL)])


def _layer(x, tabs, k_past, v_past, shift_prev, wkv0, lw, tm):
    (att_w, rw_pre_w, rw_post_w, sink8, wo_att, wo_rw, g2, w_up, w_down) = lw
    b, t, d = x.shape
    n = b * t
    x2d = x.reshape(n, d)
    q, k, v, *tok, tail = _inproj(x2d, _pad_rw(shift_prev)[:, None, :], att_w, rw_pre_w, tabs, tm, t)

    if k_past is None:
        att, rw, wkv = _mixers_prompt(q, k, v, sink8, tok, rw_post_w, b, t, min(t, 2 * WKV_TILE))
        rows = min(WINDOW, t)
        last = lambda a: a.reshape(b, t, ATT_KV)[:, t - rows:].reshape(b, rows, ATT_KV_HEADS, HEAD_DIM)
        new_k, new_v = last(k), last(v)
        shift_out = tail[:, SUBLANES - 1, :RW_REAL]
    else:
        per_tile = max(m for m in (8, 4, 2, 1) if b % m == 0)
        att = _attn_sample(q, k, v, k_past.reshape(-1, ATT_KV), v_past.reshape(-1, ATT_KV),
                           sink8, b, t, per_tile)
        new_k = k.reshape(b, t, ATT_KV_HEADS, HEAD_DIM)
        new_v = v.reshape(b, t, ATT_KV_HEADS, HEAD_DIM)
        shift_out = tail.reshape(b, t, RW_COLS)[:, t - 1, :RW_REAL]
        rw, wkv = _rwkv(tok, wkv0, rw_post_w, b, t, per_tile * t, False)
    y = _out_ffn(x2d, att, rw, wo_att, wo_rw, g2, w_up, w_down, tm, 1024)
    return (y.reshape(b, t, d), new_k, new_v, wkv, shift_out)


def kernel(x_prompt, x_sample, cache_attn_k, cache_attn_v, state_rwkv_wkv, state_rwkv_shift, ln1_g, w_in, q_norm_g, k_norm_g, attn_sinks, shift_mu, decay_w0, decay_w2, iclr_a0, iclr_a2, gate_g2, k_k, k_a, r_k, lnx_g, lnx_b, w_out, ln2_g, w_up, w_down):
    bp, tp, d = x_prompt.shape
    bs, ts, _ = x_sample.shape
    depth = w_in.shape[0]
    perm = _pair_perm()
    tabs_p = _rope_tables(jnp.arange(tp))
    tabs_s = tuple(jnp.tile(t, (bs, 1)) for t in _rope_tables(PAST_LEN + jnp.arange(ts)))
    tm_p = min(512, bp * tp)
    tm_s = min(512, bs * ts)

    hp, hs = x_prompt, x_sample
    outs_p, outs_s = [], []
    for l in range(depth):
        wl = w_in[l]
        att_cols = ATT_Q + 2 * ATT_KV
        w_in_p = jnp.concatenate(
            [wl[:, :ATT_Q][:, perm], wl[:, ATT_Q:att_cols], _pad_rw(wl[:, att_cols:])], axis=1).astype(BF16)
        row2 = lambda t: t.reshape(1, -1)
        zeros64 = jnp.zeros((64, RWKV_W), F32)
        att_w = (row2(ln1_g[l]), w_in_p, row2(jnp.tile(q_norm_g[l], ATT_HEADS)),
                 row2(jnp.tile(k_norm_g[l], ATT_KV_HEADS)))
        rw_pre_w = (
            row2(_pad_rw(shift_mu[l])),
            row2(decay_w0[l]),
            jnp.concatenate([decay_w2[l], zeros64], axis=0).astype(BF16),
            row2(iclr_a0[l]),
            jnp.concatenate([zeros64, iclr_a2[l]], axis=0).astype(BF16),
            jnp.pad(gate_g2[l], ((0, GATE_PAD - gate_g2.shape[1]), (0, 0))).astype(BF16),
            row2(k_k[l]), row2(k_a[l]),
        )
        rw_post_w = (row2(r_k[l]), row2(lnx_g[l]), row2(lnx_b[l]))
        lw = (att_w, rw_pre_w, rw_post_w, attn_sinks[l],
              w_out[l][:ATT_Q][perm].astype(BF16), w_out[l][ATT_Q:].astype(BF16),
              row2(ln2_g[l]), w_up[l].astype(BF16), w_down[l].astype(BF16))
        zero_shift = jnp.zeros((bp, RW_REAL), F32)
        hp, *op = _layer(hp, tabs_p, None, None, zero_shift, None, lw, tm_p)
        hs, *os_ = _layer(hs, tabs_s, cache_attn_k[l], cache_attn_v[l], state_rwkv_shift[l],
                          state_rwkv_wkv[l], lw, tm_s)
        outs_p.append(op)
        outs_s.append(os_)
    stack = lambda outs, i: jnp.stack([o[i] for o in outs])
    return (hp, hs,
            stack(outs_p, 0), stack(outs_p, 1), stack(outs_p, 2), stack(outs_p, 3),
            stack(outs_s, 0), stack(outs_s, 1), stack(outs_s, 2), stack(outs_s, 3))
```

```python
import functools

import jax
import jax.numpy as jnp
import numpy as np
from jax import lax
from jax.experimental import pallas as pl
from jax.experimental.pallas import tpu as pltpu

F32 = jnp.float32
BF16 = jnp.bfloat16

CHUNK = 64
WINDOW = 128
HEAD_DIM = 64
ATT_HEADS = 8
ATT_KV_HEADS = 2
RWKV_HEADS = 8
RWKV_W = 512
ATT_Q = 512
ATT_KV = 128
LORA_WA = 128
GATE_PAD = 256
RW_COLS = 3 * RWKV_W + LORA_WA + GATE_PAD
RW_REAL = 3 * RWKV_W + 64 + 64 + 160
IN_COLS_PAD = ATT_Q + 2 * ATT_KV + RW_COLS
PAST_LEN = 4096
ROPE_THETA = 10000.0
ATT_SCALE = HEAD_DIM ** -0.5
NORM_EPS = 1e-6
GN_EPS = 64e-5
DECAY_SCALE = float(np.exp(-0.5))
LANES = 128
SUBLANES = 8
WKV_TILE = 256
VMEM_LIMIT = 52 * 1024 * 1024
FFN_VMEM_LIMIT = 57 * 1024 * 1024

NN = (((1,), (0,)), ((), ()))
NT = (((1,), (1,)), ((), ()))
BNN = (((2,), (1,)), ((0,), (0,)))
BNT = (((2,), (2,)), ((0,), (0,)))
BTN = (((1,), (1,)), ((0,), (0,)))


def _dot(a, b, dims=NN):
    return lax.dot_general(a.astype(BF16), b.astype(BF16), dims, preferred_element_type=F32)


def _split2(x):
    hi = x.astype(BF16)
    lo = (x - hi.astype(F32)).astype(BF16)
    return hi, lo


def _seg_sum(x, ones_bd, split=True):
    outs = []
    for j in range(x.shape[1] // LANES):
        xb = x[:, LANES * j:LANES * (j + 1)]
        if split:
            lhs, rhs = jnp.concatenate(_split2(xb), axis=1), ones_bd
        else:
            lhs, rhs = xb.astype(BF16), ones_bd[:LANES]
        outs.append(lax.dot_general(lhs, rhs, NN, preferred_element_type=F32))
    return outs[0] if len(outs) == 1 else jnp.concatenate(outs, axis=1)


def _ones_bd():
    r = lax.broadcasted_iota(jnp.int32, (2 * LANES, LANES), 0) & (LANES - 1)
    c = lax.broadcasted_iota(jnp.int32, (2 * LANES, LANES), 1)
    return jnp.where((r < HEAD_DIM) == (c < HEAD_DIM), 1.0, 0.0).astype(BF16)


def _lane_lo(shape):
    return lax.broadcasted_iota(jnp.int32, shape, len(shape) - 1) < HEAD_DIM


_resident = functools.partial(pl.BlockSpec, pipeline_mode=pl.Buffered(1))


def _inproj_kernel(x_ref, sprev_ref, g1_ref, w_ref, qg_ref, kg_ref, cos_ref, sa_ref, sb_ref,
                   mu_ref, w0_ref, w2_ref, a0_ref, a2_ref, g2_ref, kk_ref, ka_ref,
                   q_ref, k_ref, v_ref, r_ref, km_ref, vv_ref, lw_ref, a_ref, b_ref, gate_ref,
                   tail_ref, carry_scr, *, seq):
    tm = x_ref.shape[0]
    rows = lax.broadcasted_iota(jnp.int32, (tm, 1), 0)
    rw0 = ATT_Q + 2 * ATT_KV
    ones_bd = _ones_bd()
    x = x_ref[...]
    ms = jnp.mean(x * x, axis=-1, keepdims=True)
    h = (x * lax.rsqrt(ms + NORM_EPS) * g1_ref[...]).astype(BF16)

    def project(c0, c1):
        return lax.dot_general(h, w_ref[:, c0:c1], NN, preferred_element_type=F32)

    if seq >= tm:
        @pl.when(pl.program_id(0) % (seq // tm) == 0)
        def _():
            carry_scr[0:1, :] = sprev_ref[...]

    def shifted(c0, c1):
        p = project(rw0 + c0, rw0 + c1)
        prev = pltpu.roll(p, 1, 0)
        if seq >= tm:
            prev = jnp.where(rows == 0, carry_scr[0:1, c0:c1], prev)
            carry_scr[0:1, c0:c1] = p[tm - 1:tm, :]
            tail_ref[:, c0:c1] = p[tm - SUBLANES:tm, :]
        else:
            for s in range(tm // seq):
                prev = jnp.where(rows == s * seq, sprev_ref[s, :, c0:c1], prev)
            tail_ref[:, c0:c1] = p
        return p + (prev - p) * mu_ref[:, c0:c1]

    cos = cos_ref[...]
    sin_a = sa_ref[...]
    sin_b = sb_ref[...]

    def norm_rope(p, g):
        ss = _seg_sum(p * p, ones_bd, split=False)
        y = p * lax.rsqrt(ss * (1.0 / HEAD_DIM) + NORM_EPS) * g
        outs = []
        for j in range(p.shape[1] // LANES):
            yb = y[:, LANES * j:LANES * (j + 1)]
            outs.append(yb * cos + pltpu.roll(yb, LANES - 32, 1) * sin_a + pltpu.roll(yb, 32, 1) * sin_b)
        return outs[0] if len(outs) == 1 else jnp.concatenate(outs, axis=1)

    lora = shifted(3 * RWKV_W, RW_COLS)
    k = shifted(RWKV_W, 2 * RWKV_W)
    sigmoid = lambda z: 0.5 + 0.5 * jnp.tanh(0.5 * z)

    def token_maps(rs):
        wa = lora[rs, :LORA_WA]
        gd = lora[rs, LORA_WA:]
        kr = k[rs]
        zw = w0_ref[...] + _dot(jnp.tanh(wa), w2_ref[...])
        lw_ref[rs, :] = -DECAY_SCALE * sigmoid(zw)
        a_ic = sigmoid(a0_ref[...] + _dot(wa, a2_ref[...]))
        gate_ref[rs, :] = _dot(sigmoid(gd), g2_ref[...])
        kk = kr * kk_ref[...]
        kk = kk * lax.rsqrt(jnp.maximum(_seg_sum(kk * kk, ones_bd, split=False), 1e-24))
        km_ref[rs, :] = kr * (1.0 + (a_ic - 1.0) * ka_ref[...])
        a_ref[rs, :] = -kk
        b_ref[rs, :] = kk * a_ic

    token_maps(slice(0, tm))
    q_ref[...] = norm_rope(project(0, ATT_Q), qg_ref[...])
    pkv = project(ATT_Q, ATT_Q + 2 * ATT_KV)
    k_ref[...] = norm_rope(pkv[:, :ATT_KV], kg_ref[...])
    v_ref[...] = pkv[:, ATT_KV:]
    r_ref[...] = shifted(0, RWKV_W)
    vv_ref[...] = shifted(2 * RWKV_W, 3 * RWKV_W)


def _inproj(x2d, sprev, att_w, rw_w, tabs, tm, seq):
    n, d = x2d.shape
    g1, w_in_p, qg, kg = att_w
    cos_t, sin_a, sin_b = tabs
    tab_blocks = cos_t.shape[0] // tm
    row = lambda i: (i, 0)
    const = lambda i: (0, 0)
    tab = lambda i: (i % tab_blocks, 0)
    if seq >= tm:
        tps = seq // tm
        sprev_spec = pl.BlockSpec((None, 1, RW_COLS), lambda i: (i // tps, 0, 0))
        tail_spec = pl.BlockSpec((None, SUBLANES, RW_COLS), lambda i: (i // tps, 0, 0))
        tail_shape = jax.ShapeDtypeStruct((n // seq, SUBLANES, RW_COLS), F32)
    else:
        sprev_spec = pl.BlockSpec((tm // seq, 1, RW_COLS), lambda i: (i, 0, 0))
        tail_spec = pl.BlockSpec((tm, RW_COLS), row)
        tail_shape = jax.ShapeDtypeStruct((n, RW_COLS), F32)
    wide = pl.BlockSpec((tm, RWKV_W), row)
    wide_shape = jax.ShapeDtypeStruct((n, RWKV_W), F32)
    return pl.pallas_call(
        functools.partial(_inproj_kernel, seq=seq),
        grid=(n // tm,),
        in_specs=[
            pl.BlockSpec((tm, d), row),
            sprev_spec,
            _resident((1, d), const),
            _resident((d, IN_COLS_PAD), const),
            _resident((1, ATT_Q), const),
            _resident((1, ATT_KV), const),
            pl.BlockSpec((tm, LANES), tab),
            pl.BlockSpec((tm, LANES), tab),
            pl.BlockSpec((tm, LANES), tab),
        ] + [_resident(w.shape, const) for w in rw_w],
        out_specs=[
            pl.BlockSpec((tm, ATT_Q), row),
            pl.BlockSpec((tm, ATT_KV), row),
            pl.BlockSpec((tm, ATT_KV), row),
        ] + [wide] * 7 + [tail_spec],
        out_shape=[
            jax.ShapeDtypeStruct((n, ATT_Q), F32),
            jax.ShapeDtypeStruct((n, ATT_KV), F32),
            jax.ShapeDtypeStruct((n, ATT_KV), F32),
        ] + [wide_shape] * 7 + [tail_shape],
        scratch_shapes=[pltpu.VMEM((SUBLANES, RW_COLS), F32)],
        compiler_params=pltpu.CompilerParams(
            dimension_semantics=("arbitrary",), vmem_limit_bytes=VMEM_LIMIT),
        name="inproj",
    )(x2d, sprev, g1, w_in_p, qg, kg, cos_t, sin_a, sin_b, *rw_w)


def _attend(q, kc, vc, sink_tile, first_key_pos):
    tq = q.shape[0]
    nk = kc.shape[0]
    pad = 2 * LANES - nk
    lo = _lane_lo((tq, LANES))
    blocks = [q[:, LANES * j:LANES * (j + 1)] * ATT_SCALE for j in range(ATT_HEADS // 2)]
    stacked = jnp.concatenate([jnp.where(lo, b, 0.0) for b in blocks]
                              + [jnp.where(lo, 0.0, b) for b in blocks], axis=0).astype(BF16)
    zpad = jnp.zeros((pad, LANES), BF16)
    k_ext = jnp.concatenate([kc, zpad], axis=0)
    v_ones = jnp.concatenate([jnp.concatenate([vc, zpad], axis=0),
                              jnp.ones((2 * LANES, LANES), BF16)], axis=1)
    s = lax.dot_general(stacked, k_ext, NT, preferred_element_type=F32)
    lane = lax.broadcasted_iota(jnp.int32, (ATT_HEADS * tq, LANES), 1)
    s0 = s[:, :LANES]
    s1 = s[:, LANES:]
    if first_key_pos is not None:
        s0 = jnp.where(lane + first_key_pos >= 0, s0, -jnp.inf)
        s1 = jnp.where(lane + (first_key_pos + LANES) >= 0, s1, -jnp.inf)
    s1 = jnp.where(lane < nk - LANES, s1, sink_tile)
    m = jnp.max(jnp.maximum(s0, s1), axis=-1, keepdims=True)
    p = jnp.concatenate([jnp.exp(s0 - m), jnp.exp(s1 - m)], axis=1).astype(BF16)
    od = lax.dot_general(p, v_ones, NN, preferred_element_type=F32)
    o = od[:, :LANES] / od[:, LANES:]
    half = (ATT_HEADS // 2) * tq
    outs = [jnp.where(lo, o[tq * j:tq * (j + 1)], o[half + tq * j:half + tq * (j + 1)])
            for j in range(ATT_HEADS // 2)]
    return jnp.concatenate(outs, axis=1)


def _attn_prompt_kernel(q_ref, k_ref, v_ref, kh_ref, vh_ref, sink_ref, o_ref, *, tq):
    i = pl.program_id(1)
    kcat = jnp.concatenate([kh_ref[...], k_ref[...]], axis=0).astype(BF16)
    vcat = jnp.concatenate([vh_ref[...], v_ref[...]], axis=0).astype(BF16)
    sink_tile = sink_ref[...]
    span = WINDOW + CHUNK
    for c in range(tq // CHUNK):
        first_key_pos = i * tq + c * CHUNK - WINDOW if c < WINDOW // CHUNK else None
        o_ref[CHUNK * c:CHUNK * (c + 1), :] = _attend(
            q_ref[CHUNK * c:CHUNK * (c + 1), :], kcat[CHUNK * c:CHUNK * c + span],
            vcat[CHUNK * c:CHUNK * c + span], sink_tile, first_key_pos)


def _sink_tile(sinks, rows_per_head, nk):
    lane = jnp.arange(LANES)[None, :]
    col = jnp.repeat(sinks.astype(F32), rows_per_head)[:, None]
    return jnp.where(lane == nk - LANES, col, jnp.where(lane > nk - LANES, -jnp.inf, 0.0))


def _attn_sample_kernel(q_ref, k_ref, v_ref, kc_ref, vc_ref, sink_ref, o_ref, *, t):
    ns = q_ref.shape[0] // t
    rows = kc_ref.shape[0] // ns
    for s in range(ns):
        new = slice(t * s, t * (s + 1))
        old = slice(rows * s, rows * (s + 1))
        kall = jnp.concatenate([kc_ref[old, :], k_ref[new, :]], axis=0).astype(BF16)
        vall = jnp.concatenate([vc_ref[old, :], v_ref[new, :]], axis=0).astype(BF16)
        o_ref[new, :] = _attend(q_ref[new, :], kall, vall, sink_ref[...], None)


def _attn_sample(q, k, v, k_cache, v_cache, sinks, batch, t, per_step):
    rows = k_cache.shape[0] // batch
    sink_tile = _sink_tile(sinks, t, rows + t)
    row = lambda b: (b, 0)
    return pl.pallas_call(
        functools.partial(_attn_sample_kernel, t=t),
        grid=(batch // per_step,),
        in_specs=[
            pl.BlockSpec((per_step * t, ATT_Q), row),
            pl.BlockSpec((per_step * t, ATT_KV), row),
            pl.BlockSpec((per_step * t, ATT_KV), row),
            pl.BlockSpec((per_step * rows, ATT_KV), row),
            pl.BlockSpec((per_step * rows, ATT_KV), row),
            pl.BlockSpec((ATT_HEADS * t, LANES), lambda b: (0, 0)),
        ],
        out_specs=pl.BlockSpec((per_step * t, ATT_Q), row),
        out_shape=jax.ShapeDtypeStruct((batch * t, ATT_Q), F32),
        compiler_params=pltpu.CompilerParams(dimension_semantics=("parallel",)),
        name="attn_sample",
    )(q, k, v, k_cache, v_cache, sink_tile)


def _systems(x, rows):
    nc = x.shape[0] // rows
    return jnp.stack([x[rows * c:rows * (c + 1), LANES * j:LANES * (j + 1)]
                      for c in range(nc) for j in range(RWKV_HEADS // 2)])


def _bdot(a, b, dims):
    return lax.dot_general(a.astype(BF16), b.astype(BF16), dims, preferred_element_type=F32)


def _wkv_prepare(r, k, v, lw, a, b):
    c = CHUNK
    tt = r.shape[0]
    nc = tt // c
    np_ = RWKV_HEADS // 2
    ri = lax.broadcasted_iota(jnp.int32, (tt, tt), 0)
    ci = lax.broadcasted_iota(jnp.int32, (tt, tt), 1)
    tri = jnp.where((ri >= ci) & ((ri & -c) == (ci & -c)), 1.0, 0.0).astype(BF16)
    hi = lw.astype(BF16)
    rem = lw - hi.astype(F32)
    mid = rem.astype(BF16)
    low = (rem - mid.astype(F32)).astype(BF16)
    cum = (lax.dot_general(tri, hi, NN, preferred_element_type=F32)
           + lax.dot_general(tri, mid, NN, preferred_element_type=F32)
           + lax.dot_general(tri, low, NN, preferred_element_type=F32))
    e = jnp.exp(cum)
    e_inv = jnp.exp(-cum)
    e_x = jnp.exp(cum - lw)
    rt = r * e
    at = a * e_x
    kt = k * e_inv
    bt = b * e_inv

    lo = _lane_lo((1, 1, LANES))
    at_p, rt_p, kt_p, bt_p, v_p = (_systems(t, c) for t in (at, rt, kt, bt, v))
    ec_p = jnp.stack([e[c * i + c - 1:c * i + c, LANES * j:LANES * (j + 1)]
                      for i in range(nc) for j in range(np_)])
    kh_p = kt_p * ec_p
    bh_p = bt_p * ec_p
    at0 = jnp.where(lo, at_p, 0.0)
    at1 = jnp.where(lo, 0.0, at_p)
    rt0 = jnp.where(lo, rt_p, 0.0)
    rt1 = jnp.where(lo, 0.0, rt_p)
    lhs = jnp.concatenate([at0, at1, rt0, rt1], axis=1)
    rhs = jnp.concatenate([bt_p, kt_p], axis=1)
    g = _bdot(lhs, rhs, BNT)

    r128 = lax.broadcasted_iota(jnp.int32, (1, LANES, LANES), 1)
    l128 = lax.broadcasted_iota(jnp.int32, (1, LANES, LANES), 2)
    t_idx = r128 & (c - 1)
    s_idx = l128 & (c - 1)
    ga = jnp.where(s_idx < t_idx, g[:, :LANES], 0.0)
    gr = jnp.where(s_idx <= t_idx, g[:, LANES:], 0.0)
    same = (r128 < c) == (l128 < c)

    block_diag = _block_diag

    def head_rows(x):
        return jnp.concatenate([x[:, :c], x[:, c:]], axis=2)

    a_c = jnp.where(lo, ga[:, :c], jnp.stack(
        [pltpu.roll(ga[j, c:], c, 1) for j in range(ga.shape[0])]))
    r64 = lax.broadcasted_iota(jnp.int32, (1, c, LANES), 1)
    l64 = lax.broadcasted_iota(jnp.int32, (1, c, LANES), 2)
    t_c = jnp.where((l64 & (c - 1)) == r64, 1.0, 0.0) + a_c
    p_c = _bdot(a_c, block_diag(a_c), BNN)
    for level in range(5):
        last = level == 4
        lhs_tp = t_c if last else jnp.concatenate([t_c, p_c], axis=1)
        prod = _bdot(lhs_tp, block_diag(p_c), BNN)
        t_c = t_c + prod[:, :c]
        if not last:
            p_c = prod[:, c:]

    zeros = jnp.zeros_like(v_p)
    zv = jnp.concatenate([zeros, jnp.where(lo, v_p, 0.0), zeros, jnp.where(lo, 0.0, v_p)], axis=1)
    xak = _bdot(head_rows(ga), zv, BNN)
    z = jnp.concatenate([jnp.concatenate([at0, at1], axis=1), block_diag(xak)], axis=2)
    tzs = _bdot(t_c, z, BNN)
    q_f32 = jnp.concatenate([tzs, jnp.concatenate([zeros, v_p], axis=2)], axis=1)
    q_mat = q_f32.astype(BF16)
    gq = _bdot(gr, q_mat, BNN)
    r_eff = rt_p + jnp.where(lo, gq[:, :c, :LANES], gq[:, c:, :LANES])
    y0 = jnp.where(lo, gq[:, :c, LANES:], gq[:, c:, LANES:])
    bk = jnp.concatenate([bh_p, kh_p], axis=1)
    mn = _bdot(q_mat, bk, BTN)
    wb = jnp.where(same, mn[:, :LANES], 0.0)
    n_c = jnp.where(lo, mn[:, LANES:LANES + c], mn[:, LANES + c:])

    return r_eff, y0, wb, n_c, ec_p


def _block_diag(xc):
    lo = _lane_lo((1, 1, LANES))
    return jnp.concatenate([jnp.where(lo, xc, 0), jnp.where(lo, 0, xc)], axis=1)


def _wkv_apply(s_c, prep, chain):
    r_eff, y0, wb, n_c, ec_p = prep
    np_ = RWKV_HEADS // 2
    nc = r_eff.shape[0] // np_
    flat = lambda y_i: jnp.concatenate([y_i[j] for j in range(np_)], axis=1)
    if chain:
        ys = []
        for i in range(nc):
            sl = slice(np_ * i, np_ * (i + 1))
            ys.append(flat(_bdot(r_eff[sl], _block_diag(s_c), BNT) + y0[sl]))
            s_c = s_c * ec_p[sl] + _bdot(s_c, wb[sl], BNN) + n_c[sl]
    else:
        y_all = _bdot(r_eff, _block_diag(s_c), BNT) + y0
        ys = [flat(y_all[np_ * i:np_ * (i + 1)]) for i in range(nc)]
        s_c = s_c * ec_p + _bdot(s_c, wb, BNN) + n_c
    y = ys[0] if nc == 1 else jnp.concatenate(ys, axis=0)
    return y, s_c


def _heads_to_pairs(s8):
    return jnp.stack([jnp.concatenate([s8[2 * j], s8[2 * j + 1]], axis=1)
                      for j in range(RWKV_HEADS // 2)])


def _rwkv_kernel(*refs, chain, zero_state, t, companion=None):
    tok_refs, refs = refs[:7], refs[7:]
    if not zero_state:
        s0_ref, refs = refs[0], refs[1:]
    rk_ref, lng_ref, lnb_ref, o_ref, sout_ref, s_scr = refs
    if chain:
        r, k, v, lw, a, b, gate = (ref[...] for ref in tok_refs)
        @pl.when(pl.program_id(1) == 0)
        def _():
            s_scr[...] = jnp.zeros_like(s_scr) if zero_state else _heads_to_pairs(s0_ref[...])
        s_in = s_scr[...]
    else:
        ns = tok_refs[0].shape[0] // t
        zpad = jnp.zeros((CHUNK - t, RWKV_W), F32)
        r, k, v, lw, a, b, gate = (
            jnp.concatenate([piece for s in range(ns) for piece in (ref[t * s:t * (s + 1), :], zpad)], axis=0)
            for ref in tok_refs)
        s_in = jnp.concatenate([_heads_to_pairs(s0_ref[s]) for s in range(ns)], axis=0)
    np_ = RWKV_HEADS // 2
    ones_bd = _ones_bd()
    inv_n = 1.0 / HEAD_DIM

    def store_heads(dst, s_c):
        for j in range(np_):
            dst[2 * j] = s_c[j, :, :HEAD_DIM]
            dst[2 * j + 1] = s_c[j, :, HEAD_DIM:]

    def finish(y, sl):
        mean = _seg_sum(y, ones_bd) * inv_n
        d = y - mean
        var = _seg_sum(d * d, ones_bd) * inv_n
        yn = d * lax.rsqrt(var + GN_EPS) * lng_ref[...] + lnb_ref[...]
        bonus = _seg_sum(r[sl] * k[sl] * rk_ref[...], ones_bd) * v[sl]
        return (yn + bonus) * gate[sl]

    rows = r.shape[0]
    sub = min(rows, WKV_TILE) if chain else rows
    slices = [slice(s0, s0 + sub) for s0 in range(0, rows, sub)]
    prepare = lambda sl: _wkv_prepare(r[sl], k[sl], v[sl], lw[sl], a[sl], b[sl])
    s_out = s_in
    outs = []
    for j, sl in enumerate(slices):
        y, s_out = _wkv_apply(s_out, prepare(sl), chain)
        outs.append(finish(y, sl))
        if companion is not None and j == 0:
            companion()
    out = outs[0] if len(outs) == 1 else jnp.concatenate(outs, axis=0)

    if chain:
        s_scr[...] = s_out
        o_ref[...] = out

        @pl.when(pl.program_id(1) == pl.num_programs(1) - 1)
        def _():
            store_heads(sout_ref, s_scr[...])
    else:
        for s in range(ns):
            store_heads(sout_ref.at[s], s_out[np_ * s:np_ * (s + 1)])
            o_ref[t * s:t * (s + 1), :] = out[CHUNK * s:CHUNK * s + t, :]


def _rwkv(tok, s0, post_w, n_seq, seq, tile_rows, chain):
    if chain:
        groups, nt, ns = n_seq, seq // tile_rows, None
    else:
        groups, nt, ns = n_seq * seq // tile_rows, 1, tile_rows // seq
    row = lambda g, i: (g * nt + i, 0)
    const = lambda g, i: (0, 0)
    state_spec = pl.BlockSpec((ns, RWKV_HEADS, HEAD_DIM, HEAD_DIM), lambda g, i: (g, 0, 0, 0))
    wide = pl.BlockSpec((tile_rows, RWKV_W), row)
    state_in = [] if s0 is None else [s0]
    return pl.pallas_call(
        functools.partial(_rwkv_kernel, chain=chain, zero_state=s0 is None, t=seq),
        grid=(groups, nt),
        in_specs=[wide] * 7 + [state_spec] * len(state_in) + [pl.BlockSpec(w.shape, const) for w in post_w],
        out_specs=[wide, state_spec],
        out_shape=[
            jax.ShapeDtypeStruct((n_seq * seq, RWKV_W), F32),
            jax.ShapeDtypeStruct((n_seq, RWKV_HEADS, HEAD_DIM, HEAD_DIM), F32),
        ],
        scratch_shapes=[pltpu.VMEM((RWKV_HEADS // 2, HEAD_DIM, LANES), F32)],
        compiler_params=pltpu.CompilerParams(
            dimension_semantics=("parallel", "arbitrary"), vmem_limit_bytes=VMEM_LIMIT),
        name="rwkv",
    )(*tok, *state_in, *post_w)


def _mixers_prompt_kernel(q_ref, k_ref, v_ref, kh_ref, vh_ref, sink_ref, *refs, tq):
    *rwkv_refs, att_ref, rw_ref, sout_ref, s_scr = refs
    attention = functools.partial(_attn_prompt_kernel, q_ref, k_ref, v_ref, kh_ref, vh_ref, sink_ref,
                                  att_ref, tq=tq)
    _rwkv_kernel(*rwkv_refs, rw_ref, sout_ref, s_scr, chain=True, zero_state=True, t=tq,
                 companion=attention)


def _mixers_prompt(q, k, v, sinks, tok, post_w, batch, seq, tq):
    sink_tile = _sink_tile(sinks, CHUNK, WINDOW + CHUNK)
    nt = seq // tq
    row = lambda b, i: (b * nt + i, 0)
    const = lambda b, i: (0, 0)
    halo = lambda b, i: (jnp.maximum((b * nt + i) * (tq // WINDOW) - 1, 0), 0)
    wide = pl.BlockSpec((tq, RWKV_W), row)
    state_spec = pl.BlockSpec((None, RWKV_HEADS, HEAD_DIM, HEAD_DIM), lambda b, i: (b, 0, 0, 0))
    return pl.pallas_call(
        functools.partial(_mixers_prompt_kernel, tq=tq),
        grid=(batch, nt),
        in_specs=[
            pl.BlockSpec((tq, ATT_Q), row),
            pl.BlockSpec((tq, ATT_KV), row),
            pl.BlockSpec((tq, ATT_KV), row),
            pl.BlockSpec((WINDOW, ATT_KV), halo),
            pl.BlockSpec((WINDOW, ATT_KV), halo),
            pl.BlockSpec((ATT_HEADS * CHUNK, LANES), const),
        ] + [wide] * 7 + [pl.BlockSpec(w.shape, const) for w in post_w],
        out_specs=[pl.BlockSpec((tq, ATT_Q), row), wide, state_spec],
        out_shape=[
            jax.ShapeDtypeStruct((batch * seq, ATT_Q), F32),
            jax.ShapeDtypeStruct((batch * seq, RWKV_W), F32),
            jax.ShapeDtypeStruct((batch, RWKV_HEADS, HEAD_DIM, HEAD_DIM), F32),
        ],
        scratch_shapes=[pltpu.VMEM((RWKV_HEADS // 2, HEAD_DIM, LANES), F32)],
        compiler_params=pltpu.CompilerParams(
            dimension_semantics=("parallel", "arbitrary"), vmem_limit_bytes=VMEM_LIMIT),
        name="mixers",
    )(q, k, v, k, v, sink_tile, *tok, *post_w)


def _ffn_kernel(x_ref, att_ref, rw_ref, woa_ref, wor_ref, g2_ref, wup_ref, wdn_ref, o_ref, *, tf):
    x2 = x_ref[...] + _dot(att_ref[...], woa_ref[...]) + _dot(rw_ref[...], wor_ref[...])
    h = (x2 * g2_ref[...]).astype(BF16)
    us = []
    for j in range(wup_ref.shape[1] // tf):
        u = jnp.maximum(lax.dot_general(h, wup_ref[:, tf * j:tf * (j + 1)], NN,
                                        preferred_element_type=F32), 0.0)
        us.append((u * u).astype(BF16))
    u_all = jnp.concatenate(us, axis=1)
    inv_ms = 1.0 / (jnp.mean(x2 * x2, axis=-1, keepdims=True) + NORM_EPS)
    o_ref[...] = x2 + inv_ms * lax.dot_general(u_all, wdn_ref[...], NN, preferred_element_type=F32)


def _out_ffn(x2d, att, rw, wo_att, wo_rw, g2, w_up, w_down, tm, tf):
    n, d = x2d.shape
    dff = w_up.shape[1]
    row = lambda i: (i, 0)
    const = lambda i: (0, 0)
    return pl.pallas_call(
        functools.partial(_ffn_kernel, tf=tf),
        grid=(n // tm,),
        in_specs=[
            pl.BlockSpec((tm, d), row),
            pl.BlockSpec((tm, ATT_Q), row),
            pl.BlockSpec((tm, RWKV_W), row),
            _resident((ATT_Q, d), const),
            _resident((RWKV_W, d), const),
            _resident((1, d), const),
            _resident((d, dff), const),
            _resident((dff, d), const),
        ],
        out_specs=pl.BlockSpec((tm, d), row),
        out_shape=jax.ShapeDtypeStruct((n, d), F32),
        compiler_params=pltpu.CompilerParams(
            dimension_semantics=("parallel",), vmem_limit_bytes=FFN_VMEM_LIMIT),
        name="out_ffn",
    )(x2d, att, rw, wo_att, wo_rw, g2, w_up, w_down)


def _rope_tables(pos):
    half = HEAD_DIM // 2
    inv = ROPE_THETA ** (-jnp.arange(half, dtype=F32) / half)
    ang = pos.astype(F32)[:, None] * inv[None, :]
    cos = jnp.cos(ang)
    sin = jnp.sin(ang)
    zero = jnp.zeros_like(sin)
    tile = lambda t: jnp.concatenate([t, t], axis=1)
    return (tile(jnp.concatenate([cos, cos], axis=1)),
            tile(jnp.concatenate([-sin, zero], axis=1)),
            tile(jnp.concatenate([zero, sin], axis=1)))


def _pair_perm():
    idx = []
    for j in range(ATT_HEADS // 2):
        idx += list(range(HEAD_DIM * j, HEAD_DIM * (j + 1)))
        idx += list(range(HEAD_DIM * (j + 4), HEAD_DIM * (j + 5)))
    return np.asarray(idx, dtype=np.int32)


def _pad_rw(t):
    return jnp.pad(t, [(0, 0)] * (t.ndim - 1) + [(0, RW_COLS - RW_REAL)])


def _layer(x, tabs, k_past, v_past, shift_prev, wkv0, lw, tm):
    (att_w, rw_pre_w, rw_post_w, sink8, wo_att, wo_rw, g2, w_up, w_down) = lw
    b, t, d = x.shape
    n = b * t
    x2d = x.reshape(n, d)
    q, k, v, *tok, tail = _inproj(x2d, _pad_rw(shift_prev)[:, None, :], att_w, rw_pre_w, tabs, tm, t)

    if k_past is None:
        att, rw, wkv = _mixers_prompt(q, k, v, sink8, tok, rw_post_w, b, t, min(t, 2 * WKV_TILE))
        rows = min(WINDOW, t)
        last = lambda a: a.reshape(b, t, ATT_KV)[:, t - rows:].reshape(b, rows, ATT_KV_HEADS, HEAD_DIM)
        new_k, new_v = last(k), last(v)
        shift_out = tail[:, SUBLANES - 1, :RW_REAL]
    else:
        per_tile = max(m for m in (8, 4, 2, 1) if b % m == 0)
        att = _attn_sample(q, k, v, k_past.reshape(-1, ATT_KV), v_past.reshape(-1, ATT_KV),
                           sink8, b, t, per_tile)
        new_k = k.reshape(b, t, ATT_KV_HEADS, HEAD_DIM)
        new_v = v.reshape(b, t, ATT_KV_HEADS, HEAD_DIM)
        shift_out = tail.reshape(b, t, RW_COLS)[:, t - 1, :RW_REAL]
        rw, wkv = _rwkv(tok, wkv0, rw_post_w, b, t, per_tile * t, False)
    y = _out_ffn(x2d, att, rw, wo_att, wo_rw, g2, w_up, w_down, min(n, 1024), 1024)
    return (y.reshape(b, t, d), new_k, new_v, wkv, shift_out)


def kernel(x_prompt, x_sample, cache_attn_k, cache_attn_v, state_rwkv_wkv, state_rwkv_shift, ln1_g, w_in, q_norm_g, k_norm_g, attn_sinks, shift_mu, decay_w0, decay_w2, iclr_a0, iclr_a2, gate_g2, k_k, k_a, r_k, lnx_g, lnx_b, w_out, ln2_g, w_up, w_down):
    bp, tp, d = x_prompt.shape
    bs, ts, _ = x_sample.shape
    depth = w_in.shape[0]
    perm = _pair_perm()
    tabs_p = _rope_tables(jnp.arange(tp))
    tabs_s = tuple(jnp.tile(t, (bs, 1)) for t in _rope_tables(PAST_LEN + jnp.arange(ts)))
    tm_p = min(512, bp * tp)
    tm_s = min(512, bs * ts)

    hp, hs = x_prompt, x_sample
    outs_p, outs_s = [], []
    for l in range(depth):
        wl = w_in[l]
        att_cols = ATT_Q + 2 * ATT_KV
        w_in_p = jnp.concatenate(
            [wl[:, :ATT_Q][:, perm], wl[:, ATT_Q:att_cols], _pad_rw(wl[:, att_cols:])], axis=1).astype(BF16)
        row2 = lambda t: t.reshape(1, -1)
        zeros64 = jnp.zeros((64, RWKV_W), F32)
        att_w = (row2(ln1_g[l]), w_in_p, row2(jnp.tile(q_norm_g[l], ATT_HEADS)),
                 row2(jnp.tile(k_norm_g[l], ATT_KV_HEADS)))
        rw_pre_w = (
            row2(_pad_rw(shift_mu[l])),
            row2(decay_w0[l]),
            jnp.concatenate([decay_w2[l], zeros64], axis=0).astype(BF16),
            row2(iclr_a0[l]),
            jnp.concatenate([zeros64, iclr_a2[l]], axis=0).astype(BF16),
            jnp.pad(gate_g2[l], ((0, GATE_PAD - gate_g2.shape[1]), (0, 0))).astype(BF16),
            row2(k_k[l]), row2(k_a[l]),
        )
        rw_post_w = (row2(r_k[l]), row2(lnx_g[l]), row2(lnx_b[l]))
        lw = (att_w, rw_pre_w, rw_post_w, attn_sinks[l],
              w_out[l][:ATT_Q][perm].astype(BF16), w_out[l][ATT_Q:].astype(BF16),
              row2(ln2_g[l]), w_up[l].astype(BF16), w_down[l].astype(BF16))
        zero_shift = jnp.zeros((bp, RW_REAL), F32)
        hp, *op = _layer(hp, tabs_p, None, None, zero_shift, None, lw, tm_p)
        hs, *os_ = _layer(hs, tabs_s, cache_attn_k[l], cache_attn_v[l], state_rwkv_shift[l],
                          state_rwkv_wkv[l], lw, tm_s)
        outs_p.append(op)
        outs_s.append(os_)
    stack = lambda outs, i: jnp.stack([o[i] for o in outs])
    return (hp, hs,
            stack(outs_p, 0), stack(outs_p, 1), stack(outs_p, 2), stack(outs_p, 3),
            stack(outs_s, 0), stack(outs_s, 1), stack(outs_s, 2), stack(outs_s, 3))
```

```python
import functools

import jax
import jax.numpy as jnp
import numpy as np
from jax import lax
from jax.experimental import pallas as pl
from jax.experimental.pallas import tpu as pltpu

F32 = jnp.float32
BF16 = jnp.bfloat16

CHUNK = 64
WINDOW = 128
HEAD_DIM = 64
ATT_HEADS = 8
ATT_KV_HEADS = 2
RWKV_HEADS = 8
RWKV_W = 512
ATT_Q = 512
ATT_KV = 128
LORA_WA = 128
GATE_PAD = 256
RW_COLS = 3 * RWKV_W + LORA_WA + GATE_PAD
RW_REAL = 3 * RWKV_W + 64 + 64 + 160
IN_COLS_PAD = ATT_Q + 2 * ATT_KV + RW_COLS
PAST_LEN = 4096
ROPE_THETA = 10000.0
ATT_SCALE = HEAD_DIM ** -0.5
NORM_EPS = 1e-6
GN_EPS = 64e-5
DECAY_SCALE = float(np.exp(-0.5))
LANES = 128
SUBLANES = 8
WKV_TILE = 256
VMEM_LIMIT = 52 * 1024 * 1024
FFN_VMEM_LIMIT = 57 * 1024 * 1024

NN = (((1,), (0,)), ((), ()))
NT = (((1,), (1,)), ((), ()))
BNN = (((2,), (1,)), ((0,), (0,)))
BNT = (((2,), (2,)), ((0,), (0,)))
BTN = (((1,), (1,)), ((0,), (0,)))


def _dot(a, b, dims=NN):
    return lax.dot_general(a.astype(BF16), b.astype(BF16), dims, preferred_element_type=F32)


def _split2(x):
    hi = x.astype(BF16)
    lo = (x - hi.astype(F32)).astype(BF16)
    return hi, lo


def _seg_sum(x, ones_bd, split=True):
    outs = []
    for j in range(x.shape[1] // LANES):
        xb = x[:, LANES * j:LANES * (j + 1)]
        if split:
            lhs, rhs = jnp.concatenate(_split2(xb), axis=1), ones_bd
        else:
            lhs, rhs = xb.astype(BF16), ones_bd[:LANES]
        outs.append(lax.dot_general(lhs, rhs, NN, preferred_element_type=F32))
    return outs[0] if len(outs) == 1 else jnp.concatenate(outs, axis=1)


def _ones_bd():
    r = lax.broadcasted_iota(jnp.int32, (2 * LANES, LANES), 0) & (LANES - 1)
    c = lax.broadcasted_iota(jnp.int32, (2 * LANES, LANES), 1)
    return jnp.where((r < HEAD_DIM) == (c < HEAD_DIM), 1.0, 0.0).astype(BF16)


def _lane_lo(shape):
    return lax.broadcasted_iota(jnp.int32, shape, len(shape) - 1) < HEAD_DIM


_resident = functools.partial(pl.BlockSpec, pipeline_mode=pl.Buffered(1))


def _inproj_kernel(x_ref, sprev_ref, g1_ref, w_ref, qg_ref, kg_ref, cos_ref, sa_ref, sb_ref,
                   mu_ref, w0_ref, w2_ref, a0_ref, a2_ref, g2_ref, kk_ref, ka_ref,
                   q_ref, k_ref, v_ref, r_ref, km_ref, vv_ref, lw_ref, a_ref, b_ref, gate_ref,
                   tail_ref, carry_scr, *, seq):
    tm = x_ref.shape[0]
    rows = lax.broadcasted_iota(jnp.int32, (tm, 1), 0)
    rw0 = ATT_Q + 2 * ATT_KV
    ones_bd = _ones_bd()
    x = x_ref[...]
    ms = jnp.mean(x * x, axis=-1, keepdims=True)
    h = (x * lax.rsqrt(ms + NORM_EPS) * g1_ref[...]).astype(BF16)

    def project(c0, c1):
        return lax.dot_general(h, w_ref[:, c0:c1], NN, preferred_element_type=F32)

    if seq >= tm:
        @pl.when(pl.program_id(0) % (seq // tm) == 0)
        def _():
            carry_scr[0:1, :] = sprev_ref[...]

    def shifted(c0, c1):
        p = project(rw0 + c0, rw0 + c1)
        prev = pltpu.roll(p, 1, 0)
        if seq >= tm:
            prev = jnp.where(rows == 0, carry_scr[0:1, c0:c1], prev)
            carry_scr[0:1, c0:c1] = p[tm - 1:tm, :]
            tail_ref[:, c0:c1] = p[tm - SUBLANES:tm, :]
        else:
            for s in range(tm // seq):
                prev = jnp.where(rows == s * seq, sprev_ref[s, :, c0:c1], prev)
            tail_ref[:, c0:c1] = p
        return p + (prev - p) * mu_ref[:, c0:c1]

    cos = cos_ref[...]
    sin_a = sa_ref[...]
    sin_b = sb_ref[...]

    def norm_rope(p, g):
        ss = _seg_sum(p * p, ones_bd, split=False)
        y = p * lax.rsqrt(ss * (1.0 / HEAD_DIM) + NORM_EPS) * g
        outs = []
        for j in range(p.shape[1] // LANES):
            yb = y[:, LANES * j:LANES * (j + 1)]
            outs.append(yb * cos + pltpu.roll(yb, LANES - 32, 1) * sin_a + pltpu.roll(yb, 32, 1) * sin_b)
        return outs[0] if len(outs) == 1 else jnp.concatenate(outs, axis=1)

    lora = shifted(3 * RWKV_W, RW_COLS)
    k = shifted(RWKV_W, 2 * RWKV_W)
    sigmoid = lambda z: 0.5 + 0.5 * jnp.tanh(0.5 * z)

    def token_maps(rs):
        wa = lora[rs, :LORA_WA]
        gd = lora[rs, LORA_WA:]
        kr = k[rs]
        zw = w0_ref[...] + _dot(jnp.tanh(wa), w2_ref[...])
        lw_ref[rs, :] = -DECAY_SCALE * sigmoid(zw)
        a_ic = sigmoid(a0_ref[...] + _dot(wa, a2_ref[...]))
        gate_ref[rs, :] = _dot(sigmoid(gd), g2_ref[...])
        kk = kr * kk_ref[...]
        kk = kk * lax.rsqrt(jnp.maximum(_seg_sum(kk * kk, ones_bd, split=False), 1e-24))
        km_ref[rs, :] = kr * (1.0 + (a_ic - 1.0) * ka_ref[...])
        a_ref[rs, :] = -kk
        b_ref[rs, :] = kk * a_ic

    token_maps(slice(0, tm))
    q_ref[...] = norm_rope(project(0, ATT_Q), qg_ref[...])
    pkv = project(ATT_Q, ATT_Q + 2 * ATT_KV)
    k_ref[...] = norm_rope(pkv[:, :ATT_KV], kg_ref[...])
    v_ref[...] = pkv[:, ATT_KV:]
    r_ref[...] = shifted(0, RWKV_W)
    vv_ref[...] = shifted(2 * RWKV_W, 3 * RWKV_W)


def _inproj(x2d, sprev, att_w, rw_w, tabs, tm, seq):
    n, d = x2d.shape
    g1, w_in_p, qg, kg = att_w
    cos_t, sin_a, sin_b = tabs
    tab_blocks = cos_t.shape[0] // tm
    row = lambda i: (i, 0)
    const = lambda i: (0, 0)
    tab = lambda i: (i % tab_blocks, 0)
    if seq >= tm:
        tps = seq // tm
        sprev_spec = pl.BlockSpec((None, 1, RW_COLS), lambda i: (i // tps, 0, 0))
        tail_spec = pl.BlockSpec((None, SUBLANES, RW_COLS), lambda i: (i // tps, 0, 0))
        tail_shape = jax.ShapeDtypeStruct((n // seq, SUBLANES, RW_COLS), F32)
    else:
        sprev_spec = pl.BlockSpec((tm // seq, 1, RW_COLS), lambda i: (i, 0, 0))
        tail_spec = pl.BlockSpec((tm, RW_COLS), row)
        tail_shape = jax.ShapeDtypeStruct((n, RW_COLS), F32)
    wide = pl.BlockSpec((tm, RWKV_W), row)
    wide_shape = jax.ShapeDtypeStruct((n, RWKV_W), F32)
    return pl.pallas_call(
        functools.partial(_inproj_kernel, seq=seq),
        grid=(n // tm,),
        in_specs=[
            pl.BlockSpec((tm, d), row),
            sprev_spec,
            _resident((1, d), const),
            _resident((d, IN_COLS_PAD), const),
            _resident((1, ATT_Q), const),
            _resident((1, ATT_KV), const),
            pl.BlockSpec((tm, LANES), tab),
            pl.BlockSpec((tm, LANES), tab),
            pl.BlockSpec((tm, LANES), tab),
        ] + [_resident(w.shape, const) for w in rw_w],
        out_specs=[
            pl.BlockSpec((tm, ATT_Q), row),
            pl.BlockSpec((tm, ATT_KV), row),
            pl.BlockSpec((tm, ATT_KV), row),
        ] + [wide] * 7 + [tail_spec],
        out_shape=[
            jax.ShapeDtypeStruct((n, ATT_Q), F32),
            jax.ShapeDtypeStruct((n, ATT_KV), F32),
            jax.ShapeDtypeStruct((n, ATT_KV), F32),
        ] + [wide_shape] * 7 + [tail_shape],
        scratch_shapes=[pltpu.VMEM((SUBLANES, RW_COLS), F32)],
        compiler_params=pltpu.CompilerParams(
            dimension_semantics=("arbitrary",), vmem_limit_bytes=VMEM_LIMIT),
        name="inproj",
    )(x2d, sprev, g1, w_in_p, qg, kg, cos_t, sin_a, sin_b, *rw_w)


def _attend(q, kc, vc, sink_tile, first_key_pos):
    tq = q.shape[0]
    nk = kc.shape[0]
    pad = 2 * LANES - nk
    lo = _lane_lo((tq, LANES))
    blocks = [q[:, LANES * j:LANES * (j + 1)] * ATT_SCALE for j in range(ATT_HEADS // 2)]
    stacked = jnp.concatenate([jnp.where(lo, b, 0.0) for b in blocks]
                              + [jnp.where(lo, 0.0, b) for b in blocks], axis=0).astype(BF16)
    zpad = jnp.zeros((pad, LANES), BF16)
    k_ext = jnp.concatenate([kc, zpad], axis=0)
    v_ones = jnp.concatenate([jnp.concatenate([vc, zpad], axis=0),
                              jnp.ones((2 * LANES, LANES), BF16)], axis=1)
    s = lax.dot_general(stacked, k_ext, NT, preferred_element_type=F32)
    lane = lax.broadcasted_iota(jnp.int32, (ATT_HEADS * tq, LANES), 1)
    s0 = s[:, :LANES]
    s1 = s[:, LANES:]
    if first_key_pos is not None:
        s0 = jnp.where(lane + first_key_pos >= 0, s0, -jnp.inf)
        s1 = jnp.where(lane + (first_key_pos + LANES) >= 0, s1, -jnp.inf)
    s1 = jnp.where(lane < nk - LANES, s1, sink_tile)
    m = jnp.max(jnp.maximum(s0, s1), axis=-1, keepdims=True)
    p = jnp.concatenate([jnp.exp(s0 - m), jnp.exp(s1 - m)], axis=1).astype(BF16)
    od = lax.dot_general(p, v_ones, NN, preferred_element_type=F32)
    o = od[:, :LANES] / od[:, LANES:]
    half = (ATT_HEADS // 2) * tq
    outs = [jnp.where(lo, o[tq * j:tq * (j + 1)], o[half + tq * j:half + tq * (j + 1)])
            for j in range(ATT_HEADS // 2)]
    return jnp.concatenate(outs, axis=1)


def _attn_prompt_kernel(q_ref, k_ref, v_ref, kh_ref, vh_ref, sink_ref, o_ref, *, tq):
    i = pl.program_id(1)
    kcat = jnp.concatenate([kh_ref[...], k_ref[...]], axis=0).astype(BF16)
    vcat = jnp.concatenate([vh_ref[...], v_ref[...]], axis=0).astype(BF16)
    sink_tile = sink_ref[...]
    span = WINDOW + CHUNK
    for c in range(tq // CHUNK):
        first_key_pos = i * tq + c * CHUNK - WINDOW if c < WINDOW // CHUNK else None
        o_ref[CHUNK * c:CHUNK * (c + 1), :] = _attend(
            q_ref[CHUNK * c:CHUNK * (c + 1), :], kcat[CHUNK * c:CHUNK * c + span],
            vcat[CHUNK * c:CHUNK * c + span], sink_tile, first_key_pos)


def _sink_tile(sinks, rows_per_head, nk):
    lane = jnp.arange(LANES)[None, :]
    col = jnp.repeat(sinks.astype(F32), rows_per_head)[:, None]
    return jnp.where(lane == nk - LANES, col, jnp.where(lane > nk - LANES, -jnp.inf, 0.0))


def _attn_sample_kernel(q_ref, k_ref, v_ref, kc_ref, vc_ref, sink_ref, o_ref, *, t):
    ns = q_ref.shape[0] // t
    rows = kc_ref.shape[0] // ns
    for s in range(ns):
        new = slice(t * s, t * (s + 1))
        old = slice(rows * s, rows * (s + 1))
        kall = jnp.concatenate([kc_ref[old, :], k_ref[new, :]], axis=0).astype(BF16)
        vall = jnp.concatenate([vc_ref[old, :], v_ref[new, :]], axis=0).astype(BF16)
        o_ref[new, :] = _attend(q_ref[new, :], kall, vall, sink_ref[...], None)


def _attn_sample(q, k, v, k_cache, v_cache, sinks, batch, t, per_step):
    rows = k_cache.shape[0] // batch
    sink_tile = _sink_tile(sinks, t, rows + t)
    row = lambda b: (b, 0)
    return pl.pallas_call(
        functools.partial(_attn_sample_kernel, t=t),
        grid=(batch // per_step,),
        in_specs=[
            pl.BlockSpec((per_step * t, ATT_Q), row),
            pl.BlockSpec((per_step * t, ATT_KV), row),
            pl.BlockSpec((per_step * t, ATT_KV), row),
            pl.BlockSpec((per_step * rows, ATT_KV), row),
            pl.BlockSpec((per_step * rows, ATT_KV), row),
            pl.BlockSpec((ATT_HEADS * t, LANES), lambda b: (0, 0)),
        ],
        out_specs=pl.BlockSpec((per_step * t, ATT_Q), row),
        out_shape=jax.ShapeDtypeStruct((batch * t, ATT_Q), F32),
        compiler_params=pltpu.CompilerParams(dimension_semantics=("parallel",)),
        name="attn_sample",
    )(q, k, v, k_cache, v_cache, sink_tile)


def _systems(x, rows):
    nc = x.shape[0] // rows
    return jnp.stack([x[rows * c:rows * (c + 1), LANES * j:LANES * (j + 1)]
                      for c in range(nc) for j in range(RWKV_HEADS // 2)])


def _bdot(a, b, dims):
    return lax.dot_general(a.astype(BF16), b.astype(BF16), dims, preferred_element_type=F32)


def _wkv_prepare(r, k, v, lw, a, b):
    c = CHUNK
    tt = r.shape[0]
    nc = tt // c
    np_ = RWKV_HEADS // 2
    ri = lax.broadcasted_iota(jnp.int32, (tt, tt), 0)
    ci = lax.broadcasted_iota(jnp.int32, (tt, tt), 1)
    tri = jnp.where((ri >= ci) & ((ri & -c) == (ci & -c)), 1.0, 0.0).astype(BF16)
    hi = lw.astype(BF16)
    rem = lw - hi.astype(F32)
    mid = rem.astype(BF16)
    low = (rem - mid.astype(F32)).astype(BF16)
    cum = (lax.dot_general(tri, hi, NN, preferred_element_type=F32)
           + lax.dot_general(tri, mid, NN, preferred_element_type=F32)
           + lax.dot_general(tri, low, NN, preferred_element_type=F32))
    e = jnp.exp(cum)
    e_inv = jnp.exp(-cum)
    e_x = jnp.exp(cum - lw)
    rt = r * e
    at = a * e_x
    kt = k * e_inv
    bt = b * e_inv

    lo = _lane_lo((1, 1, LANES))
    at_p, rt_p, kt_p, bt_p, v_p = (_systems(t, c) for t in (at, rt, kt, bt, v))
    ec_p = jnp.stack([e[c * i + c - 1:c * i + c, LANES * j:LANES * (j + 1)]
                      for i in range(nc) for j in range(np_)])
    kh_p = kt_p * ec_p
    bh_p = bt_p * ec_p
    at0 = jnp.where(lo, at_p, 0.0)
    at1 = jnp.where(lo, 0.0, at_p)
    rt0 = jnp.where(lo, rt_p, 0.0)
    rt1 = jnp.where(lo, 0.0, rt_p)
    lhs = jnp.concatenate([at0, at1, rt0, rt1], axis=1)
    rhs = jnp.concatenate([bt_p, kt_p], axis=1)
    g = _bdot(lhs, rhs, BNT)

    r128 = lax.broadcasted_iota(jnp.int32, (1, LANES, LANES), 1)
    l128 = lax.broadcasted_iota(jnp.int32, (1, LANES, LANES), 2)
    t_idx = r128 & (c - 1)
    s_idx = l128 & (c - 1)
    ga = jnp.where(s_idx < t_idx, g[:, :LANES], 0.0)
    gr = jnp.where(s_idx <= t_idx, g[:, LANES:], 0.0)
    same = (r128 < c) == (l128 < c)

    block_diag = _block_diag

    def head_rows(x):
        return jnp.concatenate([x[:, :c], x[:, c:]], axis=2)

    a_c = jnp.where(lo, ga[:, :c], jnp.stack(
        [pltpu.roll(ga[j, c:], c, 1) for j in range(ga.shape[0])]))
    r64 = lax.broadcasted_iota(jnp.int32, (1, c, LANES), 1)
    l64 = lax.broadcasted_iota(jnp.int32, (1, c, LANES), 2)
    t_c = jnp.where((l64 & (c - 1)) == r64, 1.0, 0.0) + a_c
    p_c = _bdot(a_c, block_diag(a_c), BNN)
    for level in range(5):
        last = level == 4
        lhs_tp = t_c if last else jnp.concatenate([t_c, p_c], axis=1)
        prod = _bdot(lhs_tp, block_diag(p_c), BNN)
        t_c = t_c + prod[:, :c]
        if not last:
            p_c = prod[:, c:]

    zeros = jnp.zeros_like(v_p)
    zv = jnp.concatenate([zeros, jnp.where(lo, v_p, 0.0), zeros, jnp.where(lo, 0.0, v_p)], axis=1)
    xak = _bdot(head_rows(ga), zv, BNN)
    z = jnp.concatenate([jnp.concatenate([at0, at1], axis=1), block_diag(xak)], axis=2)
    tzs = _bdot(t_c, z, BNN)
    q_f32 = jnp.concatenate([tzs, jnp.concatenate([zeros, v_p], axis=2)], axis=1)
    q_mat = q_f32.astype(BF16)
    gq = _bdot(gr, q_mat, BNN)
    r_eff = rt_p + jnp.where(lo, gq[:, :c, :LANES], gq[:, c:, :LANES])
    y0 = jnp.where(lo, gq[:, :c, LANES:], gq[:, c:, LANES:])
    bk = jnp.concatenate([bh_p, kh_p], axis=1)
    mn = _bdot(q_mat, bk, BTN)
    wb = jnp.where(same, mn[:, :LANES], 0.0)
    n_c = jnp.where(lo, mn[:, LANES:LANES + c], mn[:, LANES + c:])

    return r_eff, y0, wb, n_c, ec_p


def _block_diag(xc):
    lo = _lane_lo((1, 1, LANES))
    return jnp.concatenate([jnp.where(lo, xc, 0), jnp.where(lo, 0, xc)], axis=1)


def _wkv_apply(s_c, prep, chain):
    r_eff, y0, wb, n_c, ec_p = prep
    np_ = RWKV_HEADS // 2
    nc = r_eff.shape[0] // np_
    flat = lambda y_i: jnp.concatenate([y_i[j] for j in range(np_)], axis=1)
    if chain:
        ys = []
        for i in range(nc):
            sl = slice(np_ * i, np_ * (i + 1))
            ys.append(flat(_bdot(r_eff[sl], _block_diag(s_c), BNT) + y0[sl]))
            s_c = s_c * ec_p[sl] + _bdot(s_c, wb[sl], BNN) + n_c[sl]
    else:
        y_all = _bdot(r_eff, _block_diag(s_c), BNT) + y0
        ys = [flat(y_all[np_ * i:np_ * (i + 1)]) for i in range(nc)]
        s_c = s_c * ec_p + _bdot(s_c, wb, BNN) + n_c
    y = ys[0] if nc == 1 else jnp.concatenate(ys, axis=0)
    return y, s_c


def _heads_to_pairs(s8):
    return jnp.stack([jnp.concatenate([s8[2 * j], s8[2 * j + 1]], axis=1)
                      for j in range(RWKV_HEADS // 2)])


def _rwkv_kernel(*refs, chain, zero_state, t, companion=None):
    tok_refs, refs = refs[:7], refs[7:]
    if not zero_state:
        s0_ref, refs = refs[0], refs[1:]
    rk_ref, lng_ref, lnb_ref, o_ref, sout_ref, s_scr = refs
    if chain:
        r, k, v, lw, a, b, gate = (ref[...] for ref in tok_refs)
        @pl.when(pl.program_id(1) == 0)
        def _():
            s_scr[...] = jnp.zeros_like(s_scr) if zero_state else _heads_to_pairs(s0_ref[...])
        s_in = s_scr[...]
    else:
        ns = tok_refs[0].shape[0] // t
        zpad = jnp.zeros((CHUNK - t, RWKV_W), F32)
        r, k, v, lw, a, b, gate = (
            jnp.concatenate([piece for s in range(ns) for piece in (ref[t * s:t * (s + 1), :], zpad)], axis=0)
            for ref in tok_refs)
        s_in = jnp.concatenate([_heads_to_pairs(s0_ref[s]) for s in range(ns)], axis=0)
    np_ = RWKV_HEADS // 2
    ones_bd = _ones_bd()
    inv_n = 1.0 / HEAD_DIM

    def store_heads(dst, s_c):
        for j in range(np_):
            dst[2 * j] = s_c[j, :, :HEAD_DIM]
            dst[2 * j + 1] = s_c[j, :, HEAD_DIM:]

    def finish(y, sl):
        mean = _seg_sum(y, ones_bd) * inv_n
        d = y - mean
        var = _seg_sum(d * d, ones_bd) * inv_n
        yn = d * lax.rsqrt(var + GN_EPS) * lng_ref[...] + lnb_ref[...]
        bonus = _seg_sum(r[sl] * k[sl] * rk_ref[...], ones_bd) * v[sl]
        return (yn + bonus) * gate[sl]

    rows = r.shape[0]
    sub = min(rows, WKV_TILE) if chain else rows
    slices = [slice(s0, s0 + sub) for s0 in range(0, rows, sub)]
    prepare = lambda sl: _wkv_prepare(r[sl], k[sl], v[sl], lw[sl], a[sl], b[sl])
    s_out = s_in
    outs = []
    for j, sl in enumerate(slices):
        y, s_out = _wkv_apply(s_out, prepare(sl), chain)
        outs.append(finish(y, sl))
        if companion is not None and j == 0:
            companion()
    out = outs[0] if len(outs) == 1 else jnp.concatenate(outs, axis=0)

    if chain:
        s_scr[...] = s_out
        o_ref[...] = out

        @pl.when(pl.program_id(1) == pl.num_programs(1) - 1)
        def _():
            store_heads(sout_ref, s_scr[...])
    else:
        for s in range(ns):
            store_heads(sout_ref.at[s], s_out[np_ * s:np_ * (s + 1)])
            o_ref[t * s:t * (s + 1), :] = out[CHUNK * s:CHUNK * s + t, :]


def _rwkv(tok, s0, post_w, n_seq, seq, tile_rows, chain):
    if chain:
        groups, nt, ns = n_seq, seq // tile_rows, None
    else:
        groups, nt, ns = n_seq * seq // tile_rows, 1, tile_rows // seq
    row = lambda g, i: (g * nt + i, 0)
    const = lambda g, i: (0, 0)
    state_spec = pl.BlockSpec((ns, RWKV_HEADS, HEAD_DIM, HEAD_DIM), lambda g, i: (g, 0, 0, 0))
    wide = pl.BlockSpec((tile_rows, RWKV_W), row)
    state_in = [] if s0 is None else [s0]
    return pl.pallas_call(
        functools.partial(_rwkv_kernel, chain=chain, zero_state=s0 is None, t=seq),
        grid=(groups, nt),
        in_specs=[wide] * 7 + [state_spec] * len(state_in) + [pl.BlockSpec(w.shape, const) for w in post_w],
        out_specs=[wide, state_spec],
        out_shape=[
            jax.ShapeDtypeStruct((n_seq * seq, RWKV_W), F32),
            jax.ShapeDtypeStruct((n_seq, RWKV_HEADS, HEAD_DIM, HEAD_DIM), F32),
        ],
        scratch_shapes=[pltpu.VMEM((RWKV_HEADS // 2, HEAD_DIM, LANES), F32)],
        compiler_params=pltpu.CompilerParams(
            dimension_semantics=("parallel", "arbitrary"), vmem_limit_bytes=VMEM_LIMIT),
        name="rwkv",
    )(*tok, *state_in, *post_w)


def _mixers_prompt_kernel(q_ref, k_ref, v_ref, kh_ref, vh_ref, sink_ref, *refs, tq):
    *rwkv_refs, att_ref, rw_ref, sout_ref, s_scr = refs
    attention = functools.partial(_attn_prompt_kernel, q_ref, k_ref, v_ref, kh_ref, vh_ref, sink_ref,
                                  att_ref, tq=tq)
    _rwkv_kernel(*rwkv_refs, rw_ref, sout_ref, s_scr, chain=True, zero_state=True, t=tq,
                 companion=attention)


def _mixers_prompt(q, k, v, sinks, tok, post_w, batch, seq, tq):
    sink_tile = _sink_tile(sinks, CHUNK, WINDOW + CHUNK)
    nt = seq // tq
    row = lambda b, i: (b * nt + i, 0)
    const = lambda b, i: (0, 0)
    halo = lambda b, i: (jnp.maximum((b * nt + i) * (tq // WINDOW) - 1, 0), 0)
    wide = pl.BlockSpec((tq, RWKV_W), row)
    state_spec = pl.BlockSpec((None, RWKV_HEADS, HEAD_DIM, HEAD_DIM), lambda b, i: (b, 0, 0, 0))
    return pl.pallas_call(
        functools.partial(_mixers_prompt_kernel, tq=tq),
        grid=(batch, nt),
        in_specs=[
            pl.BlockSpec((tq, ATT_Q), row),
            pl.BlockSpec((tq, ATT_KV), row),
            pl.BlockSpec((tq, ATT_KV), row),
            pl.BlockSpec((WINDOW, ATT_KV), halo),
            pl.BlockSpec((WINDOW, ATT_KV), halo),
            pl.BlockSpec((ATT_HEADS * CHUNK, LANES), const),
        ] + [wide] * 7 + [pl.BlockSpec(w.shape, const) for w in post_w],
        out_specs=[pl.BlockSpec((tq, ATT_Q), row), wide, state_spec],
        out_shape=[
            jax.ShapeDtypeStruct((batch * seq, ATT_Q), F32),
            jax.ShapeDtypeStruct((batch * seq, RWKV_W), F32),
            jax.ShapeDtypeStruct((batch, RWKV_HEADS, HEAD_DIM, HEAD_DIM), F32),
        ],
        scratch_shapes=[pltpu.VMEM((RWKV_HEADS // 2, HEAD_DIM, LANES), F32)],
        compiler_params=pltpu.CompilerParams(
            dimension_semantics=("parallel", "arbitrary"), vmem_limit_bytes=VMEM_LIMIT),
        name="mixers",
    )(q, k, v, k, v, sink_tile, *tok, *post_w)


def _ffn_kernel(x_ref, att_ref, rw_ref, woa_ref, wor_ref, g2_ref, wup_ref, wdn_ref, o_ref, *, tf):
    x2 = x_ref[...] + _dot(att_ref[...], woa_ref[...]) + _dot(rw_ref[...], wor_ref[...])
    h = (x2 * g2_ref[...]).astype(BF16)
    us = []
    for j in range(wup_ref.shape[1] // tf):
        u = jnp.maximum(lax.dot_general(h, wup_ref[:, tf * j:tf * (j + 1)], NN,
                                        preferred_element_type=F32), 0.0)
        us.append((u * u).astype(BF16))
    u_all = jnp.concatenate(us, axis=1)
    inv_ms = 1.0 / (jnp.mean(x2 * x2, axis=-1, keepdims=True) + NORM_EPS)
    o_ref[...] = x2 + inv_ms * lax.dot_general(u_all, wdn_ref[...], NN, preferred_element_type=F32)


def _out_ffn(x2d, att, rw, wo_att, wo_rw, g2, w_up, w_down, tm, tf):
    n, d = x2d.shape
    dff = w_up.shape[1]
    row = lambda i: (i, 0)
    const = lambda i: (0, 0)
    return pl.pallas_call(
        functools.partial(_ffn_kernel, tf=tf),
        grid=(n // tm,),
        in_specs=[
            pl.BlockSpec((tm, d), row),
            pl.BlockSpec((tm, ATT_Q), row),
            pl.BlockSpec((tm, RWKV_W), row),
            _resident((ATT_Q, d), const),
            _resident((RWKV_W, d), const),
            _resident((1, d), const),
            _resident((d, dff), const),
            _resident((dff, d), const),
        ],
        out_specs=pl.BlockSpec((tm, d), row),
        out_shape=jax.ShapeDtypeStruct((n, d), F32),
        compiler_params=pltpu.CompilerParams(
            dimension_semantics=("parallel",), vmem_limit_bytes=FFN_VMEM_LIMIT),
        name="out_ffn",
    )(x2d, att, rw, wo_att, wo_rw, g2, w_up, w_down)


def _rope_tables(pos, reps=1):
    half = HEAD_DIM // 2
    inv = ROPE_THETA ** (-np.arange(half, dtype=np.float64) / half)
    ang = np.asarray(pos, np.float64)[:, None] * inv[None, :]
    cos = np.cos(ang)
    sin = np.sin(ang)
    zero = np.zeros_like(sin)
    lay = lambda lo, hi: jnp.asarray(np.tile(np.concatenate([lo, hi, lo, hi], axis=1), (reps, 1)), F32)
    return lay(cos, cos), lay(-sin, zero), lay(zero, sin)


def _pair_perm():
    idx = []
    for j in range(ATT_HEADS // 2):
        idx += list(range(HEAD_DIM * j, HEAD_DIM * (j + 1)))
        idx += list(range(HEAD_DIM * (j + 4), HEAD_DIM * (j + 5)))
    return np.asarray(idx, dtype=np.int32)


def _pad_rw(t):
    return jnp.pad(t, [(0, 0)] * (t.ndim - 1) + [(0, RW_COLS - RW_REAL)])


def _layer(x, tabs, k_past, v_past, shift_prev, wkv0, lw, tm):
    (att_w, rw_pre_w, rw_post_w, sink8, wo_att, wo_rw, g2, w_up, w_down) = lw
    b, t, d = x.shape
    n = b * t
    x2d = x.reshape(n, d)
    q, k, v, *tok, tail = _inproj(x2d, _pad_rw(shift_prev)[:, None, :], att_w, rw_pre_w, tabs, tm, t)

    if k_past is None:
        att, rw, wkv = _mixers_prompt(q, k, v, sink8, tok, rw_post_w, b, t, min(t, 2 * WKV_TILE))
        rows = min(WINDOW, t)
        last = lambda a: a.reshape(b, t, ATT_KV)[:, t - rows:].reshape(b, rows, ATT_KV_HEADS, HEAD_DIM)
        new_k, new_v = last(k), last(v)
        shift_out = tail[:, SUBLANES - 1, :RW_REAL]
    else:
        per_tile = max(m for m in (8, 4, 2, 1) if b % m == 0)
        att = _attn_sample(q, k, v, k_past.reshape(-1, ATT_KV), v_past.reshape(-1, ATT_KV),
                           sink8, b, t, per_tile)
        new_k = k.reshape(b, t, ATT_KV_HEADS, HEAD_DIM)
        new_v = v.reshape(b, t, ATT_KV_HEADS, HEAD_DIM)
        shift_out = tail.reshape(b, t, RW_COLS)[:, t - 1, :RW_REAL]
        rw, wkv = _rwkv(tok, wkv0, rw_post_w, b, t, per_tile * t, False)
    y = _out_ffn(x2d, att, rw, wo_att, wo_rw, g2, w_up, w_down, min(n, 1024), 1024)
    return (y.reshape(b, t, d), new_k, new_v, wkv, shift_out)


def kernel(x_prompt, x_sample, cache_attn_k, cache_attn_v, state_rwkv_wkv, state_rwkv_shift, ln1_g, w_in, q_norm_g, k_norm_g, attn_sinks, shift_mu, decay_w0, decay_w2, iclr_a0, iclr_a2, gate_g2, k_k, k_a, r_k, lnx_g, lnx_b, w_out, ln2_g, w_up, w_down):
    bp, tp, d = x_prompt.shape
    bs, ts, _ = x_sample.shape
    depth = w_in.shape[0]
    perm = _pair_perm()
    tabs_p = _rope_tables(np.arange(tp))
    tabs_s = _rope_tables(PAST_LEN + np.arange(ts), reps=bs)
    tm_p = min(512, bp * tp)
    tm_s = min(512, bs * ts)

    hp, hs = x_prompt, x_sample
    outs_p, outs_s = [], []
    for l in range(depth):
        wl = w_in[l]
        att_cols = ATT_Q + 2 * ATT_KV
        w_in_p = jnp.concatenate(
            [wl[:, :ATT_Q][:, perm], wl[:, ATT_Q:att_cols], _pad_rw(wl[:, att_cols:])], axis=1).astype(BF16)
        row2 = lambda t: t.reshape(1, -1)
        zeros64 = jnp.zeros((64, RWKV_W), F32)
        att_w = (row2(ln1_g[l]), w_in_p, row2(jnp.tile(q_norm_g[l], ATT_HEADS)),
                 row2(jnp.tile(k_norm_g[l], ATT_KV_HEADS)))
        rw_pre_w = (
            row2(_pad_rw(shift_mu[l])),
            row2(decay_w0[l]),
            jnp.concatenate([decay_w2[l], zeros64], axis=0).astype(BF16),
            row2(iclr_a0[l]),
            jnp.concatenate([zeros64, iclr_a2[l]], axis=0).astype(BF16),
            jnp.pad(gate_g2[l], ((0, GATE_PAD - gate_g2.shape[1]), (0, 0))).astype(BF16),
            row2(k_k[l]), row2(k_a[l]),
        )
        rw_post_w = (row2(r_k[l]), row2(lnx_g[l]), row2(lnx_b[l]))
        lw = (att_w, rw_pre_w, rw_post_w, attn_sinks[l],
              w_out[l][:ATT_Q][perm].astype(BF16), w_out[l][ATT_Q:].astype(BF16),
              row2(ln2_g[l]), w_up[l].astype(BF16), w_down[l].astype(BF16))
        zero_shift = jnp.zeros((bp, RW_REAL), F32)
        hp, *op = _layer(hp, tabs_p, None, None, zero_shift, None, lw, tm_p)
        hs, *os_ = _layer(hs, tabs_s, cache_attn_k[l], cache_attn_v[l], state_rwkv_shift[l],
                          state_rwkv_wkv[l], lw, tm_s)
        outs_p.append(op)
        outs_s.append(os_)
    stack = lambda outs, i: jnp.stack([o[i] for o in outs])
    return (hp, hs,
            stack(outs_p, 0), stack(outs_p, 1), stack(outs_p, 2), stack(outs_p, 3),
            stack(outs_s, 0), stack(outs_s, 1), stack(outs_s, 2), stack(outs_s, 3))
```

```python
import functools

import jax
import jax.numpy as jnp
import numpy as np
from jax import lax
from jax.experimental import pallas as pl
from jax.experimental.pallas import tpu as pltpu

F32 = jnp.float32
BF16 = jnp.bfloat16

CHUNK = 64
WINDOW = 128
HEAD_DIM = 64
ATT_HEADS = 8
ATT_KV_HEADS = 2
RWKV_HEADS = 8
RWKV_W = 512
ATT_Q = 512
ATT_KV = 128
LORA_WA = 128
GATE_PAD = 256
RW_COLS = 3 * RWKV_W + LORA_WA + GATE_PAD
RW_REAL = 3 * RWKV_W + 64 + 64 + 160
IN_COLS_PAD = ATT_Q + 2 * ATT_KV + RW_COLS
PAST_LEN = 4096
ROPE_THETA = 10000.0
ATT_SCALE = HEAD_DIM ** -0.5
NORM_EPS = 1e-6
GN_EPS = 64e-5
DECAY_SCALE = float(np.exp(-0.5))
LANES = 128
SUBLANES = 8
WKV_TILE = 256
VMEM_LIMIT = 52 * 1024 * 1024
FFN_VMEM_LIMIT = 57 * 1024 * 1024

NN = (((1,), (0,)), ((), ()))
NT = (((1,), (1,)), ((), ()))
BNN = (((2,), (1,)), ((0,), (0,)))
BNT = (((2,), (2,)), ((0,), (0,)))
BTN = (((1,), (1,)), ((0,), (0,)))


def _dot(a, b, dims=NN):
    return lax.dot_general(a.astype(BF16), b.astype(BF16), dims, preferred_element_type=F32)


def _split2(x):
    hi = x.astype(BF16)
    lo = (x - hi.astype(F32)).astype(BF16)
    return hi, lo


def _seg_sum(x, ones_bd, split=True):
    outs = []
    for j in range(x.shape[1] // LANES):
        xb = x[:, LANES * j:LANES * (j + 1)]
        if split:
            lhs, rhs = jnp.concatenate(_split2(xb), axis=1), ones_bd
        else:
            lhs, rhs = xb.astype(BF16), ones_bd[:LANES]
        outs.append(lax.dot_general(lhs, rhs, NN, preferred_element_type=F32))
    return outs[0] if len(outs) == 1 else jnp.concatenate(outs, axis=1)


def _ones_bd():
    r = lax.broadcasted_iota(jnp.int32, (2 * LANES, LANES), 0) & (LANES - 1)
    c = lax.broadcasted_iota(jnp.int32, (2 * LANES, LANES), 1)
    return jnp.where((r < HEAD_DIM) == (c < HEAD_DIM), 1.0, 0.0).astype(BF16)


def _lane_lo(shape):
    return lax.broadcasted_iota(jnp.int32, shape, len(shape) - 1) < HEAD_DIM


_resident = functools.partial(pl.BlockSpec, pipeline_mode=pl.Buffered(1))


def _inproj_kernel(x_ref, sprev_ref, g1_ref, w_ref, qg_ref, kg_ref, cos_ref, sa_ref, sb_ref,
                   mu_ref, w0_ref, w2_ref, a0_ref, a2_ref, g2_ref, kk_ref, ka_ref,
                   q_ref, k_ref, v_ref, r_ref, km_ref, vv_ref, lw_ref, a_ref, b_ref, gate_ref,
                   tail_ref, carry_scr, *, seq):
    tm = x_ref.shape[0]
    rows = lax.broadcasted_iota(jnp.int32, (tm, 1), 0)
    rw0 = ATT_Q + 2 * ATT_KV
    ones_bd = _ones_bd()
    x = x_ref[...]
    ms = jnp.mean(x * x, axis=-1, keepdims=True)
    h = (x * lax.rsqrt(ms + NORM_EPS) * g1_ref[...]).astype(BF16)

    def project(c0, c1):
        return lax.dot_general(h, w_ref[:, c0:c1], NN, preferred_element_type=F32)

    if seq >= tm:
        @pl.when(pl.program_id(0) % (seq // tm) == 0)
        def _():
            carry_scr[0:1, :] = sprev_ref[...]

    def shifted(c0, c1):
        p = project(rw0 + c0, rw0 + c1)
        prev = pltpu.roll(p, 1, 0)
        if seq >= tm:
            prev = jnp.where(rows == 0, carry_scr[0:1, c0:c1], prev)
            carry_scr[0:1, c0:c1] = p[tm - 1:tm, :]
            tail_ref[:, c0:c1] = p[tm - SUBLANES:tm, :]
        else:
            for s in range(tm // seq):
                prev = jnp.where(rows == s * seq, sprev_ref[s, :, c0:c1], prev)
            tail_ref[:, c0:c1] = p
        return p + (prev - p) * mu_ref[:, c0:c1]

    cos = cos_ref[...]
    sin_a = sa_ref[...]
    sin_b = sb_ref[...]

    def norm_rope(p, g):
        ss = _seg_sum(p * p, ones_bd, split=False)
        y = p * lax.rsqrt(ss * (1.0 / HEAD_DIM) + NORM_EPS) * g
        outs = []
        for j in range(p.shape[1] // LANES):
            yb = y[:, LANES * j:LANES * (j + 1)]
            outs.append(yb * cos + pltpu.roll(yb, LANES - 32, 1) * sin_a + pltpu.roll(yb, 32, 1) * sin_b)
        return outs[0] if len(outs) == 1 else jnp.concatenate(outs, axis=1)

    lora = shifted(3 * RWKV_W, RW_COLS)
    k = shifted(RWKV_W, 2 * RWKV_W)
    sigmoid = lambda z: 0.5 + 0.5 * jnp.tanh(0.5 * z)

    def token_maps(rs):
        wa = lora[rs, :LORA_WA]
        gd = lora[rs, LORA_WA:]
        kr = k[rs]
        zw = w0_ref[...] + _dot(jnp.tanh(wa), w2_ref[...])
        lw_ref[rs, :] = -DECAY_SCALE * sigmoid(zw)
        a_ic = sigmoid(a0_ref[...] + _dot(wa, a2_ref[...]))
        gate_ref[rs, :] = _dot(sigmoid(gd), g2_ref[...])
        kk = kr * kk_ref[...]
        kk = kk * lax.rsqrt(jnp.maximum(_seg_sum(kk * kk, ones_bd, split=False), 1e-24))
        km_ref[rs, :] = kr * (1.0 + (a_ic - 1.0) * ka_ref[...])
        a_ref[rs, :] = -kk
        b_ref[rs, :] = kk * a_ic

    token_maps(slice(0, tm))
    q_ref[...] = norm_rope(project(0, ATT_Q), qg_ref[...]).astype(BF16)
    pkv = project(ATT_Q, ATT_Q + 2 * ATT_KV)
    k_ref[...] = norm_rope(pkv[:, :ATT_KV], kg_ref[...])
    v_ref[...] = pkv[:, ATT_KV:]
    r_ref[...] = shifted(0, RWKV_W)
    vv_ref[...] = shifted(2 * RWKV_W, 3 * RWKV_W)


def _inproj(x2d, sprev, att_w, rw_w, tabs, tm, seq):
    n, d = x2d.shape
    g1, w_in_p, qg, kg = att_w
    cos_t, sin_a, sin_b = tabs
    tab_blocks = cos_t.shape[0] // tm
    row = lambda i: (i, 0)
    const = lambda i: (0, 0)
    tab = lambda i: (i % tab_blocks, 0)
    if seq >= tm:
        tps = seq // tm
        sprev_spec = pl.BlockSpec((None, 1, RW_COLS), lambda i: (i // tps, 0, 0))
        tail_spec = pl.BlockSpec((None, SUBLANES, RW_COLS), lambda i: (i // tps, 0, 0))
        tail_shape = jax.ShapeDtypeStruct((n // seq, SUBLANES, RW_COLS), F32)
    else:
        sprev_spec = pl.BlockSpec((tm // seq, 1, RW_COLS), lambda i: (i, 0, 0))
        tail_spec = pl.BlockSpec((tm, RW_COLS), row)
        tail_shape = jax.ShapeDtypeStruct((n, RW_COLS), F32)
    wide = pl.BlockSpec((tm, RWKV_W), row)
    wide_shape = jax.ShapeDtypeStruct((n, RWKV_W), F32)
    return pl.pallas_call(
        functools.partial(_inproj_kernel, seq=seq),
        grid=(n // tm,),
        in_specs=[
            pl.BlockSpec((tm, d), row),
            sprev_spec,
            _resident((1, d), const),
            _resident((d, IN_COLS_PAD), const),
            _resident((1, ATT_Q), const),
            _resident((1, ATT_KV), const),
            pl.BlockSpec((tm, LANES), tab),
            pl.BlockSpec((tm, LANES), tab),
            pl.BlockSpec((tm, LANES), tab),
        ] + [_resident(w.shape, const) for w in rw_w],
        out_specs=[
            pl.BlockSpec((tm, ATT_Q), row),
            pl.BlockSpec((tm, ATT_KV), row),
            pl.BlockSpec((tm, ATT_KV), row),
        ] + [wide] * 7 + [tail_spec],
        out_shape=[
            jax.ShapeDtypeStruct((n, ATT_Q), BF16),
            jax.ShapeDtypeStruct((n, ATT_KV), F32),
            jax.ShapeDtypeStruct((n, ATT_KV), F32),
        ] + [wide_shape] * 7 + [tail_shape],
        scratch_shapes=[pltpu.VMEM((SUBLANES, RW_COLS), F32)],
        compiler_params=pltpu.CompilerParams(
            dimension_semantics=("arbitrary",), vmem_limit_bytes=VMEM_LIMIT),
        name="inproj",
    )(x2d, sprev, g1, w_in_p, qg, kg, cos_t, sin_a, sin_b, *rw_w)


def _attend(q, kc, vc, sink_tile, first_key_pos):
    tq = q.shape[0]
    nk = kc.shape[0]
    pad = 2 * LANES - nk
    lo = _lane_lo((tq, LANES))
    blocks = [q[:, LANES * j:LANES * (j + 1)].astype(F32) * ATT_SCALE for j in range(ATT_HEADS // 2)]
    stacked = jnp.concatenate([jnp.where(lo, b, 0.0) for b in blocks]
                              + [jnp.where(lo, 0.0, b) for b in blocks], axis=0).astype(BF16)
    zpad = jnp.zeros((pad, LANES), BF16)
    k_ext = jnp.concatenate([kc, zpad], axis=0)
    v_ones = jnp.concatenate([jnp.concatenate([vc, zpad], axis=0),
                              jnp.ones((2 * LANES, LANES), BF16)], axis=1)
    s = lax.dot_general(stacked, k_ext, NT, preferred_element_type=F32)
    lane = lax.broadcasted_iota(jnp.int32, (ATT_HEADS * tq, LANES), 1)
    s0 = s[:, :LANES]
    s1 = s[:, LANES:]
    if first_key_pos is not None:
        s0 = jnp.where(lane + first_key_pos >= 0, s0, -jnp.inf)
        s1 = jnp.where(lane + (first_key_pos + LANES) >= 0, s1, -jnp.inf)
    s1 = jnp.where(lane < nk - LANES, s1, sink_tile)
    m = jnp.max(jnp.maximum(s0, s1), axis=-1, keepdims=True)
    p = jnp.concatenate([jnp.exp(s0 - m), jnp.exp(s1 - m)], axis=1).astype(BF16)
    od = lax.dot_general(p, v_ones, NN, preferred_element_type=F32)
    o = od[:, :LANES] / od[:, LANES:]
    half = (ATT_HEADS // 2) * tq
    outs = [jnp.where(lo, o[tq * j:tq * (j + 1)], o[half + tq * j:half + tq * (j + 1)])
            for j in range(ATT_HEADS // 2)]
    return jnp.concatenate(outs, axis=1)


def _attn_prompt_kernel(q_ref, k_ref, v_ref, kh_ref, vh_ref, sink_ref, o_ref, *, tq):
    i = pl.program_id(1)
    kcat = jnp.concatenate([kh_ref[...], k_ref[...]], axis=0).astype(BF16)
    vcat = jnp.concatenate([vh_ref[...], v_ref[...]], axis=0).astype(BF16)
    sink_tile = sink_ref[...]
    span = WINDOW + CHUNK
    for c in range(tq // CHUNK):
        first_key_pos = i * tq + c * CHUNK - WINDOW if c < WINDOW // CHUNK else None
        o_ref[CHUNK * c:CHUNK * (c + 1), :] = _attend(
            q_ref[CHUNK * c:CHUNK * (c + 1), :], kcat[CHUNK * c:CHUNK * c + span],
            vcat[CHUNK * c:CHUNK * c + span], sink_tile, first_key_pos)


def _sink_tile(sinks, rows_per_head, nk):
    lane = jnp.arange(LANES)[None, :]
    col = jnp.repeat(sinks.astype(F32), rows_per_head)[:, None]
    return jnp.where(lane == nk - LANES, col, jnp.where(lane > nk - LANES, -jnp.inf, 0.0))


def _attn_sample_kernel(q_ref, k_ref, v_ref, kc_ref, vc_ref, sink_ref, o_ref, *, t):
    ns = q_ref.shape[0] // t
    rows = kc_ref.shape[0] // ns
    for s in range(ns):
        new = slice(t * s, t * (s + 1))
        old = slice(rows * s, rows * (s + 1))
        kall = jnp.concatenate([kc_ref[old, :], k_ref[new, :]], axis=0).astype(BF16)
        vall = jnp.concatenate([vc_ref[old, :], v_ref[new, :]], axis=0).astype(BF16)
        o_ref[new, :] = _attend(q_ref[new, :], kall, vall, sink_ref[...], None)


def _attn_sample(q, k, v, k_cache, v_cache, sinks, batch, t, per_step):
    rows = k_cache.shape[0] // batch
    sink_tile = _sink_tile(sinks, t, rows + t)
    row = lambda b: (b, 0)
    return pl.pallas_call(
        functools.partial(_attn_sample_kernel, t=t),
        grid=(batch // per_step,),
        in_specs=[
            pl.BlockSpec((per_step * t, ATT_Q), row),
            pl.BlockSpec((per_step * t, ATT_KV), row),
            pl.BlockSpec((per_step * t, ATT_KV), row),
            pl.BlockSpec((per_step * rows, ATT_KV), row),
            pl.BlockSpec((per_step * rows, ATT_KV), row),
            pl.BlockSpec((ATT_HEADS * t, LANES), lambda b: (0, 0)),
        ],
        out_specs=pl.BlockSpec((per_step * t, ATT_Q), row),
        out_shape=jax.ShapeDtypeStruct((batch * t, ATT_Q), F32),
        compiler_params=pltpu.CompilerParams(dimension_semantics=("parallel",)),
        name="attn_sample",
    )(q, k, v, k_cache, v_cache, sink_tile)


def _systems(x, rows):
    nc = x.shape[0] // rows
    return jnp.stack([x[rows * c:rows * (c + 1), LANES * j:LANES * (j + 1)]
                      for c in range(nc) for j in range(RWKV_HEADS // 2)])


def _bdot(a, b, dims):
    return lax.dot_general(a.astype(BF16), b.astype(BF16), dims, preferred_element_type=F32)


def _wkv_prepare(r, k, v, lw, a, b):
    c = CHUNK
    tt = r.shape[0]
    nc = tt // c
    np_ = RWKV_HEADS // 2
    ri = lax.broadcasted_iota(jnp.int32, (tt, tt), 0)
    ci = lax.broadcasted_iota(jnp.int32, (tt, tt), 1)
    tri = jnp.where((ri >= ci) & ((ri & -c) == (ci & -c)), 1.0, 0.0).astype(BF16)
    hi = lw.astype(BF16)
    rem = lw - hi.astype(F32)
    mid = rem.astype(BF16)
    low = (rem - mid.astype(F32)).astype(BF16)
    cum = (lax.dot_general(tri, hi, NN, preferred_element_type=F32)
           + lax.dot_general(tri, mid, NN, preferred_element_type=F32)
           + lax.dot_general(tri, low, NN, preferred_element_type=F32))
    e = jnp.exp(cum)
    e_inv = jnp.exp(-cum)
    e_x = jnp.exp(cum - lw)
    rt = r * e
    at = a * e_x
    kt = k * e_inv
    bt = b * e_inv

    lo = _lane_lo((1, 1, LANES))
    at_p, rt_p, kt_p, bt_p, v_p = (_systems(t, c) for t in (at, rt, kt, bt, v))
    ec_p = jnp.stack([e[c * i + c - 1:c * i + c, LANES * j:LANES * (j + 1)]
                      for i in range(nc) for j in range(np_)])
    kh_p = kt_p * ec_p
    bh_p = bt_p * ec_p
    at0 = jnp.where(lo, at_p, 0.0)
    at1 = jnp.where(lo, 0.0, at_p)
    rt0 = jnp.where(lo, rt_p, 0.0)
    rt1 = jnp.where(lo, 0.0, rt_p)
    lhs = jnp.concatenate([at0, at1, rt0, rt1], axis=1)
    rhs = jnp.concatenate([bt_p, kt_p], axis=1)
    g = _bdot(lhs, rhs, BNT)

    r128 = lax.broadcasted_iota(jnp.int32, (1, LANES, LANES), 1)
    l128 = lax.broadcasted_iota(jnp.int32, (1, LANES, LANES), 2)
    t_idx = r128 & (c - 1)
    s_idx = l128 & (c - 1)
    ga = jnp.where(s_idx < t_idx, g[:, :LANES], 0.0)
    gr = jnp.where(s_idx <= t_idx, g[:, LANES:], 0.0)
    same = (r128 < c) == (l128 < c)

    block_diag = _block_diag

    def head_rows(x):
        return jnp.concatenate([x[:, :c], x[:, c:]], axis=2)

    a_c = jnp.where(lo, ga[:, :c], jnp.stack(
        [pltpu.roll(ga[j, c:], c, 1) for j in range(ga.shape[0])]))
    r64 = lax.broadcasted_iota(jnp.int32, (1, c, LANES), 1)
    l64 = lax.broadcasted_iota(jnp.int32, (1, c, LANES), 2)
    t_c = jnp.where((l64 & (c - 1)) == r64, 1.0, 0.0) + a_c
    p_c = _bdot(a_c, block_diag(a_c), BNN)
    for level in range(5):
        last = level == 4
        lhs_tp = t_c if last else jnp.concatenate([t_c, p_c], axis=1)
        prod = _bdot(lhs_tp, block_diag(p_c), BNN)
        t_c = t_c + prod[:, :c]
        if not last:
            p_c = prod[:, c:]

    zeros = jnp.zeros_like(v_p)
    zv = jnp.concatenate([zeros, jnp.where(lo, v_p, 0.0), zeros, jnp.where(lo, 0.0, v_p)], axis=1)
    xak = _bdot(head_rows(ga), zv, BNN)
    z = jnp.concatenate([jnp.concatenate([at0, at1], axis=1), block_diag(xak)], axis=2)
    tzs = _bdot(t_c, z, BNN)
    q_f32 = jnp.concatenate([tzs, jnp.concatenate([zeros, v_p], axis=2)], axis=1)
    q_mat = q_f32.astype(BF16)
    gq = _bdot(gr, q_mat, BNN)
    r_eff = rt_p + jnp.where(lo, gq[:, :c, :LANES], gq[:, c:, :LANES])
    y0 = jnp.where(lo, gq[:, :c, LANES:], gq[:, c:, LANES:])
    bk = jnp.concatenate([bh_p, kh_p], axis=1)
    mn = _bdot(q_mat, bk, BTN)
    wb = jnp.where(same, mn[:, :LANES], 0.0)
    n_c = jnp.where(lo, mn[:, LANES:LANES + c], mn[:, LANES + c:])

    return r_eff, y0, wb, n_c, ec_p


def _block_diag(xc):
    lo = _lane_lo((1, 1, LANES))
    return jnp.concatenate([jnp.where(lo, xc, 0), jnp.where(lo, 0, xc)], axis=1)


def _wkv_apply(s_c, prep, chain):
    r_eff, y0, wb, n_c, ec_p = prep
    np_ = RWKV_HEADS // 2
    nc = r_eff.shape[0] // np_
    flat = lambda y_i: jnp.concatenate([y_i[j] for j in range(np_)], axis=1)
    if chain:
        ys = []
        for i in range(nc):
            sl = slice(np_ * i, np_ * (i + 1))
            ys.append(flat(_bdot(r_eff[sl], _block_diag(s_c), BNT) + y0[sl]))
            s_c = s_c * ec_p[sl] + _bdot(s_c, wb[sl], BNN) + n_c[sl]
    else:
        y_all = _bdot(r_eff, _block_diag(s_c), BNT) + y0
        ys = [flat(y_all[np_ * i:np_ * (i + 1)]) for i in range(nc)]
        s_c = s_c * ec_p + _bdot(s_c, wb, BNN) + n_c
    y = ys[0] if nc == 1 else jnp.concatenate(ys, axis=0)
    return y, s_c


def _heads_to_pairs(s8):
    return jnp.stack([jnp.concatenate([s8[2 * j], s8[2 * j + 1]], axis=1)
                      for j in range(RWKV_HEADS // 2)])


def _rwkv_kernel(*refs, chain, zero_state, t, companion=None):
    tok_refs, refs = refs[:7], refs[7:]
    if not zero_state:
        s0_ref, refs = refs[0], refs[1:]
    rk_ref, lng_ref, lnb_ref, o_ref, sout_ref, s_scr = refs
    if chain:
        r, k, v, lw, a, b, gate = (ref[...] for ref in tok_refs)
        @pl.when(pl.program_id(1) == 0)
        def _():
            s_scr[...] = jnp.zeros_like(s_scr) if zero_state else _heads_to_pairs(s0_ref[...])
        s_in = s_scr[...]
    else:
        ns = tok_refs[0].shape[0] // t
        zpad = jnp.zeros((CHUNK - t, RWKV_W), F32)
        r, k, v, lw, a, b, gate = (
            jnp.concatenate([piece for s in range(ns) for piece in (ref[t * s:t * (s + 1), :], zpad)], axis=0)
            for ref in tok_refs)
        s_in = jnp.concatenate([_heads_to_pairs(s0_ref[s]) for s in range(ns)], axis=0)
    np_ = RWKV_HEADS // 2
    ones_bd = _ones_bd()
    inv_n = 1.0 / HEAD_DIM

    def store_heads(dst, s_c):
        for j in range(np_):
            dst[2 * j] = s_c[j, :, :HEAD_DIM]
            dst[2 * j + 1] = s_c[j, :, HEAD_DIM:]

    def finish(y, sl):
        mean = _seg_sum(y, ones_bd) * inv_n
        d = y - mean
        var = _seg_sum(d * d, ones_bd) * inv_n
        yn = d * lax.rsqrt(var + GN_EPS) * lng_ref[...] + lnb_ref[...]
        bonus = _seg_sum(r[sl] * k[sl] * rk_ref[...], ones_bd) * v[sl]
        return (yn + bonus) * gate[sl]

    rows = r.shape[0]
    sub = min(rows, WKV_TILE) if chain else rows
    slices = [slice(s0, s0 + sub) for s0 in range(0, rows, sub)]
    prepare = lambda sl: _wkv_prepare(r[sl], k[sl], v[sl], lw[sl], a[sl], b[sl])
    s_out = s_in
    outs = []
    for j, sl in enumerate(slices):
        y, s_out = _wkv_apply(s_out, prepare(sl), chain)
        outs.append(finish(y, sl))
        if companion is not None and j == 0:
            companion()
    out = outs[0] if len(outs) == 1 else jnp.concatenate(outs, axis=0)

    if chain:
        s_scr[...] = s_out
        o_ref[...] = out

        @pl.when(pl.program_id(1) == pl.num_programs(1) - 1)
        def _():
            store_heads(sout_ref, s_scr[...])
    else:
        for s in range(ns):
            store_heads(sout_ref.at[s], s_out[np_ * s:np_ * (s + 1)])
            o_ref[t * s:t * (s + 1), :] = out[CHUNK * s:CHUNK * s + t, :]


def _rwkv(tok, s0, post_w, n_seq, seq, tile_rows, chain):
    if chain:
        groups, nt, ns = n_seq, seq // tile_rows, None
    else:
        groups, nt, ns = n_seq * seq // tile_rows, 1, tile_rows // seq
    row = lambda g, i: (g * nt + i, 0)
    const = lambda g, i: (0, 0)
    state_spec = pl.BlockSpec((ns, RWKV_HEADS, HEAD_DIM, HEAD_DIM), lambda g, i: (g, 0, 0, 0))
    wide = pl.BlockSpec((tile_rows, RWKV_W), row)
    state_in = [] if s0 is None else [s0]
    return pl.pallas_call(
        functools.partial(_rwkv_kernel, chain=chain, zero_state=s0 is None, t=seq),
        grid=(groups, nt),
        in_specs=[wide] * 7 + [state_spec] * len(state_in) + [pl.BlockSpec(w.shape, const) for w in post_w],
        out_specs=[wide, state_spec],
        out_shape=[
            jax.ShapeDtypeStruct((n_seq * seq, RWKV_W), F32),
            jax.ShapeDtypeStruct((n_seq, RWKV_HEADS, HEAD_DIM, HEAD_DIM), F32),
        ],
        scratch_shapes=[pltpu.VMEM((RWKV_HEADS // 2, HEAD_DIM, LANES), F32)],
        compiler_params=pltpu.CompilerParams(
            dimension_semantics=("parallel", "arbitrary"), vmem_limit_bytes=VMEM_LIMIT),
        name="rwkv",
    )(*tok, *state_in, *post_w)


def _mixers_prompt_kernel(q_ref, k_ref, v_ref, kh_ref, vh_ref, sink_ref, *refs, tq):
    *rwkv_refs, att_ref, rw_ref, sout_ref, s_scr = refs
    attention = functools.partial(_attn_prompt_kernel, q_ref, k_ref, v_ref, kh_ref, vh_ref, sink_ref,
                                  att_ref, tq=tq)
    _rwkv_kernel(*rwkv_refs, rw_ref, sout_ref, s_scr, chain=True, zero_state=True, t=tq,
                 companion=attention)


def _mixers_prompt(q, k, v, sinks, tok, post_w, batch, seq, tq):
    sink_tile = _sink_tile(sinks, CHUNK, WINDOW + CHUNK)
    nt = seq // tq
    row = lambda b, i: (b * nt + i, 0)
    const = lambda b, i: (0, 0)
    halo = lambda b, i: (jnp.maximum((b * nt + i) * (tq // WINDOW) - 1, 0), 0)
    wide = pl.BlockSpec((tq, RWKV_W), row)
    state_spec = pl.BlockSpec((None, RWKV_HEADS, HEAD_DIM, HEAD_DIM), lambda b, i: (b, 0, 0, 0))
    return pl.pallas_call(
        functools.partial(_mixers_prompt_kernel, tq=tq),
        grid=(batch, nt),
        in_specs=[
            pl.BlockSpec((tq, ATT_Q), row),
            pl.BlockSpec((tq, ATT_KV), row),
            pl.BlockSpec((tq, ATT_KV), row),
            pl.BlockSpec((WINDOW, ATT_KV), halo),
            pl.BlockSpec((WINDOW, ATT_KV), halo),
            pl.BlockSpec((ATT_HEADS * CHUNK, LANES), const),
        ] + [wide] * 7 + [pl.BlockSpec(w.shape, const) for w in post_w],
        out_specs=[pl.BlockSpec((tq, ATT_Q), row), wide, state_spec],
        out_shape=[
            jax.ShapeDtypeStruct((batch * seq, ATT_Q), F32),
            jax.ShapeDtypeStruct((batch * seq, RWKV_W), F32),
            jax.ShapeDtypeStruct((batch, RWKV_HEADS, HEAD_DIM, HEAD_DIM), F32),
        ],
        scratch_shapes=[pltpu.VMEM((RWKV_HEADS // 2, HEAD_DIM, LANES), F32)],
        compiler_params=pltpu.CompilerParams(
            dimension_semantics=("parallel", "arbitrary"), vmem_limit_bytes=VMEM_LIMIT),
        name="mixers",
    )(q, k, v, k, v, sink_tile, *tok, *post_w)


def _ffn_kernel(x_ref, att_ref, rw_ref, woa_ref, wor_ref, g2_ref, wup_ref, wdn_ref, o_ref, *, tf):
    x2 = x_ref[...] + _dot(att_ref[...], woa_ref[...]) + _dot(rw_ref[...], wor_ref[...])
    h = (x2 * g2_ref[...]).astype(BF16)
    us = []
    for j in range(wup_ref.shape[1] // tf):
        u = jnp.maximum(lax.dot_general(h, wup_ref[:, tf * j:tf * (j + 1)], NN,
                                        preferred_element_type=F32), 0.0)
        us.append((u * u).astype(BF16))
    u_all = jnp.concatenate(us, axis=1)
    inv_ms = 1.0 / (jnp.mean(x2 * x2, axis=-1, keepdims=True) + NORM_EPS)
    o_ref[...] = x2 + inv_ms * lax.dot_general(u_all, wdn_ref[...], NN, preferred_element_type=F32)


def _out_ffn(x2d, att, rw, wo_att, wo_rw, g2, w_up, w_down, tm, tf):
    n, d = x2d.shape
    dff = w_up.shape[1]
    row = lambda i: (i, 0)
    const = lambda i: (0, 0)
    return pl.pallas_call(
        functools.partial(_ffn_kernel, tf=tf),
        grid=(n // tm,),
        in_specs=[
            pl.BlockSpec((tm, d), row),
            pl.BlockSpec((tm, ATT_Q), row),
            pl.BlockSpec((tm, RWKV_W), row),
            _resident((ATT_Q, d), const),
            _resident((RWKV_W, d), const),
            _resident((1, d), const),
            _resident((d, dff), const),
            _resident((dff, d), const),
        ],
        out_specs=pl.BlockSpec((tm, d), row),
        out_shape=jax.ShapeDtypeStruct((n, d), F32),
        compiler_params=pltpu.CompilerParams(
            dimension_semantics=("parallel",), vmem_limit_bytes=FFN_VMEM_LIMIT),
        name="out_ffn",
    )(x2d, att, rw, wo_att, wo_rw, g2, w_up, w_down)


def _rope_tables(pos, reps=1):
    half = HEAD_DIM // 2
    inv = ROPE_THETA ** (-np.arange(half, dtype=np.float64) / half)
    ang = np.asarray(pos, np.float64)[:, None] * inv[None, :]
    cos = np.cos(ang)
    sin = np.sin(ang)
    zero = np.zeros_like(sin)
    lay = lambda lo, hi: jnp.asarray(np.tile(np.concatenate([lo, hi, lo, hi], axis=1), (reps, 1)), F32)
    return lay(cos, cos), lay(-sin, zero), lay(zero, sin)


def _pair_perm():
    idx = []
    for j in range(ATT_HEADS // 2):
        idx += list(range(HEAD_DIM * j, HEAD_DIM * (j + 1)))
        idx += list(range(HEAD_DIM * (j + 4), HEAD_DIM * (j + 5)))
    return np.asarray(idx, dtype=np.int32)


def _pad_rw(t):
    return jnp.pad(t, [(0, 0)] * (t.ndim - 1) + [(0, RW_COLS - RW_REAL)])


def _layer(x, tabs, k_past, v_past, shift_prev, wkv0, lw, tm):
    (att_w, rw_pre_w, rw_post_w, sink8, wo_att, wo_rw, g2, w_up, w_down) = lw
    b, t, d = x.shape
    n = b * t
    x2d = x.reshape(n, d)
    q, k, v, *tok, tail = _inproj(x2d, _pad_rw(shift_prev)[:, None, :], att_w, rw_pre_w, tabs, tm, t)

    if k_past is None:
        att, rw, wkv = _mixers_prompt(q, k, v, sink8, tok, rw_post_w, b, t, min(t, 2 * WKV_TILE))
        rows = min(WINDOW, t)
        last = lambda a: a.reshape(b, t, ATT_KV)[:, t - rows:].reshape(b, rows, ATT_KV_HEADS, HEAD_DIM)
        new_k, new_v = last(k), last(v)
        shift_out = tail[:, SUBLANES - 1, :RW_REAL]
    else:
        per_tile = max(m for m in (8, 4, 2, 1) if b % m == 0)
        att = _attn_sample(q, k, v, k_past.reshape(-1, ATT_KV), v_past.reshape(-1, ATT_KV),
                           sink8, b, t, per_tile)
        new_k = k.reshape(b, t, ATT_KV_HEADS, HEAD_DIM)
        new_v = v.reshape(b, t, ATT_KV_HEADS, HEAD_DIM)
        shift_out = tail.reshape(b, t, RW_COLS)[:, t - 1, :RW_REAL]
        rw, wkv = _rwkv(tok, wkv0, rw_post_w, b, t, per_tile * t, False)
    y = _out_ffn(x2d, att, rw, wo_att, wo_rw, g2, w_up, w_down, min(n, 1024), 1024)
    return (y.reshape(b, t, d), new_k, new_v, wkv, shift_out)


def kernel(x_prompt, x_sample, cache_attn_k, cache_attn_v, state_rwkv_wkv, state_rwkv_shift, ln1_g, w_in, q_norm_g, k_norm_g, attn_sinks, shift_mu, decay_w0, decay_w2, iclr_a0, iclr_a2, gate_g2, k_k, k_a, r_k, lnx_g, lnx_b, w_out, ln2_g, w_up, w_down):
    bp, tp, d = x_prompt.shape
    bs, ts, _ = x_sample.shape
    depth = w_in.shape[0]
    perm = _pair_perm()
    tabs_p = _rope_tables(np.arange(tp))
    tabs_s = _rope_tables(PAST_LEN + np.arange(ts), reps=bs)
    tm_p = min(512, bp * tp)
    tm_s = min(512, bs * ts)

    hp, hs = x_prompt, x_sample
    outs_p, outs_s = [], []
    for l in range(depth):
        wl = w_in[l]
        att_cols = ATT_Q + 2 * ATT_KV
        w_in_p = jnp.concatenate(
            [wl[:, :ATT_Q][:, perm], wl[:, ATT_Q:att_cols], _pad_rw(wl[:, att_cols:])], axis=1).astype(BF16)
        row2 = lambda t: t.reshape(1, -1)
        zeros64 = jnp.zeros((64, RWKV_W), F32)
        att_w = (row2(ln1_g[l]), w_in_p, row2(jnp.tile(q_norm_g[l], ATT_HEADS)),
                 row2(jnp.tile(k_norm_g[l], ATT_KV_HEADS)))
        rw_pre_w = (
            row2(_pad_rw(shift_mu[l])),
            row2(decay_w0[l]),
            jnp.concatenate([decay_w2[l], zeros64], axis=0).astype(BF16),
            row2(iclr_a0[l]),
            jnp.concatenate([zeros64, iclr_a2[l]], axis=0).astype(BF16),
            jnp.pad(gate_g2[l], ((0, GATE_PAD - gate_g2.shape[1]), (0, 0))).astype(BF16),
            row2(k_k[l]), row2(k_a[l]),
        )
        rw_post_w = (row2(r_k[l]), row2(lnx_g[l]), row2(lnx_b[l]))
        lw = (att_w, rw_pre_w, rw_post_w, attn_sinks[l],
              w_out[l][:ATT_Q][perm].astype(BF16), w_out[l][ATT_Q:].astype(BF16),
              row2(ln2_g[l]), w_up[l].astype(BF16), w_down[l].astype(BF16))
        zero_shift = jnp.zeros((bp, RW_REAL), F32)
        hp, *op = _layer(hp, tabs_p, None, None, zero_shift, None, lw, tm_p)
        hs, *os_ = _layer(hs, tabs_s, cache_attn_k[l], cache_attn_v[l], state_rwkv_shift[l],
                          state_rwkv_wkv[l], lw, tm_s)
        outs_p.append(op)
        outs_s.append(os_)
    stack = lambda outs, i: jnp.stack([o[i] for o in outs])
    return (hp, hs,
            stack(outs_p, 0), stack(outs_p, 1), stack(outs_p, 2), stack(outs_p, 3),
            stack(outs_s, 0), stack(outs_s, 1), stack(outs_s, 2), stack(outs_s, 3))
```
